```python
import math
import jax
import jax.numpy as jnp
from jax import lax
import numpy as np

D_MODEL = 1024
BATCH = 4
SEQ = 8192
DEPTH = 2

CTX_LEN = 256
GRID_W = 64
EPS = 1e-6

NA_HEADS = 8
NA_HEAD_DIM = 64
NA_WIN_H = 8
NA_WIN_W = 16
SSM_HEADS = 4
SSM_HEAD_DIM = 64
SSM_GROUPS = 2
SSM_STATE = 128
SSM_CONV = 5
SSM_CHUNK = 128
ML_HEADS = 4
ML_HEAD_DIM = 64
ML_CHUNK = 128
ROPE_THETA = 10000.0
N_EXPERTS = 32
TOP_K = 4
D_FF = 1024
SWIGLU_ALPHA = 1.702
SWIGLU_LIMIT = 7.0

NA_WIDTH = NA_HEADS * NA_HEAD_DIM
SSM_WIDTH = SSM_HEADS * SSM_HEAD_DIM
SSM_BC = SSM_GROUPS * SSM_STATE
SSM_CONV_CH = SSM_WIDTH + 2 * SSM_BC
ML_WIDTH = ML_HEADS * ML_HEAD_DIM
D_MIX = NA_WIDTH + SSM_WIDTH + ML_WIDTH
IN_SPLITS = (NA_WIDTH, NA_WIDTH, NA_WIDTH, SSM_CONV_CH, SSM_WIDTH, 2 * SSM_HEADS,
             ML_WIDTH, ML_WIDTH, ML_WIDTH, ML_WIDTH, 2 * ML_HEADS, 2 * ML_HEADS)
IN_OFFSETS = tuple(int(o) for o in np.cumsum(IN_SPLITS)[:-1])
D_IN = sum(IN_SPLITS)

kernel_name = "hybrid_natten_ssd_mlstm_moe_dit"


def rms_norm(x, g):
    xf = x.astype(jnp.float32)
    y = xf * lax.rsqrt(jnp.mean(xf * xf, axis=-1, keepdims=True) + EPS)
    return (y * g.astype(jnp.float32)).astype(x.dtype)


def modulate(x, shift, scale):
    return x * (1 + scale) + shift


def ada_params(cond, w, b):
    mod = jax.nn.silu(cond) @ w + b
    return [m[..., None, :] for m in jnp.split(mod, 6, axis=-1)]


def split_heads(t, n_heads):
    return t.reshape(t.shape[0], t.shape[1], n_heads, -1)


def orient(t, flip):
    return jnp.flip(t, axis=1) if flip else t


def to_chunks(t, size):
    return t.reshape((t.shape[0], t.shape[1] // size, size) + t.shape[2:]).astype(jnp.float32)


def tri_mask(n):
    return jnp.tril(jnp.ones((n, n), dtype=bool))


def axial_rope(n_tokens, head_dim):
    pos = jnp.arange(n_tokens, dtype=jnp.int32)
    row = (pos // GRID_W).astype(jnp.float32)
    col = (pos % GRID_W).astype(jnp.float32)
    n_freq = head_dim // 4
    inv_freq = ROPE_THETA ** (-jnp.arange(n_freq, dtype=jnp.float32) / n_freq)
    ang = jnp.concatenate([row[:, None] * inv_freq, col[:, None] * inv_freq], axis=-1)
    return jnp.cos(ang), jnp.sin(ang)


def apply_rope(x, cos, sin):
    xf = x.astype(jnp.float32)
    x1, x2 = jnp.split(xf, 2, axis=-1)
    cos = cos[None, :, None, :]
    sin = sin[None, :, None, :]
    return jnp.concatenate([x1 * cos - x2 * sin, x1 * sin + x2 * cos], axis=-1).astype(x.dtype)


def window_starts(n, win):
    pos = jnp.arange(n, dtype=jnp.int32)
    return jnp.clip(pos - win // 2, 0, n - win)


def neighborhood_attention(q, k, v, k_ctx, v_ctx, rpb, rows):
    bsz, L, H, d = q.shape
    win_h = min(NA_WIN_H, rows)
    scale = d ** -0.5
    kg = k.reshape(bsz, rows, GRID_W, H, d)
    vg = v.reshape(bsz, rows, GRID_W, H, d)
    q_rows = jnp.moveaxis(q.reshape(bsz, rows, GRID_W, H, d), 1, 0)
    row_start = window_starts(rows, win_h)
    col_idx = window_starts(GRID_W, NA_WIN_W)[:, None] + jnp.arange(NA_WIN_W, dtype=jnp.int32)
    dc = col_idx - jnp.arange(GRID_W, dtype=jnp.int32)[:, None] + (NA_WIN_W - 1)
    n_win = win_h * NA_WIN_W

    def one_row(args):
        q_r, r, r0 = args
        k_win = lax.dynamic_slice_in_dim(kg, r0, win_h, axis=1)[:, :, col_idx]
        v_win = lax.dynamic_slice_in_dim(vg, r0, win_h, axis=1)[:, :, col_idx]
        dr = r0 + jnp.arange(win_h, dtype=jnp.int32) - r + (NA_WIN_H - 1)
        bias = rpb[:, dr[:, None, None], dc[None]]
        s_win = (jnp.einsum('bqhd,brqwhd->bhqrw', q_r, k_win).astype(jnp.float32) * scale
                 + jnp.transpose(bias, (0, 2, 1, 3)).astype(jnp.float32))
        s_ctx = jnp.einsum('bqhd,bkhd->bhqk', q_r, k_ctx).astype(jnp.float32) * scale
        s = jnp.concatenate([s_win.reshape(bsz, H, GRID_W, n_win), s_ctx], axis=-1)
        p = jax.nn.softmax(s, axis=-1).astype(v.dtype)
        p_win = p[..., :n_win].reshape(bsz, H, GRID_W, win_h, NA_WIN_W)
        return (jnp.einsum('bhqrw,brqwhd->bqhd', p_win, v_win)
                + jnp.einsum('bhqk,bkhd->bqhd', p[..., n_win:], v_ctx))

    out = lax.map(one_row, (q_rows, jnp.arange(rows, dtype=jnp.int32), row_start))
    return jnp.moveaxis(out, 0, 1).reshape(bsz, L, H, d)


def context_attention(q, k, v):
    scale = q.shape[-1] ** -0.5
    s = jnp.einsum('bqhd,bkhd->bhqk', q, k).astype(jnp.float32) * scale
    p = jax.nn.softmax(s, axis=-1).astype(v.dtype)
    return jnp.einsum('bhqk,bkhd->bqhd', p, v)


def depthwise_conv(x, w, b):
    y = lax.conv_general_dilated(x, w[:, None, :], window_strides=(1,),
                                 padding=[(SSM_CONV // 2, SSM_CONV // 2)],
                                 dimension_numbers=('NWC', 'WIO', 'NWC'),
                                 feature_group_count=x.shape[-1])
    return y + b


def mamba2_inputs(xbc, dt_raw, conv_w, conv_b, dt_bias):
    bsz, L, _ = xbc.shape
    xbc = jax.nn.silu(depthwise_conv(xbc, conv_w, conv_b))
    xs, bm, cm = jnp.split(xbc, (SSM_WIDTH, SSM_WIDTH + SSM_BC), axis=-1)
    rep = SSM_HEADS // SSM_GROUPS
    bm = jnp.repeat(bm.reshape(bsz, L, SSM_GROUPS, SSM_STATE), rep, axis=2)
    cm = jnp.repeat(cm.reshape(bsz, L, SSM_GROUPS, SSM_STATE), rep, axis=2)
    dt = jax.nn.softplus(dt_raw.reshape(bsz, L, 2, SSM_HEADS).astype(jnp.float32)
                         + dt_bias.astype(jnp.float32))
    return xs.reshape(bsz, L, SSM_HEADS, SSM_HEAD_DIM), bm, cm, dt


def ssd_scan(xs, dt, a_coef, bm, cm, h0, need_y):
    bsz, L, H, P = xs.shape
    xc, bc, cc, dtc = (to_chunks(t, SSM_CHUNK) for t in (xs, bm, cm, dt))
    la = jnp.swapaxes(dtc * a_coef.astype(jnp.float32), 2, 3)
    acum = jnp.cumsum(la, axis=-1)
    xdt = xc * dtc[..., None]
    states = jnp.einsum('bchq,bcqhn,bcqhp->bchpn', jnp.exp(acum[..., -1:] - acum), bc, xdt)
    decay = jnp.exp(acum[..., -1])

    def step(h, inp):
        s, g = inp
        return g[..., None, None] * h + s, h

    h_final, h_in = lax.scan(step, h0, (jnp.moveaxis(states, 1, 0), jnp.moveaxis(decay, 1, 0)))
    if not need_y:
        return None, h_final
    h_in = jnp.moveaxis(h_in, 0, 1)
    lmat = jnp.exp(jnp.where(tri_mask(SSM_CHUNK), acum[..., :, None] - acum[..., None, :], -jnp.inf))
    scores = jnp.einsum('bcihn,bcjhn->bchij', cc, bc) * lmat
    y = jnp.einsum('bchij,bcjhp->bcihp', scores, xdt)
    y = y + jnp.einsum('bcihn,bchpn->bcihp', cc, h_in) * jnp.swapaxes(jnp.exp(acum), 2, 3)[..., None]
    return y.reshape(bsz, L, H, P), h_final


def ssd_bidirectional(lat, ctx, a_log, d_skip, ctx_out):
    a_coef = -jnp.exp(a_log.astype(jnp.float32))
    bsz = lat[0].shape[0]
    outs_l, outs_c = [], []
    for direction in range(2):
        flip = direction == 1
        xs_c, b_c, c_c, dt_c = (orient(t, flip) for t in ctx)
        xs_l, b_l, c_l, dt_l = (orient(t, flip) for t in lat)
        h0 = jnp.zeros((bsz, SSM_HEADS, SSM_HEAD_DIM, SSM_STATE), jnp.float32)
        y_c, h_c = ssd_scan(xs_c, dt_c[:, :, direction], a_coef[direction], b_c, c_c, h0, ctx_out)
        y_l, _ = ssd_scan(xs_l, dt_l[:, :, direction], a_coef[direction], b_l, c_l, h_c, True)
        skip = d_skip[direction].astype(jnp.float32)[:, None]
        outs_l.append(orient(y_l, flip) + lat[0].astype(jnp.float32) * skip)
        if ctx_out:
            outs_c.append(orient(y_c, flip) + ctx[0].astype(jnp.float32) * skip)
    y_ctx = outs_c[0] + outs_c[1] if ctx_out else None
    return outs_l[0] + outs_l[1], y_ctx


def mlstm_gate_logs(i_raw, f_raw, gate_b):
    bsz, L, _ = i_raw.shape
    li = i_raw.reshape(bsz, L, 2, ML_HEADS).astype(jnp.float32) + gate_b[:, 0].astype(jnp.float32)
    lf = jax.nn.log_sigmoid(f_raw.reshape(bsz, L, 2, ML_HEADS).astype(jnp.float32)
                            + gate_b[:, 1].astype(jnp.float32))
    return li, lf


def mlstm_scan(q, k, v, li, lf, state0, need_h):
    bsz, L, H, d = q.shape
    qc = to_chunks(q, ML_CHUNK) * d ** -0.5
    kc, vc = to_chunks(k, ML_CHUNK), to_chunks(v, ML_CHUNK)
    li_c = jnp.swapaxes(to_chunks(li, ML_CHUNK), 2, 3)
    bcum = jnp.cumsum(jnp.swapaxes(to_chunks(lf, ML_CHUNK), 2, 3), axis=-1)
    g = bcum[..., -1]
    w_end = g[..., None] - bcum + li_c
    m_loc = jnp.max(w_end, axis=-1)
    e_end = jnp.exp(w_end - m_loc[..., None])
    c_loc = jnp.einsum('bchj,bcjhe,bcjhd->bched', e_end, vc, kc)
    n_loc = jnp.einsum('bchj,bcjhd->bchd', e_end, kc)

    def step(state, inp):
        C, n, m = state
        cl, nl, ml, gc = inp
        m_new = jnp.maximum(gc + m, ml)
        a_old = jnp.exp(gc + m - m_new)
        a_new = jnp.exp(ml - m_new)
        return ((a_old[..., None, None] * C + a_new[..., None, None] * cl,
                 a_old[..., None] * n + a_new[..., None] * nl, m_new), (C, n, m))

    final, (c_in, n_in, m_in) = lax.scan(
        step, state0, tuple(jnp.moveaxis(t, 1, 0) for t in (c_loc, n_loc, m_loc, g)))
    if not need_h:
        return None, final
    c_in, n_in, m_in = (jnp.moveaxis(t, 0, 1) for t in (c_in, n_in, m_in))
    log_d = jnp.where(tri_mask(ML_CHUNK),
                      bcum[..., :, None] - bcum[..., None, :] + li_c[..., None, :], -jnp.inf)
    log_inter = bcum + m_in[..., None]
    m_i = jnp.maximum(jnp.max(log_d, axis=-1), log_inter)
    s = jnp.einsum('bcihd,bcjhd->bchij', qc, kc) * jnp.exp(log_d - m_i[..., None])
    inter = jnp.swapaxes(jnp.exp(log_inter - m_i), 2, 3)
    num = (jnp.einsum('bchij,bcjhe->bcihe', s, vc)
           + jnp.einsum('bched,bcihd->bcihe', c_in, qc) * inter[..., None])
    den = (jnp.swapaxes(jnp.sum(s, axis=-1), 2, 3)
           + jnp.einsum('bchd,bcihd->bcih', n_in, qc) * inter)
    floor = jnp.swapaxes(jnp.exp(-m_i), 2, 3)
    h = num / jnp.maximum(jnp.abs(den), floor)[..., None]
    return h.reshape(bsz, L, H, d), final


def mlstm_bidirectional(lat, ctx, ctx_out):
    bsz = lat[0].shape[0]
    outs_l, outs_c = [], []
    for direction in range(2):
        flip = direction == 1
        q_c, k_c, v_c, li_c, lf_c = (orient(t, flip) for t in ctx)
        q_l, k_l, v_l, li_l, lf_l = (orient(t, flip) for t in lat)
        state0 = (jnp.zeros((bsz, ML_HEADS, ML_HEAD_DIM, ML_HEAD_DIM), jnp.float32),
                  jnp.zeros((bsz, ML_HEADS, ML_HEAD_DIM), jnp.float32),
                  jnp.zeros((bsz, ML_HEADS), jnp.float32))
        h_c, s_c = mlstm_scan(q_c, k_c, v_c, li_c[:, :, direction], lf_c[:, :, direction], state0, ctx_out)
        h_l, _ = mlstm_scan(q_l, k_l, v_l, li_l[:, :, direction], lf_l[:, :, direction], s_c, True)
        outs_l.append(orient(h_l, flip))
        if ctx_out:
            outs_c.append(orient(h_c, flip))
    h_ctx = outs_c[0] + outs_c[1] if ctx_out else None
    return outs_l[0] + outs_l[1], h_ctx


def merge_groups(ya, yb, z, yc, o, ssm_g, ml_g, w_out):
    bsz, L = ya.shape[:2]
    dtype = z.dtype
    yb = rms_norm(yb.reshape(bsz, L, SSM_WIDTH).astype(dtype) * jax.nn.silu(z), ssm_g)
    yc = jax.nn.sigmoid(o).reshape(bsz, L, ML_HEADS, ML_HEAD_DIM) * yc.astype(dtype)
    yc = rms_norm(yc, ml_g.reshape(ML_HEADS, ML_HEAD_DIM)).reshape(bsz, L, ML_WIDTH)
    return jnp.concatenate([ya.reshape(bsz, L, NA_WIDTH), yb, yc], axis=-1) @ w_out


def token_mixer(u_l, u_c, w_in, qk_g, rpb, conv_w, conv_b, dt_bias, a_log, d_skip, ssm_g,
                ml_gate_b, ml_g, w_out, rope_cos, rope_sin, rows, ctx_out):
    pl = jnp.split(u_l @ w_in, IN_OFFSETS, axis=-1)
    pc = jnp.split(u_c @ w_in, IN_OFFSETS, axis=-1)

    qa_l = rms_norm(split_heads(pl[0], NA_HEADS), qk_g[0])
    ka_l = rms_norm(split_heads(pl[1], NA_HEADS), qk_g[1])
    va_l = split_heads(pl[2], NA_HEADS)
    ka_c = rms_norm(split_heads(pc[1], NA_HEADS), qk_g[1])
    va_c = split_heads(pc[2], NA_HEADS)
    ya_l = neighborhood_attention(qa_l, ka_l, va_l, ka_c, va_c, rpb, rows)

    ssm_l = mamba2_inputs(pl[3], pl[5], conv_w, conv_b, dt_bias)
    ssm_c = mamba2_inputs(pc[3], pc[5], conv_w, conv_b, dt_bias)
    yb_l, yb_c = ssd_bidirectional(ssm_l, ssm_c, a_log, d_skip, ctx_out)

    ml_l = (apply_rope(split_heads(pl[6], ML_HEADS), rope_cos, rope_sin),
            apply_rope(split_heads(pl[7], ML_HEADS), rope_cos, rope_sin),
            split_heads(pl[8], ML_HEADS)) + mlstm_gate_logs(pl[10], pl[11], ml_gate_b)
    ml_c = (split_heads(pc[6], ML_HEADS), split_heads(pc[7], ML_HEADS),
            split_heads(pc[8], ML_HEADS)) + mlstm_gate_logs(pc[10], pc[11], ml_gate_b)
    yc_l, yc_c = mlstm_bidirectional(ml_l, ml_c, ctx_out)

    y_l = merge_groups(ya_l, yb_l, pl[4], yc_l, pl[9], ssm_g, ml_g, w_out)
    if not ctx_out:
        return y_l, None
    qa_c = rms_norm(split_heads(pc[0], NA_HEADS), qk_g[0])
    ya_c = context_attention(qa_c, ka_c, va_c)
    y_c = merge_groups(ya_c, yb_c, pc[4], yc_c, pc[9], ssm_g, ml_g, w_out)
    return y_l, y_c


def moe_ffn(t, router_w, router_b, w1, b1, w2, b2):
    logits = (t @ router_w + router_b).astype(jnp.float32)
    top_vals, top_idx = lax.top_k(logits, TOP_K)
    top_w = jax.nn.softmax(top_vals, axis=-1)
    gates = jnp.sum(jax.nn.one_hot(top_idx, N_EXPERTS, dtype=jnp.float32) * top_w[..., None],
                    axis=1).astype(t.dtype)
    out = jnp.zeros_like(t)
    for e in range(N_EXPERTS):
        x_glu, x_lin = jnp.split(t @ w1[e] + b1[e], 2, axis=-1)
        x_glu = jnp.minimum(x_glu, SWIGLU_LIMIT)
        x_lin = jnp.clip(x_lin, -SWIGLU_LIMIT, SWIGLU_LIMIT)
        act = x_glu * jax.nn.sigmoid(SWIGLU_ALPHA * x_glu) * (x_lin + 1)
        out = out + gates[:, e:e + 1] * (act @ w2[e] + b2[e])
    return out


def setup_inputs(seed: int = 0) -> dict:
    key = jax.random.key(seed)
    ks = jax.random.split(key, 26)
    f32 = jnp.float32

    def normal(k, shape, std):
        return jax.random.normal(k, shape, f32) * std

    dt0 = jnp.exp(jax.random.uniform(ks[12], (DEPTH, 2, SSM_HEADS), f32, math.log(1e-3), math.log(1e-1)))
    input_b = normal(ks[16], (DEPTH, 2, ML_HEADS), 0.1)
    forget_b = jnp.linspace(3.0, 6.0, ML_HEADS, dtype=f32) + normal(ks[17], (DEPTH, 2, ML_HEADS), 0.1)
    return {
        "x": normal(ks[0], (BATCH, SEQ, D_MODEL), 1.0),
        "c": normal(ks[1], (BATCH, D_MODEL), 1.0),
        "ctx": normal(ks[2], (BATCH, CTX_LEN, D_MODEL), 1.0),
        "c_ctx": normal(ks[3], (D_MODEL,), 1.0),
        "ada_w": normal(ks[4], (DEPTH, D_MODEL, 6 * D_MODEL), 0.5 * D_MODEL ** -0.5),
        "ada_b": normal(ks[5], (DEPTH, 6 * D_MODEL), 0.1),
        "norm_g": 1.0 + normal(ks[6], (DEPTH, 2, D_MODEL), 0.05),
        "w_in": normal(ks[7], (DEPTH, D_MODEL, D_IN), D_MODEL ** -0.5),
        "na_qk_g": 1.0 + normal(ks[8], (DEPTH, 2, NA_HEAD_DIM), 0.05),
        "na_rpb": normal(ks[9], (DEPTH, NA_HEADS, 2 * NA_WIN_H - 1, 2 * NA_WIN_W - 1), 0.5),
        "ssm_conv_w": normal(ks[10], (DEPTH, SSM_CONV, SSM_CONV_CH), SSM_CONV ** -0.5),
        "ssm_conv_b": normal(ks[11], (DEPTH, SSM_CONV_CH), 0.05),
        "ssm_dt_bias": dt0 + jnp.log(-jnp.expm1(-dt0)),
        "ssm_a_log": jnp.log(jax.random.uniform(ks[13], (DEPTH, 2, SSM_HEADS), f32, 1.0, 16.0)),
        "ssm_d": 1.0 + normal(ks[14], (DEPTH, 2, SSM_HEADS), 0.1),
        "ssm_norm_g": 1.0 + normal(ks[15], (DEPTH, SSM_WIDTH), 0.05),
        "ml_gate_b": jnp.stack([input_b, forget_b], axis=2),
        "ml_norm_g": 1.0 + normal(ks[18], (DEPTH, ML_WIDTH), 0.05),
        "w_out": normal(ks[19], (DEPTH, D_MIX, D_MODEL), D_MIX ** -0.5),
        "router_w": normal(ks[20], (DEPTH, D_MODEL, N_EXPERTS), D_MODEL ** -0.5),
        "router_b": normal(ks[21], (DEPTH, N_EXPERTS), 0.01),
        "exp_w1": normal(ks[22], (DEPTH, N_EXPERTS, D_MODEL, 2 * D_FF), D_MODEL ** -0.5),
        "exp_b1": normal(ks[23], (DEPTH, N_EXPERTS, 2 * D_FF), 0.02),
        "exp_w2": normal(ks[24], (DEPTH, N_EXPERTS, D_FF, D_MODEL), D_FF ** -0.5),
        "exp_b2": normal(ks[25], (DEPTH, N_EXPERTS, D_MODEL), 0.02),
    }


def reference(x, c, ctx, c_ctx, ada_w, ada_b, norm_g, w_in, na_qk_g, na_rpb, ssm_conv_w, ssm_conv_b,
              ssm_dt_bias, ssm_a_log, ssm_d, ssm_norm_g, ml_gate_b, ml_norm_g, w_out,
              router_w, router_b, exp_w1, exp_b1, exp_w2, exp_b2):
    bsz, L, D = x.shape
    n_ctx = ctx.shape[1]
    rows = L // GRID_W
    rope_cos, rope_sin = axial_rope(L, ML_HEAD_DIM)
    h, hc = x, ctx
    for layer in range(DEPTH):
        ctx_out = layer < DEPTH - 1
        sh1, sc1, g1, sh2, sc2, g2 = ada_params(c, ada_w[layer], ada_b[layer])
        sh1c, sc1c, g1c, sh2c, sc2c, g2c = ada_params(c_ctx, ada_w[layer], ada_b[layer])
        u_l = modulate(rms_norm(h, norm_g[layer, 0]), sh1, sc1)
        u_c = modulate(rms_norm(hc, norm_g[layer, 0]), sh1c, sc1c)
        y_l, y_c = token_mixer(u_l, u_c, w_in[layer], na_qk_g[layer], na_rpb[layer],
                               ssm_conv_w[layer], ssm_conv_b[layer], ssm_dt_bias[layer],
                               ssm_a_log[layer], ssm_d[layer], ssm_norm_g[layer],
                               ml_gate_b[layer], ml_norm_g[layer], w_out[layer],
                               rope_cos, rope_sin, rows, ctx_out)
        h = h + g1 * y_l
        moe = (router_w[layer], router_b[layer], exp_w1[layer], exp_b1[layer], exp_w2[layer], exp_b2[layer])
        v_l = modulate(rms_norm(h, norm_g[layer, 1]), sh2, sc2).reshape(bsz * L, D)
        if ctx_out:
            hc = hc + g1c * y_c
            v_c = modulate(rms_norm(hc, norm_g[layer, 1]), sh2c, sc2c).reshape(bsz * n_ctx, D)
            f = moe_ffn(jnp.concatenate([v_l, v_c], axis=0), *moe)
            h = h + g2 * f[: bsz * L].reshape(bsz, L, D)
            hc = hc + g2c * f[bsz * L:].reshape(bsz, n_ctx, D)
        else:
            h = h + g2 * moe_ffn(v_l, *moe).reshape(bsz, L, D)
    return h
```

```python
import functools
import math

import numpy as np
import jax
import jax.numpy as jnp
from jax import lax
from jax.experimental import pallas as pl
from jax.experimental.pallas import tpu as pltpu

F32 = jnp.float32
BF16 = jnp.bfloat16

GRID_W = 64
EPS = 1e-6
NA_HEADS = 8
NA_HEAD_DIM = 64
NA_WIN_H = 8
NA_WIN_W = 16
SSM_HEADS = 4
SSM_HEAD_DIM = 64
SSM_GROUPS = 2
SSM_STATE = 128
SSM_CONV = 5
ML_HEADS = 4
ML_HEAD_DIM = 64
ROPE_THETA = 10000.0
N_EXPERTS = 32
TOP_K = 4
SWIGLU_ALPHA = 1.702
SWIGLU_LIMIT = 7.0

NA_WIDTH = NA_HEADS * NA_HEAD_DIM
SSM_WIDTH = SSM_HEADS * SSM_HEAD_DIM
SSM_BC = SSM_GROUPS * SSM_STATE
SSM_CONV_CH = SSM_WIDTH + 2 * SSM_BC
ML_WIDTH = ML_HEADS * ML_HEAD_DIM
IN_SPLITS = (NA_WIDTH, NA_WIDTH, NA_WIDTH, SSM_CONV_CH, SSM_WIDTH, 2 * SSM_HEADS,
             ML_WIDTH, ML_WIDTH, ML_WIDTH, ML_WIDTH, 2 * ML_HEADS, 2 * ML_HEADS)
IN_OFFSETS = tuple(int(o) for o in np.cumsum((0,) + IN_SPLITS))

LANE = 128
CHUNK = 128
HALO = 16
TOK_TILE = 256
NA_ROWS = 2
NA_KROWS = NA_ROWS + NA_WIN_H
BIG_W = 3 * NA_WIDTH + SSM_CONV_CH + SSM_WIDTH + 4 * ML_WIDTH
SMALL_W = 2 * LANE
NEG = -1e30
VMEM_LIMIT = 48 * 1024 * 1024

COL_Q, COL_K, COL_V = 0, NA_WIDTH, 2 * NA_WIDTH
COL_XBC = 3 * NA_WIDTH
COL_Z = COL_XBC + SSM_CONV_CH
COL_MQ = COL_Z + SSM_WIDTH
COL_MK = COL_MQ + ML_WIDTH
COL_MV = COL_MK + ML_WIDTH
COL_MO = COL_MV + ML_WIDTH


def _cparams(sem):
    return pltpu.CompilerParams(dimension_semantics=sem, vmem_limit_bytes=VMEM_LIMIT)


def _sigmoid(x):
    return 1.0 / (1.0 + jnp.exp(-x))


def _softplus(x):
    return jnp.maximum(x, 0.0) + jnp.log1p(jnp.exp(-jnp.abs(x)))


def _dot(a, b):
    return jnp.dot(a, b, preferred_element_type=F32)


def _dot_nt(a, b):
    return lax.dot_general(a, b, (((1,), (1,)), ((), ())), preferred_element_type=F32)


def _dot_tn(a, b):
    return lax.dot_general(a, b, (((0,), (0,)), ((), ())), preferred_element_type=F32)


def _split3(x):
    x1 = x.astype(BF16)
    r1 = x - x1.astype(F32)
    x2 = r1.astype(BF16)
    x3 = (r1 - x2.astype(F32)).astype(BF16)
    return x1, x2, x3


def _split2(x):
    x1 = x.astype(BF16)
    return x1, (x - x1.astype(F32)).astype(BF16)


def _segment_mean_sq(x, bd):
    hi, lo = _split2(x * x)
    return (_dot(hi, bd) + _dot(lo, bd)) * (1.0 / NA_HEAD_DIM)


def _ada_kernel(c_ref, w_ref, b_ref, o_ref):
    c = c_ref[...]
    s = c * _sigmoid(c)
    o_ref[...] = jnp.dot(s, w_ref[...], preferred_element_type=F32,
                         precision=lax.Precision.HIGHEST) + b_ref[...]


def _ada(cond, w, b):
    n, d = cond.shape
    wcols = w.shape[1]
    bn = wcols // 4
    return pl.pallas_call(
        _ada_kernel,
        out_shape=jax.ShapeDtypeStruct((n, wcols), F32),
        grid=(4,),
        in_specs=[pl.BlockSpec((n, d), lambda j: (0, 0)),
                  pl.BlockSpec((d, bn), lambda j: (0, j)),
                  pl.BlockSpec((1, bn), lambda j: (0, j))],
        out_specs=pl.BlockSpec((n, bn), lambda j: (0, j)),
        compiler_params=_cparams(("arbitrary",)),
        name="ada",
    )(cond, w, b.reshape(1, wcols))


def _inproj_kernel(h_ref, sh_ref, sc_ref, g_ref, w_ref, qkg_ref, bd_ref, cos_ref, sin_ref,
                   big_ref, small_ref):
    x = h_ref[...]
    ms = jnp.mean(x * x, axis=-1, keepdims=True)
    u = x * lax.rsqrt(ms + EPS) * g_ref[...]
    u = u * (1.0 + sc_ref[...]) + sh_ref[...]
    ub = u.astype(BF16)
    bd = bd_ref[...]

    for c0 in range(0, 2 * NA_WIDTH, 2 * LANE):
        acc = _dot(ub, w_ref[:, c0:c0 + 2 * LANE])
        msq = _segment_mean_sq(acc, bd)
        y = acc * lax.rsqrt(msq + EPS) * qkg_ref[:, c0:c0 + 2 * LANE]
        big_ref[:, c0:c0 + 2 * LANE] = y.astype(BF16)
    for c0 in range(COL_V, COL_MQ, 2 * LANE):
        big_ref[:, c0:c0 + 2 * LANE] = _dot(ub, w_ref[:, c0:c0 + 2 * LANE]).astype(BF16)
    lane = lax.broadcasted_iota(jnp.int32, (x.shape[0], LANE), 1)
    first_half = (lane % ML_HEAD_DIM) < (ML_HEAD_DIM // 2)
    for base, scale in ((COL_MQ, ML_HEAD_DIM ** -0.5), (COL_MK, 1.0)):
        for j in range(ML_WIDTH // LANE):
            c0 = base + j * LANE
            acc = _dot(ub, w_ref[:, c0:c0 + LANE])
            swapped = jnp.where(first_half,
                                pltpu.roll(acc, LANE - ML_HEAD_DIM // 2, 1),
                                pltpu.roll(acc, ML_HEAD_DIM // 2, 1))
            y = acc * cos_ref[:, j * LANE:(j + 1) * LANE] + swapped * sin_ref[:, j * LANE:(j + 1) * LANE]
            big_ref[:, c0:c0 + LANE] = (y * scale).astype(BF16)
    for c0 in range(COL_MV, BIG_W, 2 * LANE):
        big_ref[:, c0:c0 + 2 * LANE] = _dot(ub, w_ref[:, c0:c0 + 2 * LANE]).astype(BF16)
    small_ref[...] = _dot(ub, w_ref[:, BIG_W:BIG_W + SMALL_W])


def _inproj(h, sh, sc, g, w, qkg, bd, cosf, sinf):
    bsz, s_len, d = h.shape
    tm = TOK_TILE
    mod_spec = pl.BlockSpec((None, None, 1, d), lambda b, i: (b, jnp.minimum(i, 1), 0, 0))
    full = lambda shape: pl.BlockSpec(shape, lambda b, i: (0,) * len(shape))
    return pl.pallas_call(
        _inproj_kernel,
        out_shape=(jax.ShapeDtypeStruct((bsz, s_len, BIG_W), BF16),
                   jax.ShapeDtypeStruct((bsz, s_len, SMALL_W), F32)),
        grid=(bsz, s_len // tm),
        in_specs=[pl.BlockSpec((None, tm, d), lambda b, i: (b, i, 0)),
                  mod_spec, mod_spec, full((1, d)), full(w.shape), full(qkg.shape), full(bd.shape),
                  pl.BlockSpec((tm, ML_WIDTH), lambda b, i: (i, 0)),
                  pl.BlockSpec((tm, ML_WIDTH), lambda b, i: (i, 0))],
        out_specs=(pl.BlockSpec((None, tm, BIG_W), lambda b, i: (b, i, 0)),
                   pl.BlockSpec((None, tm, SMALL_W), lambda b, i: (b, i, 0))),
        compiler_params=_cparams(("arbitrary", "arbitrary")),
        name="inproj",
    )(h, sh, sc, g, w, qkg, bd, cosf, sinf)


def _na_kernel(q_ref, k0, k1, k2, k3, k4, v0, v1, v2, v3, v4, kc_ref, vc_ref, bias_ref, o_ref):
    nq = q_ref.shape[0]
    lane = lax.broadcasted_iota(jnp.int32, (nq, LANE), 1)
    low = lane < NA_HEAD_DIM
    zero = jnp.zeros((), BF16)
    for p in range(NA_WIDTH // LANE):
        cs = slice(p * LANE, (p + 1) * LANE)
        qp = q_ref[:, cs]
        q01 = jnp.concatenate([jnp.where(low, qp, zero), jnp.where(low, zero, qp)], axis=0)
        kw = jnp.concatenate([r[:, cs] for r in (k0, k1, k2, k3, k4)], axis=0)
        vw = jnp.concatenate([r[:, cs] for r in (v0, v1, v2, v3, v4)], axis=0)
        s_w = _dot_nt(q01, kw) + bias_ref[p]
        s_c = _dot_nt(q01, kc_ref[:, cs])
        m = jnp.maximum(jnp.max(s_w, axis=-1, keepdims=True), jnp.max(s_c, axis=-1, keepdims=True))
        p_w = jnp.exp(s_w - m)
        p_c = jnp.exp(s_c - m)
        den = jnp.sum(p_w, axis=-1, keepdims=True) + jnp.sum(p_c, axis=-1, keepdims=True)
        o01 = (_dot(p_w.astype(BF16), vw) + _dot(p_c.astype(BF16), vc_ref[:, cs])) / den
        o_ref[:, cs] = jnp.where(low, o01[:nq], o01[nq:]).astype(o_ref.dtype)


def _attention(big, bias, n_ctx, rows, ctx_out):
    bsz, s_len, _ = big.shape
    nq = NA_ROWS * GRID_W
    ncb = n_ctx // nq
    nrp = rows // NA_ROWS
    first = 0 if ctx_out else ncb
    n_win = NA_KROWS // NA_ROWS

    def lat(i):
        return jnp.maximum(i + first - ncb, 0)

    def kv_spec(j, col):
        return pl.BlockSpec(
            (None, nq, NA_WIDTH),
            lambda b, i: (b, ncb + jnp.clip(lat(i) - 2, 0, nrp - n_win) + j, col))

    def bias_type(b, i):
        step = i + first
        r = lat(i)
        t = jnp.where(r < 2, r, jnp.where(r >= nrp - 2, r - (nrp - 2) + 3, 2))
        return (jnp.where(step < ncb, 5, t), 0, 0, 0)

    return pl.pallas_call(
        _na_kernel,
        out_shape=jax.ShapeDtypeStruct((bsz, s_len, NA_WIDTH), BF16),
        grid=(bsz, ncb + nrp - first),
        in_specs=[pl.BlockSpec((None, nq, NA_WIDTH), lambda b, i: (b, i + first, 0))]
                 + [kv_spec(j, 1) for j in range(n_win)] + [kv_spec(j, 2) for j in range(n_win)]
                 + [pl.BlockSpec((None, n_ctx, NA_WIDTH), lambda b, i: (b, 0, 1)),
                    pl.BlockSpec((None, n_ctx, NA_WIDTH), lambda b, i: (b, 0, 2)),
                    pl.BlockSpec((None, NA_WIDTH // LANE, 2 * nq, NA_KROWS * GRID_W), bias_type)],
        out_specs=pl.BlockSpec((None, nq, NA_WIDTH), lambda b, i: (b, i + first, 0)),
        compiler_params=_cparams(("arbitrary", "arbitrary")),
        name="attention",
    )(big, *([big] * (2 * n_win + 2)), bias)


def _attention_bias(rpb, rows):
    nrp = rows // NA_ROWS
    n_win = NA_KROWS // NA_ROWS
    reps = np.array([0, 1, 2, nrp - 2, nrp - 1])
    qr = (NA_ROWS * reps)[:, None] + np.arange(NA_ROWS)[None, :]
    ws = NA_ROWS * np.clip(reps - 2, 0, nrp - n_win)
    kr = ws[:, None] + np.arange(NA_KROWS)[None, :]
    r0 = np.clip(qr - NA_WIN_H // 2, 0, rows - NA_WIN_H)
    ok_r = (kr[:, None, :] >= r0[:, :, None]) & (kr[:, None, :] < r0[:, :, None] + NA_WIN_H)
    dr = np.clip(kr[:, None, :] - qr[:, :, None] + (NA_WIN_H - 1), 0, 2 * NA_WIN_H - 2)
    qc = np.arange(GRID_W)
    c0 = np.clip(qc - NA_WIN_W // 2, 0, GRID_W - NA_WIN_W)
    kc = np.arange(GRID_W)
    ok_c = (kc[None, :] >= c0[:, None]) & (kc[None, :] < c0[:, None] + NA_WIN_W)
    dc = np.clip(kc[None, :] - qc[:, None] + (NA_WIN_W - 1), 0, 2 * NA_WIN_W - 2)
    g = rpb[:, dr[:, :, None, :, None], dc[None, None, :, None, :]]
    ok = ok_r[:, :, None, :, None] & ok_c[None, None, :, None, :]
    g = jnp.where(ok[None], g.astype(F32), NEG)
    g = jnp.moveaxis(g, 0, 1)
    g = g.reshape(5, NA_HEADS * NA_ROWS * GRID_W, NA_KROWS * GRID_W)
    g = g.reshape(5, NA_WIDTH // LANE, 2 * NA_ROWS * GRID_W, NA_KROWS * GRID_W)
    masked = jnp.full((1,) + g.shape[1:], NEG, F32)
    return jnp.concatenate([g, masked], axis=0)


def _scan_chunk(d, t, ncc, nc):
    bwd = jnp.where(t < ncc, ncc - 1 - t, ncc + (nc - 1 - t))
    return jnp.where(d == 0, t, bwd)


def _precedes(d):
    ii = lax.broadcasted_iota(jnp.int32, (CHUNK, CHUNK), 0)
    jj = lax.broadcasted_iota(jnp.int32, (CHUNK, CHUNK), 1)
    return (jj - ii) * (1 - 2 * d) <= 0


def _cumsum_scan_order(prec, x):
    tri = jnp.where(prec, 1.0, 0.0).astype(BF16)
    x1, x2, x3 = _split3(x)
    return _dot(tri, x1) + _dot(tri, x2) + _dot(tri, x3)


def _ssd_kernel(x_ref, xp_ref, xn_ref, sm_ref, cw_ref, cb_ref, par_ref, y_ref, st_ref, cbuf_ref,
                *, ncc, nc):
    d = pl.program_id(1)
    t = pl.program_id(2)
    chunk = _scan_chunk(d, t, ncc, nc)

    @pl.when(t == 0)
    def _():
        st_ref[...] = jnp.zeros_like(st_ref)

    prev_ok = jnp.where(jnp.logical_and(chunk != 0, chunk != ncc), 1.0, 0.0)
    next_ok = jnp.where(jnp.logical_and(chunk != ncc - 1, chunk != nc - 1), 1.0, 0.0)
    cbuf_ref[0:HALO, :] = xp_ref[...].astype(F32) * prev_ok
    cbuf_ref[HALO:HALO + CHUNK, :] = x_ref[...].astype(F32)
    cbuf_ref[HALO + CHUNK:, :] = xn_ref[...].astype(F32) * next_ok
    acc = jnp.broadcast_to(cb_ref[...], (CHUNK, SSM_CONV_CH))
    for k in range(SSM_CONV):
        r0 = HALO - SSM_CONV // 2 + k
        acc = acc + cbuf_ref[r0:r0 + CHUNK, :] * cw_ref[k:k + 1, :]
    xc = acc * _sigmoid(acc)
    xs = xc[:, :SSM_WIDTH]
    bm = xc[:, SSM_WIDTH:SSM_WIDTH + SSM_BC]
    cm = xc[:, SSM_WIDTH + SSM_BC:]

    dt = _softplus(sm_ref[...] + par_ref[0:1, :])
    la = dt * (-jnp.exp(par_ref[1:2, :]))
    prec = _precedes(d)
    acum = _cumsum_scan_order(prec, la)
    acum_t = acum.T
    total = jnp.sum(la, axis=0, keepdims=True)

    rep = SSM_HEADS // SSM_GROUPS
    for g in range(SSM_GROUPS):
        cg = cm[:, g * SSM_STATE:(g + 1) * SSM_STATE].astype(BF16)
        bg = bm[:, g * SSM_STATE:(g + 1) * SSM_STATE].astype(BF16)
        cb = _dot_nt(cg, bg)
        for hh in range(rep):
            h = g * rep + hh
            a_col = acum[:, h:h + 1]
            a_row = acum_t[h:h + 1, :]
            tot = total[:, h:h + 1]
            x_h = xs[:, h * SSM_HEAD_DIM:(h + 1) * SSM_HEAD_DIM]
            xdt = x_h * dt[:, h:h + 1]
            lmat = jnp.exp(jnp.where(prec, a_col - a_row, -jnp.inf))
            h_in = st_ref[h]
            y = _dot((cb * lmat).astype(BF16), xdt.astype(BF16))
            y = y + _dot(cg, h_in.astype(BF16)) * jnp.exp(a_col)
            y = y + x_h * par_ref[2:3, h:h + 1]
            xw = (xdt * jnp.exp(tot - a_col)).astype(BF16)
            st_ref[h] = jnp.exp(tot) * h_in + _dot_tn(bg, xw)
            y_ref[:, h * SSM_HEAD_DIM:(h + 1) * SSM_HEAD_DIM] = y


def _ssd(big, small, conv_w, conv_b, par, n_ctx):
    bsz, s_len, _ = big.shape
    ncc, nc = n_ctx // CHUNK, s_len // CHUNK
    per = CHUNK // HALO
    ch = functools.partial(_scan_chunk, ncc=ncc, nc=nc)
    cblk = COL_XBC // SSM_CONV_CH
    return pl.pallas_call(
        functools.partial(_ssd_kernel, ncc=ncc, nc=nc),
        out_shape=jax.ShapeDtypeStruct((2, bsz, s_len, SSM_WIDTH), F32),
        grid=(bsz, 2, nc),
        in_specs=[
            pl.BlockSpec((None, CHUNK, SSM_CONV_CH), lambda b, d, t: (b, ch(d, t), cblk)),
            pl.BlockSpec((None, HALO, SSM_CONV_CH),
                         lambda b, d, t: (b, jnp.maximum(ch(d, t) * per - 1, 0), cblk)),
            pl.BlockSpec((None, HALO, SSM_CONV_CH),
                         lambda b, d, t: (b, jnp.minimum((ch(d, t) + 1) * per, nc * per - 1), cblk)),
            pl.BlockSpec((None, CHUNK, LANE), lambda b, d, t: (b, ch(d, t), d)),
            pl.BlockSpec(conv_w.shape, lambda b, d, t: (0, 0)),
            pl.BlockSpec(conv_b.shape, lambda b, d, t: (0, 0)),
            pl.BlockSpec((None, 8, LANE), lambda b, d, t: (d, 0, 0)),
        ],
        out_specs=pl.BlockSpec((None, None, CHUNK, SSM_WIDTH), lambda b, d, t: (d, b, ch(d, t), 0)),
        scratch_shapes=[pltpu.VMEM((SSM_HEADS, SSM_STATE, SSM_HEAD_DIM), F32),
                        pltpu.VMEM((CHUNK + 2 * HALO, SSM_CONV_CH), F32)],
        compiler_params=_cparams(("arbitrary", "arbitrary", "arbitrary")),
        name="ssd",
    )(big, big, big, small, conv_w, conv_b, par)


def _mlstm_kernel(q_ref, k_ref, v_ref, sm_ref, par_ref, y_ref, ct_ref, n_ref, m_ref, *, ncc, nc):
    d = pl.program_id(1)
    t = pl.program_id(2)

    @pl.when(t == 0)
    def _():
        ct_ref[...] = jnp.zeros_like(ct_ref)
        n_ref[...] = jnp.zeros_like(n_ref)
        m_ref[...] = jnp.zeros_like(m_ref)

    pre = sm_ref[...] + par_ref[0:1, :]
    lf = -_softplus(-pre)
    prec = _precedes(d)
    bcum = _cumsum_scan_order(prec, lf)
    bcum_t = bcum.T
    li_t = pre.T
    gsum = jnp.sum(lf, axis=0, keepdims=True)

    for h in range(ML_HEADS):
        hs = slice(h * ML_HEAD_DIM, (h + 1) * ML_HEAD_DIM)
        ci, cf = ML_HEADS + h, 2 * ML_HEADS + h
        b_col = bcum[:, cf:cf + 1]
        b_row = bcum_t[cf:cf + 1, :]
        gt = gsum[:, cf:cf + 1]
        li_col = pre[:, ci:ci + 1]
        li_row = li_t[ci:ci + 1, :]
        qh = q_ref[:, hs]
        kh = k_ref[:, hs]
        vh = v_ref[:, hs]
        m_in = m_ref[h][0:1, 0:1]
        ct_in = ct_ref[h]
        n_in = n_ref[h][0:1, :]

        log_d = jnp.where(prec, b_col - b_row + li_row, -jnp.inf)
        log_inter = b_col + m_in
        m_i = jnp.maximum(jnp.max(log_d, axis=-1, keepdims=True), log_inter)
        s = _dot_nt(qh, kh) * jnp.exp(log_d - m_i)
        inter = jnp.exp(log_inter - m_i)
        num = _dot(s.astype(BF16), vh) + _dot(qh, ct_in.astype(BF16)) * inter
        den = (jnp.sum(s, axis=-1, keepdims=True)
               + jnp.sum(qh.astype(F32) * n_in, axis=-1, keepdims=True) * inter)
        y_ref[:, hs] = num / jnp.maximum(jnp.abs(den), jnp.exp(-m_i))

        w_end = gt - b_col + li_col
        m_loc = jnp.max(w_end, axis=0, keepdims=True)
        ke = kh.astype(F32) * jnp.exp(w_end - m_loc)
        m_new = jnp.maximum(gt + m_in, m_loc)
        a_old = jnp.exp(gt + m_in - m_new)
        a_new = jnp.exp(m_loc - m_new)
        ct_ref[h] = a_old * ct_in + a_new * _dot_tn(ke.astype(BF16), vh)
        n_new = a_old * n_in + a_new * jnp.sum(ke, axis=0, keepdims=True)
        n_ref[h] = jnp.broadcast_to(n_new, n_ref.shape[1:])
        m_ref[h] = jnp.broadcast_to(m_new, m_ref.shape[1:])


def _mlstm(big, small, par, n_ctx):
    bsz, s_len, _ = big.shape
    ncc, nc = n_ctx // CHUNK, s_len // CHUNK
    ch = functools.partial(_scan_chunk, ncc=ncc, nc=nc)

    def col_spec(col):
        return pl.BlockSpec((None, CHUNK, ML_WIDTH), lambda b, d, t: (b, ch(d, t), col // ML_WIDTH))

    return pl.pallas_call(
        functools.partial(_mlstm_kernel, ncc=ncc, nc=nc),
        out_shape=jax.ShapeDtypeStruct((2, bsz, s_len, ML_WIDTH), F32),
        grid=(bsz, 2, nc),
        in_specs=[col_spec(COL_MQ), col_spec(COL_MK), col_spec(COL_MV),
                  pl.BlockSpec((None, CHUNK, LANE), lambda b, d, t: (b, ch(d, t), d)),
                  pl.BlockSpec((None, 8, LANE), lambda b, d, t: (d, 0, 0))],
        out_specs=pl.BlockSpec((None, None, CHUNK, ML_WIDTH), lambda b, d, t: (d, b, ch(d, t), 0)),
        scratch_shapes=[pltpu.VMEM((ML_HEADS, ML_HEAD_DIM, ML_HEAD_DIM), F32),
                        pltpu.VMEM((ML_HEADS, 8, ML_HEAD_DIM), F32),
                        pltpu.VMEM((ML_HEADS, 8, LANE), F32)],
        compiler_params=_cparams(("arbitrary", "arbitrary", "arbitrary")),
        name="mlstm",
    )(big, big, big, small, par)


def _merge_kernel(ya_ref, yb_ref, z_ref, yc_ref, o_ref, h_ref, g1_ref, sh_ref, sc_ref, ng_ref,
                  sg_ref, mg_ref, bd_ref, wo_ref, rwh_ref, rwl_ref, rb_ref,
                  hout_ref, v_ref, gates_ref):
    z = z_ref[...].astype(F32)
    yb = (yb_ref[0] + yb_ref[1]) * (z * _sigmoid(z))
    yb = yb * lax.rsqrt(jnp.mean(yb * yb, axis=-1, keepdims=True) + EPS) * sg_ref[...]
    yc = _sigmoid(o_ref[...].astype(F32)) * (yc_ref[0] + yc_ref[1])
    yc = yc * lax.rsqrt(_segment_mean_sq(yc, bd_ref[...]) + EPS) * mg_ref[...]
    y = (_dot(ya_ref[...], wo_ref[0:NA_WIDTH, :])
         + _dot(yb.astype(BF16), wo_ref[NA_WIDTH:NA_WIDTH + SSM_WIDTH, :])
         + _dot(yc.astype(BF16), wo_ref[NA_WIDTH + SSM_WIDTH:, :]))
    hn = h_ref[...] + g1_ref[...] * y
    hout_ref[...] = hn
    v = hn * lax.rsqrt(jnp.mean(hn * hn, axis=-1, keepdims=True) + EPS) * ng_ref[...]
    v = v * (1.0 + sc_ref[...]) + sh_ref[...]
    v_hi, v_lo = _split2(v)
    v_ref[...] = v_hi
    rwh = rwh_ref[...]
    logits = _dot(v_hi, rwh) + _dot(v_lo, rwh) + _dot(v_hi, rwl_ref[...]) + rb_ref[...]

    lane = lax.broadcasted_iota(jnp.int32, logits.shape, 1).astype(F32)
    work = logits
    top = None
    gates = jnp.zeros_like(logits)
    den = jnp.zeros((logits.shape[0], 1), F32)
    for _ in range(TOP_K):
        mx = jnp.max(work, axis=-1, keepdims=True)
        idx = jnp.min(jnp.where(work == mx, lane, float(N_EXPERTS)), axis=-1, keepdims=True)
        hit = lane == idx
        top = mx if top is None else top
        e = jnp.exp(mx - top)
        gates = gates + jnp.where(hit, e, 0.0)
        den = den + e
        work = jnp.where(hit, -jnp.inf, work)
    gates_ref[...] = gates / den


def _merge(ya, yb, yc, big, h, g1, sh2, sc2, ng, sg, mg, bd, wo, rwh, rwl, rb, n_ctx, ctx_out):
    bsz, s_len, d = h.shape
    tm = TOK_TILE
    first = 0 if ctx_out else n_ctx // tm
    n_out = s_len - first * tm
    mod_spec = pl.BlockSpec((None, None, 1, d), lambda b, i: (b, jnp.minimum(i + first, 1), 0, 0))
    full = lambda shape: pl.BlockSpec(shape, lambda b, i: (0,) * len(shape))
    pair = lambda w: pl.BlockSpec((2, None, tm, w), lambda b, i: (0, b, i + first, 0))
    out_tok = lambda w: pl.BlockSpec((None, tm, w), lambda b, i: (b, i, 0))
    return pl.pallas_call(
        _merge_kernel,
        out_shape=(jax.ShapeDtypeStruct((bsz, n_out, d), F32),
                   jax.ShapeDtypeStruct((bsz, n_out, d), BF16),
                   jax.ShapeDtypeStruct((bsz, n_out, N_EXPERTS), F32)),
        grid=(bsz, n_out // tm),
        in_specs=[pl.BlockSpec((None, tm, NA_WIDTH), lambda b, i: (b, i + first, 0)),
                  pair(SSM_WIDTH),
                  pl.BlockSpec((None, tm, SSM_WIDTH), lambda b, i: (b, i + first, COL_Z // SSM_WIDTH)),
                  pair(ML_WIDTH),
                  pl.BlockSpec((None, tm, ML_WIDTH), lambda b, i: (b, i + first, COL_MO // ML_WIDTH)),
                  pl.BlockSpec((None, tm, d), lambda b, i: (b, i + first, 0)),
                  mod_spec, mod_spec, mod_spec,
                  full((1, d)), full((1, SSM_WIDTH)), full((1, ML_WIDTH)), full(bd.shape),
                  full(wo.shape), full(rwh.shape), full(rwl.shape), full((1, N_EXPERTS))],
        out_specs=(out_tok(d), out_tok(d), out_tok(N_EXPERTS)),
        compiler_params=_cparams(("arbitrary", "arbitrary")),
        name="merge",
    )(ya, yb, big, yc, big, h, g1, sh2, sc2, ng, sg, mg, bd, wo, rwh, rwl, rb)


def _moe_kernel(x_ref, gates_ref, w1_ref, b1_ref, w2_ref, b2_ref, f_ref, acc_ref):
    e = pl.program_id(1)
    d_ff = w2_ref.shape[0]

    @pl.when(e == 0)
    def _():
        acc_ref[...] = jnp.zeros_like(acc_ref)

    x = x_ref[...]
    gates = gates_ref[...]
    lane = lax.broadcasted_iota(jnp.int32, gates.shape, 1)
    ge = jnp.sum(jnp.where(lane == e, gates, 0.0), axis=-1, keepdims=True)
    y = jnp.broadcast_to(b2_ref[...], acc_ref.shape)
    for c0 in range(0, d_ff, 2 * LANE):
        x_glu = _dot(x, w1_ref[:, c0:c0 + 2 * LANE]) + b1_ref[:, c0:c0 + 2 * LANE]
        x_lin = (_dot(x, w1_ref[:, d_ff + c0:d_ff + c0 + 2 * LANE])
                 + b1_ref[:, d_ff + c0:d_ff + c0 + 2 * LANE])
        x_glu = jnp.minimum(x_glu, SWIGLU_LIMIT)
        x_lin = jnp.clip(x_lin, -SWIGLU_LIMIT, SWIGLU_LIMIT)
        act = x_glu * _sigmoid(SWIGLU_ALPHA * x_glu) * (x_lin + 1.0)
        y = y + _dot(act.astype(BF16), w2_ref[c0:c0 + 2 * LANE, :])
    acc_ref[...] += ge * y

    @pl.when(e == N_EXPERTS - 1)
    def _():
        f_ref[...] = acc_ref[...]


def _moe(x, gates, w1, b1, w2, b2):
    n_tok, d = x.shape
    tm = math.gcd(n_tok, 1024)
    d_ff = w2.shape[1]
    return pl.pallas_call(
        _moe_kernel,
        out_shape=jax.ShapeDtypeStruct((n_tok, d), F32),
        grid=(n_tok // tm, N_EXPERTS),
        in_specs=[pl.BlockSpec((tm, d), lambda i, e: (i, 0)),
                  pl.BlockSpec((tm, N_EXPERTS), lambda i, e: (i, 0)),
                  pl.BlockSpec((None, d, 2 * d_ff), lambda i, e: (e, 0, 0)),
                  pl.BlockSpec((None, 1, 2 * d_ff), lambda i, e: (e, 0, 0)),
                  pl.BlockSpec((None, d_ff, d), lambda i, e: (e, 0, 0)),
                  pl.BlockSpec((None, 1, d), lambda i, e: (e, 0, 0))],
        out_specs=pl.BlockSpec((tm, d), lambda i, e: (i, 0)),
        scratch_shapes=[pltpu.VMEM((tm, d), F32)],
        compiler_params=_cparams(("arbitrary", "arbitrary")),
        name="moe",
    )(x, gates, w1, b1, w2, b2)


def _resid_kernel(h_ref, f_ref, g_ref, o_ref):
    o_ref[...] = h_ref[...] + g_ref[...] * f_ref[...]


def _resid(h, f, g2, first):
    bsz, n_tok, d = h.shape
    tm = TOK_TILE
    return pl.pallas_call(
        _resid_kernel,
        out_shape=jax.ShapeDtypeStruct((bsz, n_tok, d), F32),
        grid=(bsz, n_tok // tm),
        in_specs=[pl.BlockSpec((None, tm, d), lambda b, i: (b, i, 0)),
                  pl.BlockSpec((None, tm, d), lambda b, i: (b, i, 0)),
                  pl.BlockSpec((None, None, 1, d), lambda b, i: (b, jnp.minimum(i + first, 1), 0, 0))],
        out_specs=pl.BlockSpec((None, tm, d), lambda b, i: (b, i, 0)),
        compiler_params=_cparams(("arbitrary", "arbitrary")),
        name="resid",
    )(h, f, g2)


def _in_weight_perm():
    o = IN_OFFSETS
    big = np.concatenate([np.arange(o[0], o[5]), np.arange(o[6], o[10])])
    small = np.full((2, LANE), -1, np.int64)
    for d in range(2):
        for h in range(SSM_HEADS):
            small[d, h] = o[5] + d * SSM_HEADS + h
        for h in range(ML_HEADS):
            small[d, ML_HEADS + h] = o[10] + d * ML_HEADS + h
            small[d, 2 * ML_HEADS + h] = o[11] + d * ML_HEADS + h
    return np.concatenate([big, small.reshape(-1)])


def _rope_tables(n_ctx, n_lat):
    pos = np.arange(n_lat)
    row = (pos // GRID_W).astype(np.float32)
    col = (pos % GRID_W).astype(np.float32)
    n_freq = ML_HEAD_DIM // 4
    inv_freq = jnp.asarray(ROPE_THETA, F32) ** (-jnp.arange(n_freq, dtype=F32) / n_freq)
    ang = jnp.concatenate([jnp.asarray(row)[:, None] * inv_freq, jnp.asarray(col)[:, None] * inv_freq], axis=-1)
    cos, sin = jnp.cos(ang), jnp.sin(ang)
    cosf = jnp.tile(jnp.concatenate([cos, cos], axis=-1), (1, ML_HEADS))
    sinf = jnp.tile(jnp.concatenate([-sin, sin], axis=-1), (1, ML_HEADS))
    cosf = jnp.concatenate([jnp.ones((n_ctx, ML_WIDTH), F32), cosf], axis=0)
    sinf = jnp.concatenate([jnp.zeros((n_ctx, ML_WIDTH), F32), sinf], axis=0)
    return cosf, sinf


def _lane_rows(rows):
    out = jnp.zeros((8, LANE), F32)
    for r, (off, vec) in enumerate(rows):
        out = out.at[r, off:off + vec.shape[0]].set(vec.astype(F32))
    return out


def kernel(x, c, ctx, c_ctx, ada_w, ada_b, norm_g, w_in, na_qk_g, na_rpb, ssm_conv_w, ssm_conv_b,
           ssm_dt_bias, ssm_a_log, ssm_d, ssm_norm_g, ml_gate_b, ml_norm_g, w_out,
           router_w, router_b, exp_w1, exp_b1, exp_w2, exp_b2):
    bsz, n_lat, d = x.shape
    n_ctx = ctx.shape[1]
    rows = n_lat // GRID_W
    depth = w_in.shape[0]
    assert bsz < 8 and n_ctx == TOK_TILE and n_lat % TOK_TILE == 0 and rows >= NA_KROWS

    h = jnp.concatenate([ctx, x], axis=1)
    cond = jnp.zeros((8, d), F32).at[:bsz].set(c).at[bsz].set(c_ctx)
    cosf, sinf = _rope_tables(n_ctx, n_lat)
    perm = _in_weight_perm()
    seg = np.arange(2 * LANE) // NA_HEAD_DIM
    bd = jnp.asarray(seg[:, None] == seg[None, :], BF16)

    out = None
    for layer in range(depth):
        ctx_out = layer < depth - 1
        mod = _ada(cond, ada_w[layer], ada_b[layer])
        mods = []
        for m in jnp.split(mod, 6, axis=-1):
            lat = m[:bsz]
            cx = jnp.broadcast_to(m[bsz][None], (bsz, d))
            mods.append(jnp.stack([cx, lat], axis=1).reshape(bsz, 2, 1, d))
        sh1, sc1, g1, sh2, sc2, g2 = mods

        w = jnp.where(perm[None, :] >= 0, w_in[layer][:, np.maximum(perm, 0)], 0.0).astype(BF16)
        qkg = jnp.concatenate([jnp.tile(na_qk_g[layer, 0] * NA_HEAD_DIM ** -0.5, NA_HEADS),
                               jnp.tile(na_qk_g[layer, 1], NA_HEADS)]).reshape(1, 2 * NA_WIDTH)
        big, small = _inproj(h, sh1, sc1, norm_g[layer, 0].reshape(1, d), w, qkg, bd, cosf, sinf)

        bias = _attention_bias(na_rpb[layer], rows)
        ya = _attention(big, bias, n_ctx, rows, ctx_out)

        conv_w = jnp.zeros((8, SSM_CONV_CH), F32).at[:SSM_CONV].set(ssm_conv_w[layer])
        ssm_par = jnp.stack([_lane_rows([(0, ssm_dt_bias[layer, dr]), (0, ssm_a_log[layer, dr]),
                                         (0, ssm_d[layer, dr])]) for dr in range(2)])
        yb = _ssd(big, small, conv_w, ssm_conv_b[layer].reshape(1, -1), ssm_par, n_ctx)

        gate_row = [jnp.concatenate([jnp.zeros((ML_HEADS,), F32), ml_gate_b[layer, dr, 0],
                                     ml_gate_b[layer, dr, 1]]) for dr in range(2)]
        ml_par = jnp.stack([_lane_rows([(0, gate_row[dr])]) for dr in range(2)])
        yc = _mlstm(big, small, ml_par, n_ctx)

        rw = router_w[layer]
        rwh = rw.astype(BF16)
        rwl = (rw - rwh.astype(F32)).astype(BF16)
        hn, v, gates = _merge(ya, yb, yc, big, h, g1, sh2, sc2, norm_g[layer, 1].reshape(1, d),
                              ssm_norm_g[layer].reshape(1, -1), ml_norm_g[layer].reshape(1, -1), bd,
                              w_out[layer].astype(BF16), rwh, rwl, router_b[layer].reshape(1, -1),
                              n_ctx, ctx_out)

        n_tok = hn.shape[0] * hn.shape[1]
        f = _moe(v.reshape(n_tok, d), gates.reshape(n_tok, N_EXPERTS),
                 exp_w1[layer].astype(BF16), exp_b1[layer][:, None, :],
                 exp_w2[layer].astype(BF16), exp_b2[layer][:, None, :])
        h = _resid(hn, f.reshape(hn.shape), g2, 0 if ctx_out else 1)
        out = h
    return out
```

```python
import functools
import math

import numpy as np
import jax
import jax.numpy as jnp
from jax import lax
from jax.experimental import pallas as pl
from jax.experimental.pallas import tpu as pltpu

F32 = jnp.float32
BF16 = jnp.bfloat16

GRID_W = 64
EPS = 1e-6
NA_HEADS = 8
NA_HEAD_DIM = 64
NA_WIN_H = 8
NA_WIN_W = 16
SSM_HEADS = 4
SSM_HEAD_DIM = 64
SSM_GROUPS = 2
SSM_STATE = 128
SSM_CONV = 5
ML_HEADS = 4
ML_HEAD_DIM = 64
ROPE_THETA = 10000.0
N_EXPERTS = 32
TOP_K = 4
SWIGLU_ALPHA = 1.702
SWIGLU_LIMIT = 7.0

NA_WIDTH = NA_HEADS * NA_HEAD_DIM
SSM_WIDTH = SSM_HEADS * SSM_HEAD_DIM
SSM_BC = SSM_GROUPS * SSM_STATE
SSM_CONV_CH = SSM_WIDTH + 2 * SSM_BC
ML_WIDTH = ML_HEADS * ML_HEAD_DIM
IN_SPLITS = (NA_WIDTH, NA_WIDTH, NA_WIDTH, SSM_CONV_CH, SSM_WIDTH, 2 * SSM_HEADS,
             ML_WIDTH, ML_WIDTH, ML_WIDTH, ML_WIDTH, 2 * ML_HEADS, 2 * ML_HEADS)
IN_OFFSETS = tuple(int(o) for o in np.cumsum((0,) + IN_SPLITS))

LANE = 128
CHUNK = 128
HALO = 16
TOK_TILE = 256
NA_ROWS = 2
NA_KROWS = NA_ROWS + NA_WIN_H
BIG_W = 3 * NA_WIDTH + SSM_CONV_CH + SSM_WIDTH + 4 * ML_WIDTH
SMALL_W = 2 * LANE
MOE_BLOCK = 1024
MOE_CAP = 160
NEG = -1e30
VMEM_LIMIT = 48 * 1024 * 1024

COL_Q, COL_K, COL_V = 0, NA_WIDTH, 2 * NA_WIDTH
COL_XBC = 3 * NA_WIDTH
COL_Z = COL_XBC + SSM_CONV_CH
COL_MQ = COL_Z + SSM_WIDTH
COL_MK = COL_MQ + ML_WIDTH
COL_MV = COL_MK + ML_WIDTH
COL_MO = COL_MV + ML_WIDTH


def _cparams(sem):
    return pltpu.CompilerParams(dimension_semantics=sem, vmem_limit_bytes=VMEM_LIMIT)


def _sigmoid(x):
    return 1.0 / (1.0 + jnp.exp(-x))


def _softplus(x):
    return jnp.maximum(x, 0.0) + jnp.log1p(jnp.exp(-jnp.abs(x)))


def _dot(a, b):
    return jnp.dot(a, b, preferred_element_type=F32)


def _dot_nt(a, b):
    return lax.dot_general(a, b, (((1,), (1,)), ((), ())), preferred_element_type=F32)


def _dot_tn(a, b):
    return lax.dot_general(a, b, (((0,), (0,)), ((), ())), preferred_element_type=F32)


def _split3(x):
    x1 = x.astype(BF16)
    r1 = x - x1.astype(F32)
    x2 = r1.astype(BF16)
    x3 = (r1 - x2.astype(F32)).astype(BF16)
    return x1, x2, x3


def _split2(x):
    x1 = x.astype(BF16)
    return x1, (x - x1.astype(F32)).astype(BF16)


def _segment_mean_sq(x, bd):
    hi, lo = _split2(x * x)
    return (_dot(hi, bd) + _dot(lo, bd)) * (1.0 / NA_HEAD_DIM)


def _ada_kernel(c_ref, w_ref, b_ref, o_ref):
    c = c_ref[...]
    s = c * _sigmoid(c)
    o_ref[...] = jnp.dot(s, w_ref[...], preferred_element_type=F32,
                         precision=lax.Precision.HIGHEST) + b_ref[...]


def _ada(cond, w, b):
    n, d = cond.shape
    wcols = w.shape[1]
    bn = wcols // 4
    return pl.pallas_call(
        _ada_kernel,
        out_shape=jax.ShapeDtypeStruct((n, wcols), F32),
        grid=(4,),
        in_specs=[pl.BlockSpec((n, d), lambda j: (0, 0)),
                  pl.BlockSpec((d, bn), lambda j: (0, j)),
                  pl.BlockSpec((1, bn), lambda j: (0, j))],
        out_specs=pl.BlockSpec((n, bn), lambda j: (0, j)),
        compiler_params=_cparams(("arbitrary",)),
        name="ada",
    )(cond, w, b.reshape(1, wcols))


def _inproj_kernel(h_ref, sh_ref, sc_ref, g_ref, w_ref, qkg_ref, bd_ref, cos_ref, sin_ref,
                   big_ref, small_ref):
    x = h_ref[...]
    ms = jnp.mean(x * x, axis=-1, keepdims=True)
    u = x * lax.rsqrt(ms + EPS) * g_ref[...]
    u = u * (1.0 + sc_ref[...]) + sh_ref[...]
    ub = u.astype(BF16)
    bd = bd_ref[...]

    for c0 in range(0, 2 * NA_WIDTH, 2 * LANE):
        acc = _dot(ub, w_ref[:, c0:c0 + 2 * LANE])
        msq = _segment_mean_sq(acc, bd)
        y = acc * lax.rsqrt(msq + EPS) * qkg_ref[:, c0:c0 + 2 * LANE]
        big_ref[:, c0:c0 + 2 * LANE] = y.astype(BF16)
    for c0 in range(COL_V, COL_MQ, 2 * LANE):
        big_ref[:, c0:c0 + 2 * LANE] = _dot(ub, w_ref[:, c0:c0 + 2 * LANE]).astype(BF16)
    lane = lax.broadcasted_iota(jnp.int32, (x.shape[0], LANE), 1)
    first_half = (lane % ML_HEAD_DIM) < (ML_HEAD_DIM // 2)
    for base, scale in ((COL_MQ, ML_HEAD_DIM ** -0.5), (COL_MK, 1.0)):
        for j in range(ML_WIDTH // LANE):
            c0 = base + j * LANE
            acc = _dot(ub, w_ref[:, c0:c0 + LANE])
            swapped = jnp.where(first_half,
                                pltpu.roll(acc, LANE - ML_HEAD_DIM // 2, 1),
                                pltpu.roll(acc, ML_HEAD_DIM // 2, 1))
            y = acc * cos_ref[:, j * LANE:(j + 1) * LANE] + swapped * sin_ref[:, j * LANE:(j + 1) * LANE]
            big_ref[:, c0:c0 + LANE] = (y * scale).astype(BF16)
    for c0 in range(COL_MV, BIG_W, 2 * LANE):
        big_ref[:, c0:c0 + 2 * LANE] = _dot(ub, w_ref[:, c0:c0 + 2 * LANE]).astype(BF16)
    small_ref[...] = _dot(ub, w_ref[:, BIG_W:BIG_W + SMALL_W])


def _inproj(h, sh, sc, g, w, qkg, bd, cosf, sinf):
    bsz, s_len, d = h.shape
    tm = TOK_TILE
    mod_spec = pl.BlockSpec((None, None, 1, d), lambda b, i: (b, jnp.minimum(i, 1), 0, 0))
    full = lambda shape: pl.BlockSpec(shape, lambda b, i: (0,) * len(shape))
    return pl.pallas_call(
        _inproj_kernel,
        out_shape=(jax.ShapeDtypeStruct((bsz, s_len, BIG_W), BF16),
                   jax.ShapeDtypeStruct((bsz, s_len, SMALL_W), F32)),
        grid=(bsz, s_len // tm),
        in_specs=[pl.BlockSpec((None, tm, d), lambda b, i: (b, i, 0)),
                  mod_spec, mod_spec, full((1, d)), full(w.shape), full(qkg.shape), full(bd.shape),
                  pl.BlockSpec((tm, ML_WIDTH), lambda b, i: (i, 0)),
                  pl.BlockSpec((tm, ML_WIDTH), lambda b, i: (i, 0))],
        out_specs=(pl.BlockSpec((None, tm, BIG_W), lambda b, i: (b, i, 0)),
                   pl.BlockSpec((None, tm, SMALL_W), lambda b, i: (b, i, 0))),
        compiler_params=_cparams(("arbitrary", "arbitrary")),
        name="inproj",
    )(h, sh, sc, g, w, qkg, bd, cosf, sinf)


def _na_kernel(q_ref, k0, k1, k2, k3, k4, v0, v1, v2, v3, v4, kc_ref, vc_ref, bias_ref, o_ref):
    nq = q_ref.shape[0]
    lane = lax.broadcasted_iota(jnp.int32, (nq, LANE), 1)
    low = lane < NA_HEAD_DIM
    zero = jnp.zeros((), BF16)
    for p in range(NA_WIDTH // LANE):
        cs = slice(p * LANE, (p + 1) * LANE)
        qp = q_ref[:, cs]
        q01 = jnp.concatenate([jnp.where(low, qp, zero), jnp.where(low, zero, qp)], axis=0)
        kw = jnp.concatenate([r[:, cs] for r in (k0, k1, k2, k3, k4)], axis=0)
        vw = jnp.concatenate([r[:, cs] for r in (v0, v1, v2, v3, v4)], axis=0)
        s_w = _dot_nt(q01, kw) + bias_ref[p]
        s_c = _dot_nt(q01, kc_ref[:, cs])
        m = jnp.maximum(jnp.max(s_w, axis=-1, keepdims=True), jnp.max(s_c, axis=-1, keepdims=True))
        p_w = jnp.exp(s_w - m)
        p_c = jnp.exp(s_c - m)
        den = jnp.sum(p_w, axis=-1, keepdims=True) + jnp.sum(p_c, axis=-1, keepdims=True)
        o01 = (_dot(p_w.astype(BF16), vw) + _dot(p_c.astype(BF16), vc_ref[:, cs])) / den
        o_ref[:, cs] = jnp.where(low, o01[:nq], o01[nq:]).astype(o_ref.dtype)


def _attention(big, bias, n_ctx, rows, ctx_out):
    bsz, s_len, _ = big.shape
    nq = NA_ROWS * GRID_W
    ncb = n_ctx // nq
    nrp = rows // NA_ROWS
    first = 0 if ctx_out else ncb
    n_win = NA_KROWS // NA_ROWS

    def lat(i):
        return jnp.maximum(i + first - ncb, 0)

    def kv_spec(j, col):
        return pl.BlockSpec(
            (None, nq, NA_WIDTH),
            lambda b, i: (b, ncb + jnp.clip(lat(i) - 2, 0, nrp - n_win) + j, col))

    def bias_type(b, i):
        step = i + first
        r = lat(i)
        t = jnp.where(r < 2, r, jnp.where(r >= nrp - 2, r - (nrp - 2) + 3, 2))
        return (jnp.where(step < ncb, 5, t), 0, 0, 0)

    return pl.pallas_call(
        _na_kernel,
        out_shape=jax.ShapeDtypeStruct((bsz, s_len, NA_WIDTH), BF16),
        grid=(bsz, ncb + nrp - first),
        in_specs=[pl.BlockSpec((None, nq, NA_WIDTH), lambda b, i: (b, i + first, 0))]
                 + [kv_spec(j, 1) for j in range(n_win)] + [kv_spec(j, 2) for j in range(n_win)]
                 + [pl.BlockSpec((None, n_ctx, NA_WIDTH), lambda b, i: (b, 0, 1)),
                    pl.BlockSpec((None, n_ctx, NA_WIDTH), lambda b, i: (b, 0, 2)),
                    pl.BlockSpec((None, NA_WIDTH // LANE, 2 * nq, NA_KROWS * GRID_W), bias_type)],
        out_specs=pl.BlockSpec((None, nq, NA_WIDTH), lambda b, i: (b, i + first, 0)),
        compiler_params=_cparams(("arbitrary", "arbitrary")),
        name="attention",
    )(big, *([big] * (2 * n_win + 2)), bias)


def _attention_bias(rpb, rows):
    nrp = rows // NA_ROWS
    n_win = NA_KROWS // NA_ROWS
    reps = np.array([0, 1, 2, nrp - 2, nrp - 1])
    qr = (NA_ROWS * reps)[:, None] + np.arange(NA_ROWS)[None, :]
    ws = NA_ROWS * np.clip(reps - 2, 0, nrp - n_win)
    kr = ws[:, None] + np.arange(NA_KROWS)[None, :]
    r0 = np.clip(qr - NA_WIN_H // 2, 0, rows - NA_WIN_H)
    ok_r = (kr[:, None, :] >= r0[:, :, None]) & (kr[:, None, :] < r0[:, :, None] + NA_WIN_H)
    dr = np.clip(kr[:, None, :] - qr[:, :, None] + (NA_WIN_H - 1), 0, 2 * NA_WIN_H - 2)
    qc = np.arange(GRID_W)
    c0 = np.clip(qc - NA_WIN_W // 2, 0, GRID_W - NA_WIN_W)
    kc = np.arange(GRID_W)
    ok_c = (kc[None, :] >= c0[:, None]) & (kc[None, :] < c0[:, None] + NA_WIN_W)
    dc = np.clip(kc[None, :] - qc[:, None] + (NA_WIN_W - 1), 0, 2 * NA_WIN_W - 2)
    sel_r = jnp.asarray(dr[..., None] == np.arange(2 * NA_WIN_H - 1), F32)
    sel_c = jnp.asarray(dc[..., None] == np.arange(2 * NA_WIN_W - 1), F32)
    hp = lax.Precision.HIGHEST
    g = jnp.einsum('hrc,taer->thaec', rpb.astype(F32), sel_r, precision=hp)
    g = jnp.einsum('thaec,qkc->thaqek', g, sel_c, precision=hp)
    ok = ok_r[:, None, :, None, :, None] & ok_c[None, None, None, :, None, :]
    g = jnp.where(ok, g, NEG)
    g = g.reshape(5, NA_HEADS * NA_ROWS * GRID_W, NA_KROWS * GRID_W)
    g = g.reshape(5, NA_WIDTH // LANE, 2 * NA_ROWS * GRID_W, NA_KROWS * GRID_W)
    masked = jnp.full((1,) + g.shape[1:], NEG, F32)
    return jnp.concatenate([g, masked], axis=0)


def _scan_chunk(d, t, ncc, nc):
    bwd = jnp.where(t < ncc, ncc - 1 - t, ncc + (nc - 1 - t))
    return jnp.where(d == 0, t, bwd)


def _precedes(d):
    ii = lax.broadcasted_iota(jnp.int32, (CHUNK, CHUNK), 0)
    jj = lax.broadcasted_iota(jnp.int32, (CHUNK, CHUNK), 1)
    return (jj - ii) * (1 - 2 * d) <= 0


def _cumsum_scan_order(prec, x):
    tri = jnp.where(prec, 1.0, 0.0).astype(BF16)
    x1, x2, x3 = _split3(x)
    return _dot(tri, x1) + _dot(tri, x2) + _dot(tri, x3)


def _ssd_kernel(x_ref, xp_ref, xn_ref, sm_ref, cw_ref, cb_ref, par_ref, y_ref, st_ref, cbuf_ref,
                *, ncc, nc):
    d = pl.program_id(1)
    t = pl.program_id(2)
    chunk = _scan_chunk(d, t, ncc, nc)

    @pl.when(t == 0)
    def _():
        st_ref[...] = jnp.zeros_like(st_ref)

    prev_ok = jnp.where(jnp.logical_and(chunk != 0, chunk != ncc), 1.0, 0.0)
    next_ok = jnp.where(jnp.logical_and(chunk != ncc - 1, chunk != nc - 1), 1.0, 0.0)
    cbuf_ref[0:HALO, :] = xp_ref[...].astype(F32) * prev_ok
    cbuf_ref[HALO:HALO + CHUNK, :] = x_ref[...].astype(F32)
    cbuf_ref[HALO + CHUNK:, :] = xn_ref[...].astype(F32) * next_ok
    acc = jnp.broadcast_to(cb_ref[...], (CHUNK, SSM_CONV_CH))
    for k in range(SSM_CONV):
        r0 = HALO - SSM_CONV // 2 + k
        acc = acc + cbuf_ref[r0:r0 + CHUNK, :] * cw_ref[k:k + 1, :]
    xc = acc * _sigmoid(acc)
    xs = xc[:, :SSM_WIDTH]
    bm = xc[:, SSM_WIDTH:SSM_WIDTH + SSM_BC]
    cm = xc[:, SSM_WIDTH + SSM_BC:]

    dt = _softplus(sm_ref[...] + par_ref[0:1, :])
    la = dt * (-jnp.exp(par_ref[1:2, :]))
    prec = _precedes(d)
    acum = _cumsum_scan_order(prec, la)
    acum_t = acum.T
    total = jnp.sum(la, axis=0, keepdims=True)

    rep = SSM_HEADS // SSM_GROUPS
    for g in range(SSM_GROUPS):
        cg = cm[:, g * SSM_STATE:(g + 1) * SSM_STATE].astype(BF16)
        bg = bm[:, g * SSM_STATE:(g + 1) * SSM_STATE].astype(BF16)
        cb = _dot_nt(cg, bg)
        for hh in range(rep):
            h = g * rep + hh
            a_col = acum[:, h:h + 1]
            a_row = acum_t[h:h + 1, :]
            tot = total[:, h:h + 1]
            x_h = xs[:, h * SSM_HEAD_DIM:(h + 1) * SSM_HEAD_DIM]
            xdt = x_h * dt[:, h:h + 1]
            lmat = jnp.exp(jnp.where(prec, a_col - a_row, -jnp.inf))
            h_in = st_ref[h]
            y = _dot((cb * lmat).astype(BF16), xdt.astype(BF16))
            y = y + _dot(cg, h_in.astype(BF16)) * jnp.exp(a_col)
            y = y + x_h * par_ref[2:3, h:h + 1]
            xw = (xdt * jnp.exp(tot - a_col)).astype(BF16)
            st_ref[h] = jnp.exp(tot) * h_in + _dot_tn(bg, xw)
            y_ref[:, h * SSM_HEAD_DIM:(h + 1) * SSM_HEAD_DIM] = y


def _ssd(big, small, conv_w, conv_b, par, n_ctx):
    bsz, s_len, _ = big.shape
    ncc, nc = n_ctx // CHUNK, s_len // CHUNK
    per = CHUNK // HALO
    ch = functools.partial(_scan_chunk, ncc=ncc, nc=nc)
    cblk = COL_XBC // SSM_CONV_CH
    return pl.pallas_call(
        functools.partial(_ssd_kernel, ncc=ncc, nc=nc),
        out_shape=jax.ShapeDtypeStruct((2, bsz, s_len, SSM_WIDTH), F32),
        grid=(bsz, 2, nc),
        in_specs=[
            pl.BlockSpec((None, CHUNK, SSM_CONV_CH), lambda b, d, t: (b, ch(d, t), cblk)),
            pl.BlockSpec((None, HALO, SSM_CONV_CH),
                         lambda b, d, t: (b, jnp.maximum(ch(d, t) * per - 1, 0), cblk)),
            pl.BlockSpec((None, HALO, SSM_CONV_CH),
                         lambda b, d, t: (b, jnp.minimum((ch(d, t) + 1) * per, nc * per - 1), cblk)),
            pl.BlockSpec((None, CHUNK, LANE), lambda b, d, t: (b, ch(d, t), d)),
            pl.BlockSpec(conv_w.shape, lambda b, d, t: (0, 0)),
            pl.BlockSpec(conv_b.shape, lambda b, d, t: (0, 0)),
            pl.BlockSpec((None, 8, LANE), lambda b, d, t: (d, 0, 0)),
        ],
        out_specs=pl.BlockSpec((None, None, CHUNK, SSM_WIDTH), lambda b, d, t: (d, b, ch(d, t), 0)),
        scratch_shapes=[pltpu.VMEM((SSM_HEADS, SSM_STATE, SSM_HEAD_DIM), F32),
                        pltpu.VMEM((CHUNK + 2 * HALO, SSM_CONV_CH), F32)],
        compiler_params=_cparams(("arbitrary", "arbitrary", "arbitrary")),
        name="ssd",
    )(big, big, big, small, conv_w, conv_b, par)


def _mlstm_kernel(q_ref, k_ref, v_ref, sm_ref, par_ref, y_ref, ct_ref, n_ref, m_ref, *, ncc, nc):
    d = pl.program_id(1)
    t = pl.program_id(2)

    @pl.when(t == 0)
    def _():
        ct_ref[...] = jnp.zeros_like(ct_ref)
        n_ref[...] = jnp.zeros_like(n_ref)
        m_ref[...] = jnp.zeros_like(m_ref)

    pre = sm_ref[...] + par_ref[0:1, :]
    lf = -_softplus(-pre)
    prec = _precedes(d)
    bcum = _cumsum_scan_order(prec, lf)
    bcum_t = bcum.T
    li_t = pre.T
    gsum = jnp.sum(lf, axis=0, keepdims=True)

    for h in range(ML_HEADS):
        hs = slice(h * ML_HEAD_DIM, (h + 1) * ML_HEAD_DIM)
        ci, cf = ML_HEADS + h, 2 * ML_HEADS + h
        b_col = bcum[:, cf:cf + 1]
        b_row = bcum_t[cf:cf + 1, :]
        gt = gsum[:, cf:cf + 1]
        li_col = pre[:, ci:ci + 1]
        li_row = li_t[ci:ci + 1, :]
        qh = q_ref[:, hs]
        kh = k_ref[:, hs]
        vh = v_ref[:, hs]
        m_in = m_ref[h][0:1, 0:1]
        ct_in = ct_ref[h]
        n_in = n_ref[h][0:1, :]

        log_d = jnp.where(prec, b_col - b_row + li_row, -jnp.inf)
        log_inter = b_col + m_in
        m_i = jnp.maximum(jnp.max(log_d, axis=-1, keepdims=True), log_inter)
        s = _dot_nt(qh, kh) * jnp.exp(log_d - m_i)
        inter = jnp.exp(log_inter - m_i)
        num = _dot(s.astype(BF16), vh) + _dot(qh, ct_in.astype(BF16)) * inter
        den = (jnp.sum(s, axis=-1, keepdims=True)
               + jnp.sum(qh.astype(F32) * n_in, axis=-1, keepdims=True) * inter)
        y_ref[:, hs] = num / jnp.maximum(jnp.abs(den), jnp.exp(-m_i))

        w_end = gt - b_col + li_col
        m_loc = jnp.max(w_end, axis=0, keepdims=True)
        ke = kh.astype(F32) * jnp.exp(w_end - m_loc)
        m_new = jnp.maximum(gt + m_in, m_loc)
        a_old = jnp.exp(gt + m_in - m_new)
        a_new = jnp.exp(m_loc - m_new)
        ct_ref[h] = a_old * ct_in + a_new * _dot_tn(ke.astype(BF16), vh)
        n_new = a_old * n_in + a_new * jnp.sum(ke, axis=0, keepdims=True)
        n_ref[h] = jnp.broadcast_to(n_new, n_ref.shape[1:])
        m_ref[h] = jnp.broadcast_to(m_new, m_ref.shape[1:])


def _mlstm(big, small, par, n_ctx):
    bsz, s_len, _ = big.shape
    ncc, nc = n_ctx // CHUNK, s_len // CHUNK
    ch = functools.partial(_scan_chunk, ncc=ncc, nc=nc)

    def col_spec(col):
        return pl.BlockSpec((None, CHUNK, ML_WIDTH), lambda b, d, t: (b, ch(d, t), col // ML_WIDTH))

    return pl.pallas_call(
        functools.partial(_mlstm_kernel, ncc=ncc, nc=nc),
        out_shape=jax.ShapeDtypeStruct((2, bsz, s_len, ML_WIDTH), F32),
        grid=(bsz, 2, nc),
        in_specs=[col_spec(COL_MQ), col_spec(COL_MK), col_spec(COL_MV),
                  pl.BlockSpec((None, CHUNK, LANE), lambda b, d, t: (b, ch(d, t), d)),
                  pl.BlockSpec((None, 8, LANE), lambda b, d, t: (d, 0, 0))],
        out_specs=pl.BlockSpec((None, None, CHUNK, ML_WIDTH), lambda b, d, t: (d, b, ch(d, t), 0)),
        scratch_shapes=[pltpu.VMEM((ML_HEADS, ML_HEAD_DIM, ML_HEAD_DIM), F32),
                        pltpu.VMEM((ML_HEADS, 8, ML_HEAD_DIM), F32),
                        pltpu.VMEM((ML_HEADS, 8, LANE), F32)],
        compiler_params=_cparams(("arbitrary", "arbitrary", "arbitrary")),
        name="mlstm",
    )(big, big, big, small, par)


def _merge_kernel(ya_ref, yb_ref, z_ref, yc_ref, o_ref, h_ref, g1_ref, sh_ref, sc_ref, ng_ref,
                  sg_ref, mg_ref, bd_ref, wo_ref, rwh_ref, rwl_ref, rb_ref,
                  hout_ref, v_ref, gates_ref):
    z = z_ref[...].astype(F32)
    yb = (yb_ref[0] + yb_ref[1]) * (z * _sigmoid(z))
    yb = yb * lax.rsqrt(jnp.mean(yb * yb, axis=-1, keepdims=True) + EPS) * sg_ref[...]
    yc = _sigmoid(o_ref[...].astype(F32)) * (yc_ref[0] + yc_ref[1])
    yc = yc * lax.rsqrt(_segment_mean_sq(yc, bd_ref[...]) + EPS) * mg_ref[...]
    y = (_dot(ya_ref[...], wo_ref[0:NA_WIDTH, :])
         + _dot(yb.astype(BF16), wo_ref[NA_WIDTH:NA_WIDTH + SSM_WIDTH, :])
         + _dot(yc.astype(BF16), wo_ref[NA_WIDTH + SSM_WIDTH:, :]))
    hn = h_ref[...] + g1_ref[...] * y
    hout_ref[...] = hn
    v = hn * lax.rsqrt(jnp.mean(hn * hn, axis=-1, keepdims=True) + EPS) * ng_ref[...]
    v = v * (1.0 + sc_ref[...]) + sh_ref[...]
    v_hi, v_lo = _split2(v)
    v_ref[...] = v_hi
    rwh = rwh_ref[...]
    logits = _dot(v_hi, rwh) + _dot(v_lo, rwh) + _dot(v_hi, rwl_ref[...]) + rb_ref[...]

    lane = lax.broadcasted_iota(jnp.int32, logits.shape, 1).astype(F32)
    work = logits
    top = None
    gates = jnp.zeros_like(logits)
    den = jnp.zeros((logits.shape[0], 1), F32)
    for _ in range(TOP_K):
        mx = jnp.max(work, axis=-1, keepdims=True)
        idx = jnp.min(jnp.where(work == mx, lane, float(LANE)), axis=-1, keepdims=True)
        hit = lane == idx
        top = mx if top is None else top
        e = jnp.exp(mx - top)
        gates = gates + jnp.where(hit, e, 0.0)
        den = den + e
        work = jnp.where(hit, -jnp.inf, work)
    gates_ref[...] = gates / den


def _merge(ya, yb, yc, big, h, g1, sh2, sc2, ng, sg, mg, bd, wo, rwh, rwl, rb, n_ctx, ctx_out):
    bsz, s_len, d = h.shape
    tm = TOK_TILE
    first = 0 if ctx_out else n_ctx // tm
    n_out = s_len - first * tm
    mod_spec = pl.BlockSpec((None, None, 1, d), lambda b, i: (b, jnp.minimum(i + first, 1), 0, 0))
    full = lambda shape: pl.BlockSpec(shape, lambda b, i: (0,) * len(shape))
    pair = lambda w: pl.BlockSpec((2, None, tm, w), lambda b, i: (0, b, i + first, 0))
    out_tok = lambda w: pl.BlockSpec((None, tm, w), lambda b, i: (b, i, 0))
    return pl.pallas_call(
        _merge_kernel,
        out_shape=(jax.ShapeDtypeStruct((bsz, n_out, d), F32),
                   jax.ShapeDtypeStruct((bsz, n_out, d), BF16),
                   jax.ShapeDtypeStruct((bsz, n_out, LANE), F32)),
        grid=(bsz, n_out // tm),
        in_specs=[pl.BlockSpec((None, tm, NA_WIDTH), lambda b, i: (b, i + first, 0)),
                  pair(SSM_WIDTH),
                  pl.BlockSpec((None, tm, SSM_WIDTH), lambda b, i: (b, i + first, COL_Z // SSM_WIDTH)),
                  pair(ML_WIDTH),
                  pl.BlockSpec((None, tm, ML_WIDTH), lambda b, i: (b, i + first, COL_MO // ML_WIDTH)),
                  pl.BlockSpec((None, tm, d), lambda b, i: (b, i + first, 0)),
                  mod_spec, mod_spec, mod_spec,
                  full((1, d)), full((1, SSM_WIDTH)), full((1, ML_WIDTH)), full(bd.shape),
                  full(wo.shape), full(rwh.shape), full(rwl.shape), full((1, LANE))],
        out_specs=(out_tok(d), out_tok(d), out_tok(LANE)),
        compiler_params=_cparams(("arbitrary", "arbitrary")),
        name="merge",
    )(ya, yb, big, yc, big, h, g1, sh2, sc2, ng, sg, mg, bd, wo, rwh, rwl, rb)


def _moe_kernel(x_ref, gates_ref, w1_ref, b1_ref, w2_ref, b2_ref, f_ref, acc_ref, rank_ref, gate_ref):
    e = pl.program_id(1)
    tb = x_ref.shape[0]
    d_ff = w2_ref.shape[0]

    @pl.when(e == 0)
    def _():
        acc_ref[...] = jnp.zeros_like(acc_ref)
        gates = gates_ref[...]
        picked = gates > 0.0
        ii = lax.broadcasted_iota(jnp.int32, (tb, tb), 0)
        jj = lax.broadcasted_iota(jnp.int32, (tb, tb), 1)
        tri = jnp.where(jj <= ii, 1.0, 0.0).astype(BF16)
        incl = _dot(tri, jnp.where(picked, 1.0, 0.0).astype(BF16))
        rank_ref[...] = jnp.where(picked, incl - 1.0, -1.0).T
        gate_ref[...] = gates.T

    slot = rank_ref[pl.ds(e, 1), :]
    gate = gate_ref[pl.ds(e, 1), :]
    count = jnp.max(slot).astype(jnp.int32) + 1
    x = x_ref[...]

    def sub_tile(s, carry):
        row = (lax.broadcasted_iota(jnp.int32, (MOE_CAP, tb), 0) + s * MOE_CAP).astype(F32)
        hit = slot == row
        onehot = jnp.where(hit, 1.0, 0.0).astype(BF16)
        xe = _dot(onehot, x).astype(BF16)
        w = jnp.sum(jnp.where(hit, gate, 0.0), axis=-1, keepdims=True)
        y = jnp.broadcast_to(b2_ref[...], (MOE_CAP, x.shape[1]))
        for c0 in range(0, d_ff, 2 * LANE):
            x_glu = _dot(xe, w1_ref[:, c0:c0 + 2 * LANE]) + b1_ref[:, c0:c0 + 2 * LANE]
            x_lin = (_dot(xe, w1_ref[:, d_ff + c0:d_ff + c0 + 2 * LANE])
                     + b1_ref[:, d_ff + c0:d_ff + c0 + 2 * LANE])
            x_glu = jnp.minimum(x_glu, SWIGLU_LIMIT)
            x_lin = jnp.clip(x_lin, -SWIGLU_LIMIT, SWIGLU_LIMIT)
            act = x_glu * _sigmoid(SWIGLU_ALPHA * x_glu) * (x_lin + 1.0)
            y = y + _dot(act.astype(BF16), w2_ref[c0:c0 + 2 * LANE, :])
        acc_ref[...] += _dot_tn(onehot, (y * w).astype(BF16))
        return carry

    lax.fori_loop(0, (count + MOE_CAP - 1) // MOE_CAP, sub_tile, 0)

    @pl.when(e == N_EXPERTS - 1)
    def _():
        f_ref[...] = acc_ref[...]


def _moe(x, gates, w1, b1, w2, b2):
    n_tok, d = x.shape
    tb = math.gcd(n_tok, MOE_BLOCK)
    d_ff = w2.shape[1]
    return pl.pallas_call(
        _moe_kernel,
        out_shape=jax.ShapeDtypeStruct((n_tok, d), F32),
        grid=(n_tok // tb, N_EXPERTS),
        in_specs=[pl.BlockSpec((tb, d), lambda i, e: (i, 0)),
                  pl.BlockSpec((tb, LANE), lambda i, e: (i, 0)),
                  pl.BlockSpec((None, d, 2 * d_ff), lambda i, e: (e, 0, 0)),
                  pl.BlockSpec((None, 1, 2 * d_ff), lambda i, e: (e, 0, 0)),
                  pl.BlockSpec((None, d_ff, d), lambda i, e: (e, 0, 0)),
                  pl.BlockSpec((None, 1, d), lambda i, e: (e, 0, 0))],
        out_specs=pl.BlockSpec((tb, d), lambda i, e: (i, 0)),
        scratch_shapes=[pltpu.VMEM((tb, d), F32), pltpu.VMEM((LANE, tb), F32), pltpu.VMEM((LANE, tb), F32)],
        compiler_params=_cparams(("arbitrary", "arbitrary")),
        name="moe",
    )(x, gates, w1, b1, w2, b2)


def _resid_kernel(h_ref, f_ref, g_ref, o_ref):
    o_ref[...] = h_ref[...] + g_ref[...] * f_ref[...]


def _resid(h, f, g2, first):
    bsz, n_tok, d = h.shape
    tm = TOK_TILE
    return pl.pallas_call(
        _resid_kernel,
        out_shape=jax.ShapeDtypeStruct((bsz, n_tok, d), F32),
        grid=(bsz, n_tok // tm),
        in_specs=[pl.BlockSpec((None, tm, d), lambda b, i: (b, i, 0)),
                  pl.BlockSpec((None, tm, d), lambda b, i: (b, i, 0)),
                  pl.BlockSpec((None, None, 1, d), lambda b, i: (b, jnp.minimum(i + first, 1), 0, 0))],
        out_specs=pl.BlockSpec((None, tm, d), lambda b, i: (b, i, 0)),
        compiler_params=_cparams(("arbitrary", "arbitrary")),
        name="resid",
    )(h, f, g2)


def _in_weight_perm():
    o = IN_OFFSETS
    big = np.concatenate([np.arange(o[0], o[5]), np.arange(o[6], o[10])])
    small = np.full((2, LANE), -1, np.int64)
    for d in range(2):
        for h in range(SSM_HEADS):
            small[d, h] = o[5] + d * SSM_HEADS + h
        for h in range(ML_HEADS):
            small[d, ML_HEADS + h] = o[10] + d * ML_HEADS + h
            small[d, 2 * ML_HEADS + h] = o[11] + d * ML_HEADS + h
    return np.concatenate([big, small.reshape(-1)])


def _rope_tables(n_ctx, n_lat):
    pos = np.arange(n_lat)
    row = (pos // GRID_W).astype(np.float32)
    col = (pos % GRID_W).astype(np.float32)
    n_freq = ML_HEAD_DIM // 4
    inv_freq = jnp.asarray(ROPE_THETA, F32) ** (-jnp.arange(n_freq, dtype=F32) / n_freq)
    ang = jnp.concatenate([jnp.asarray(row)[:, None] * inv_freq, jnp.asarray(col)[:, None] * inv_freq], axis=-1)
    cos, sin = jnp.cos(ang), jnp.sin(ang)
    cosf = jnp.tile(jnp.concatenate([cos, cos], axis=-1), (1, ML_HEADS))
    sinf = jnp.tile(jnp.concatenate([-sin, sin], axis=-1), (1, ML_HEADS))
    cosf = jnp.concatenate([jnp.ones((n_ctx, ML_WIDTH), F32), cosf], axis=0)
    sinf = jnp.concatenate([jnp.zeros((n_ctx, ML_WIDTH), F32), sinf], axis=0)
    return cosf, sinf


def _lane_rows(rows):
    out = jnp.zeros((8, LANE), F32)
    for r, (off, vec) in enumerate(rows):
        out = out.at[r, off:off + vec.shape[0]].set(vec.astype(F32))
    return out


def kernel(x, c, ctx, c_ctx, ada_w, ada_b, norm_g, w_in, na_qk_g, na_rpb, ssm_conv_w, ssm_conv_b,
           ssm_dt_bias, ssm_a_log, ssm_d, ssm_norm_g, ml_gate_b, ml_norm_g, w_out,
           router_w, router_b, exp_w1, exp_b1, exp_w2, exp_b2):
    bsz, n_lat, d = x.shape
    n_ctx = ctx.shape[1]
    rows = n_lat // GRID_W
    depth = w_in.shape[0]
    assert bsz < 8 and n_ctx == TOK_TILE and n_lat % TOK_TILE == 0 and rows >= NA_KROWS

    h = jnp.concatenate([ctx, x], axis=1)
    cond = jnp.zeros((8, d), F32).at[:bsz].set(c).at[bsz].set(c_ctx)
    cosf, sinf = _rope_tables(n_ctx, n_lat)
    perm = _in_weight_perm()
    seg = np.arange(2 * LANE) // NA_HEAD_DIM
    bd = jnp.asarray(seg[:, None] == seg[None, :], BF16)

    out = None
    for layer in range(depth):
        ctx_out = layer < depth - 1
        mod = _ada(cond, ada_w[layer], ada_b[layer])
        mods = []
        for m in jnp.split(mod, 6, axis=-1):
            lat = m[:bsz]
            cx = jnp.broadcast_to(m[bsz][None], (bsz, d))
            mods.append(jnp.stack([cx, lat], axis=1).reshape(bsz, 2, 1, d))
        sh1, sc1, g1, sh2, sc2, g2 = mods

        w = jnp.where(perm[None, :] >= 0, w_in[layer][:, np.maximum(perm, 0)], 0.0).astype(BF16)
        qkg = jnp.concatenate([jnp.tile(na_qk_g[layer, 0] * NA_HEAD_DIM ** -0.5, NA_HEADS),
                               jnp.tile(na_qk_g[layer, 1], NA_HEADS)]).reshape(1, 2 * NA_WIDTH)
        big, small = _inproj(h, sh1, sc1, norm_g[layer, 0].reshape(1, d), w, qkg, bd, cosf, sinf)

        bias = _attention_bias(na_rpb[layer], rows)
        ya = _attention(big, bias, n_ctx, rows, ctx_out)

        conv_w = jnp.zeros((8, SSM_CONV_CH), F32).at[:SSM_CONV].set(ssm_conv_w[layer])
        ssm_par = jnp.stack([_lane_rows([(0, ssm_dt_bias[layer, dr]), (0, ssm_a_log[layer, dr]),
                                         (0, ssm_d[layer, dr])]) for dr in range(2)])
        yb = _ssd(big, small, conv_w, ssm_conv_b[layer].reshape(1, -1), ssm_par, n_ctx)

        gate_row = [jnp.concatenate([jnp.zeros((ML_HEADS,), F32), ml_gate_b[layer, dr, 0],
                                     ml_gate_b[layer, dr, 1]]) for dr in range(2)]
        ml_par = jnp.stack([_lane_rows([(0, gate_row[dr])]) for dr in range(2)])
        yc = _mlstm(big, small, ml_par, n_ctx)

        rw = jnp.zeros((d, LANE), F32).at[:, :N_EXPERTS].set(router_w[layer])
        rwh = rw.astype(BF16)
        rwl = (rw - rwh.astype(F32)).astype(BF16)
        rb = jnp.full((1, LANE), NEG, F32).at[0, :N_EXPERTS].set(router_b[layer])
        hn, v, gates = _merge(ya, yb, yc, big, h, g1, sh2, sc2, norm_g[layer, 1].reshape(1, d),
                              ssm_norm_g[layer].reshape(1, -1), ml_norm_g[layer].reshape(1, -1), bd,
                              w_out[layer].astype(BF16), rwh, rwl, rb, n_ctx, ctx_out)

        n_tok = hn.shape[0] * hn.shape[1]
        f = _moe(v.reshape(n_tok, d), gates.reshape(n_tok, LANE),
                 exp_w1[layer].astype(BF16), exp_b1[layer][:, None, :],
                 exp_w2[layer].astype(BF16), exp_b2[layer][:, None, :])
        h = _resid(hn, f.reshape(hn.shape), g2, 0 if ctx_out else 1)
        out = h
    return out
```

```python
import functools
import math

import numpy as np
import jax
import jax.numpy as jnp
from jax import lax
from jax.experimental import pallas as pl
from jax.experimental.pallas import tpu as pltpu

F32 = jnp.float32
BF16 = jnp.bfloat16

GRID_W = 64
EPS = 1e-6
NA_HEADS = 8
NA_HEAD_DIM = 64
NA_WIN_H = 8
NA_WIN_W = 16
SSM_HEADS = 4
SSM_HEAD_DIM = 64
SSM_GROUPS = 2
SSM_STATE = 128
SSM_CONV = 5
ML_HEADS = 4
ML_HEAD_DIM = 64
ROPE_THETA = 10000.0
N_EXPERTS = 32
TOP_K = 4
SWIGLU_ALPHA = 1.702
SWIGLU_LIMIT = 7.0

NA_WIDTH = NA_HEADS * NA_HEAD_DIM
SSM_WIDTH = SSM_HEADS * SSM_HEAD_DIM
SSM_BC = SSM_GROUPS * SSM_STATE
SSM_CONV_CH = SSM_WIDTH + 2 * SSM_BC
ML_WIDTH = ML_HEADS * ML_HEAD_DIM
IN_SPLITS = (NA_WIDTH, NA_WIDTH, NA_WIDTH, SSM_CONV_CH, SSM_WIDTH, 2 * SSM_HEADS,
             ML_WIDTH, ML_WIDTH, ML_WIDTH, ML_WIDTH, 2 * ML_HEADS, 2 * ML_HEADS)
IN_OFFSETS = tuple(int(o) for o in np.cumsum((0,) + IN_SPLITS))

LANE = 128
CHUNK = 128
HALO = 16
TOK_TILE = 256
NA_ROWS = 2
NA_KROWS = NA_ROWS + NA_WIN_H
BIG_W = 3 * NA_WIDTH + SSM_CONV_CH + SSM_WIDTH + 4 * ML_WIDTH
SMALL_W = 2 * LANE
MOE_BLOCK = 1024
MOE_GROUP = 2
MOE_CAP = 160
SCAN_BATCH = 4
NEG = -1e30
VMEM_LIMIT = 48 * 1024 * 1024
MOE_VMEM_LIMIT = 56 * 1024 * 1024

COL_Q, COL_K, COL_V = 0, NA_WIDTH, 2 * NA_WIDTH
COL_XBC = 3 * NA_WIDTH
COL_Z = COL_XBC + SSM_CONV_CH
COL_MQ = COL_Z + SSM_WIDTH
COL_MK = COL_MQ + ML_WIDTH
COL_MV = COL_MK + ML_WIDTH
COL_MO = COL_MV + ML_WIDTH


def _cparams(sem):
    return pltpu.CompilerParams(dimension_semantics=sem, vmem_limit_bytes=VMEM_LIMIT)


def _sigmoid(x):
    return 1.0 / (1.0 + jnp.exp(-x))


def _softplus(x):
    return jnp.maximum(x, 0.0) + jnp.log1p(jnp.exp(-jnp.abs(x)))


def _dot(a, b):
    return jnp.dot(a, b, preferred_element_type=F32)


def _dot_nt(a, b):
    return lax.dot_general(a, b, (((1,), (1,)), ((), ())), preferred_element_type=F32)


def _dot_tn(a, b):
    return lax.dot_general(a, b, (((0,), (0,)), ((), ())), preferred_element_type=F32)


def _split3(x):
    x1 = x.astype(BF16)
    r1 = x - x1.astype(F32)
    x2 = r1.astype(BF16)
    x3 = (r1 - x2.astype(F32)).astype(BF16)
    return x1, x2, x3


def _split2(x):
    x1 = x.astype(BF16)
    return x1, (x - x1.astype(F32)).astype(BF16)


def _segment_mean_sq(x, bd):
    hi, lo = _split2(x * x)
    return (_dot(hi, bd) + _dot(lo, bd)) * (1.0 / NA_HEAD_DIM)


def _ada_kernel(c_ref, w_ref, b_ref, o_ref):
    c = c_ref[...]
    s = c * _sigmoid(c)
    o_ref[...] = jnp.dot(s, w_ref[...], preferred_element_type=F32,
                         precision=lax.Precision.HIGHEST) + b_ref[...]


def _ada(cond, w, b):
    n, d = cond.shape
    wcols = w.shape[1]
    bn = wcols // 4
    return pl.pallas_call(
        _ada_kernel,
        out_shape=jax.ShapeDtypeStruct((n, wcols), F32),
        grid=(4,),
        in_specs=[pl.BlockSpec((n, d), lambda j: (0, 0)),
                  pl.BlockSpec((d, bn), lambda j: (0, j)),
                  pl.BlockSpec((1, bn), lambda j: (0, j))],
        out_specs=pl.BlockSpec((n, bn), lambda j: (0, j)),
        compiler_params=_cparams(("arbitrary",)),
        name="ada",
    )(cond, w, b.reshape(1, wcols))


def _inproj_kernel(h_ref, sh_ref, sc_ref, g_ref, w_ref, qkg_ref, bd_ref, cos_ref, sin_ref,
                   big_ref, small_ref):
    x = h_ref[...]
    ms = jnp.mean(x * x, axis=-1, keepdims=True)
    u = x * lax.rsqrt(ms + EPS) * g_ref[...]
    u = u * (1.0 + sc_ref[...]) + sh_ref[...]
    ub = u.astype(BF16)
    bd = bd_ref[...]

    for c0 in range(0, 2 * NA_WIDTH, 2 * LANE):
        acc = _dot(ub, w_ref[:, c0:c0 + 2 * LANE])
        msq = _segment_mean_sq(acc, bd)
        y = acc * lax.rsqrt(msq + EPS) * qkg_ref[:, c0:c0 + 2 * LANE]
        big_ref[:, c0:c0 + 2 * LANE] = y.astype(BF16)
    for c0 in range(COL_V, COL_MQ, 2 * LANE):
        big_ref[:, c0:c0 + 2 * LANE] = _dot(ub, w_ref[:, c0:c0 + 2 * LANE]).astype(BF16)
    lane = lax.broadcasted_iota(jnp.int32, (x.shape[0], LANE), 1)
    first_half = (lane % ML_HEAD_DIM) < (ML_HEAD_DIM // 2)
    for base, scale in ((COL_MQ, ML_HEAD_DIM ** -0.5), (COL_MK, 1.0)):
        for j in range(ML_WIDTH // LANE):
            c0 = base + j * LANE
            acc = _dot(ub, w_ref[:, c0:c0 + LANE])
            swapped = jnp.where(first_half,
                                pltpu.roll(acc, LANE - ML_HEAD_DIM // 2, 1),
                                pltpu.roll(acc, ML_HEAD_DIM // 2, 1))
            y = acc * cos_ref[:, j * LANE:(j + 1) * LANE] + swapped * sin_ref[:, j * LANE:(j + 1) * LANE]
            big_ref[:, c0:c0 + LANE] = (y * scale).astype(BF16)
    for c0 in range(COL_MV, BIG_W, 2 * LANE):
        big_ref[:, c0:c0 + 2 * LANE] = _dot(ub, w_ref[:, c0:c0 + 2 * LANE]).astype(BF16)
    small_ref[...] = _dot(ub, w_ref[:, BIG_W:BIG_W + SMALL_W])


def _inproj(h, sh, sc, g, w, qkg, bd, cosf, sinf):
    bsz, s_len, d = h.shape
    tm = TOK_TILE
    mod_spec = pl.BlockSpec((None, None, 1, d), lambda b, i: (b, jnp.minimum(i, 1), 0, 0))
    full = lambda shape: pl.BlockSpec(shape, lambda b, i: (0,) * len(shape))
    return pl.pallas_call(
        _inproj_kernel,
        out_shape=(jax.ShapeDtypeStruct((bsz, s_len, BIG_W), BF16),
                   jax.ShapeDtypeStruct((bsz, s_len, SMALL_W), F32)),
        grid=(bsz, s_len // tm),
        in_specs=[pl.BlockSpec((None, tm, d), lambda b, i: (b, i, 0)),
                  mod_spec, mod_spec, full((1, d)), full(w.shape), full(qkg.shape), full(bd.shape),
                  pl.BlockSpec((tm, ML_WIDTH), lambda b, i: (i, 0)),
                  pl.BlockSpec((tm, ML_WIDTH), lambda b, i: (i, 0))],
        out_specs=(pl.BlockSpec((None, tm, BIG_W), lambda b, i: (b, i, 0)),
                   pl.BlockSpec((None, tm, SMALL_W), lambda b, i: (b, i, 0))),
        compiler_params=_cparams(("arbitrary", "arbitrary")),
        name="inproj",
    )(h, sh, sc, g, w, qkg, bd, cosf, sinf)


def _na_kernel(q_ref, k0, k1, k2, k3, k4, v0, v1, v2, v3, v4, kc_ref, vc_ref, bias_ref, o_ref):
    nq = q_ref.shape[0]
    lane = lax.broadcasted_iota(jnp.int32, (nq, LANE), 1)
    low = lane < NA_HEAD_DIM
    zero = jnp.zeros((), BF16)
    for p in range(NA_WIDTH // LANE):
        cs = slice(p * LANE, (p + 1) * LANE)
        qp = q_ref[:, cs]
        q01 = jnp.concatenate([jnp.where(low, qp, zero), jnp.where(low, zero, qp)], axis=0)
        kw = jnp.concatenate([r[:, cs] for r in (k0, k1, k2, k3, k4)], axis=0)
        vw = jnp.concatenate([r[:, cs] for r in (v0, v1, v2, v3, v4)], axis=0)
        s_w = _dot_nt(q01, kw) + bias_ref[p]
        s_c = _dot_nt(q01, kc_ref[:, cs])
        m = jnp.maximum(jnp.max(s_w, axis=-1, keepdims=True), jnp.max(s_c, axis=-1, keepdims=True))
        p_w = jnp.exp(s_w - m)
        p_c = jnp.exp(s_c - m)
        den = jnp.sum(p_w, axis=-1, keepdims=True) + jnp.sum(p_c, axis=-1, keepdims=True)
        o01 = (_dot(p_w.astype(BF16), vw) + _dot(p_c.astype(BF16), vc_ref[:, cs])) / den
        o_ref[:, cs] = jnp.where(low, o01[:nq], o01[nq:]).astype(o_ref.dtype)


def _attention(big, bias, n_ctx, rows, ctx_out):
    bsz, s_len, _ = big.shape
    nq = NA_ROWS * GRID_W
    ncb = n_ctx // nq
    nrp = rows // NA_ROWS
    first = 0 if ctx_out else ncb
    n_win = NA_KROWS // NA_ROWS

    def lat(i):
        return jnp.maximum(i + first - ncb, 0)

    def kv_spec(j, col):
        return pl.BlockSpec(
            (None, nq, NA_WIDTH),
            lambda b, i: (b, ncb + jnp.clip(lat(i) - 2, 0, nrp - n_win) + j, col))

    def bias_type(b, i):
        step = i + first
        r = lat(i)
        t = jnp.where(r < 2, r, jnp.where(r >= nrp - 2, r - (nrp - 2) + 3, 2))
        return (jnp.where(step < ncb, 5, t), 0, 0, 0)

    return pl.pallas_call(
        _na_kernel,
        out_shape=jax.ShapeDtypeStruct((bsz, s_len, NA_WIDTH), BF16),
        grid=(bsz, ncb + nrp - first),
        in_specs=[pl.BlockSpec((None, nq, NA_WIDTH), lambda b, i: (b, i + first, 0))]
                 + [kv_spec(j, 1) for j in range(n_win)] + [kv_spec(j, 2) for j in range(n_win)]
                 + [pl.BlockSpec((None, n_ctx, NA_WIDTH), lambda b, i: (b, 0, 1)),
                    pl.BlockSpec((None, n_ctx, NA_WIDTH), lambda b, i: (b, 0, 2)),
                    pl.BlockSpec((None, NA_WIDTH // LANE, 2 * nq, NA_KROWS * GRID_W), bias_type)],
        out_specs=pl.BlockSpec((None, nq, NA_WIDTH), lambda b, i: (b, i + first, 0)),
        compiler_params=_cparams(("arbitrary", "arbitrary")),
        name="attention",
    )(big, *([big] * (2 * n_win + 2)), bias)


def _attention_bias(rpb, rows):
    nrp = rows // NA_ROWS
    n_win = NA_KROWS // NA_ROWS
    reps = np.array([0, 1, 2, nrp - 2, nrp - 1])
    qr = (NA_ROWS * reps)[:, None] + np.arange(NA_ROWS)[None, :]
    ws = NA_ROWS * np.clip(reps - 2, 0, nrp - n_win)
    kr = ws[:, None] + np.arange(NA_KROWS)[None, :]
    r0 = np.clip(qr - NA_WIN_H // 2, 0, rows - NA_WIN_H)
    ok_r = (kr[:, None, :] >= r0[:, :, None]) & (kr[:, None, :] < r0[:, :, None] + NA_WIN_H)
    dr = np.clip(kr[:, None, :] - qr[:, :, None] + (NA_WIN_H - 1), 0, 2 * NA_WIN_H - 2)
    qc = np.arange(GRID_W)
    c0 = np.clip(qc - NA_WIN_W // 2, 0, GRID_W - NA_WIN_W)
    kc = np.arange(GRID_W)
    ok_c = (kc[None, :] >= c0[:, None]) & (kc[None, :] < c0[:, None] + NA_WIN_W)
    dc = np.clip(kc[None, :] - qc[:, None] + (NA_WIN_W - 1), 0, 2 * NA_WIN_W - 2)
    sel_r = jnp.asarray(dr[..., None] == np.arange(2 * NA_WIN_H - 1), F32)
    sel_c = jnp.asarray(dc[..., None] == np.arange(2 * NA_WIN_W - 1), F32)
    hp = lax.Precision.HIGHEST
    g = jnp.einsum('hrc,taer->thaec', rpb.astype(F32), sel_r, precision=hp)
    g = jnp.einsum('thaec,qkc->thaqek', g, sel_c, precision=hp)
    ok = ok_r[:, None, :, None, :, None] & ok_c[None, None, None, :, None, :]
    g = jnp.where(ok, g, NEG)
    g = g.reshape(5, NA_HEADS * NA_ROWS * GRID_W, NA_KROWS * GRID_W)
    g = g.reshape(5, NA_WIDTH // LANE, 2 * NA_ROWS * GRID_W, NA_KROWS * GRID_W)
    masked = jnp.full((1,) + g.shape[1:], NEG, F32)
    return jnp.concatenate([g, masked], axis=0)


def _scan_chunk(d, t, ncc, nc):
    bwd = jnp.where(t < ncc, ncc - 1 - t, ncc + (nc - 1 - t))
    return jnp.where(d == 0, t, bwd)


def _precedes(d):
    ii = lax.broadcasted_iota(jnp.int32, (CHUNK, CHUNK), 0)
    jj = lax.broadcasted_iota(jnp.int32, (CHUNK, CHUNK), 1)
    return (jj - ii) * (1 - 2 * d) <= 0


def _cumsum_scan_order(prec, x):
    tri = jnp.where(prec, 1.0, 0.0).astype(BF16)
    x1, x2, x3 = _split3(x)
    return _dot(tri, x1) + _dot(tri, x2) + _dot(tri, x3)


def _ssd_kernel(x_ref, xp_ref, xn_ref, sm_ref, cw_ref, cb_ref, par_ref, y_ref, st_ref, cbuf_ref,
                *, ncc, nc):
    d = pl.program_id(1)
    t = pl.program_id(2)
    chunk = _scan_chunk(d, t, ncc, nc)

    @pl.when(t == 0)
    def _():
        st_ref[...] = jnp.zeros_like(st_ref)

    prev_ok = jnp.where(jnp.logical_and(chunk != 0, chunk != ncc), 1.0, 0.0)
    next_ok = jnp.where(jnp.logical_and(chunk != ncc - 1, chunk != nc - 1), 1.0, 0.0)
    prec = _precedes(d)
    rep = SSM_HEADS // SSM_GROUPS
    nb = x_ref.shape[0]
    for bb in range(nb):
        cbuf_ref[bb, 0:HALO, :] = xp_ref[bb].astype(F32) * prev_ok
        cbuf_ref[bb, HALO:HALO + CHUNK, :] = x_ref[bb].astype(F32)
        cbuf_ref[bb, HALO + CHUNK:, :] = xn_ref[bb].astype(F32) * next_ok
    taps = [[cbuf_ref[bb, HALO - SSM_CONV // 2 + k:HALO - SSM_CONV // 2 + k + CHUNK, :]
             for k in range(SSM_CONV)] for bb in range(nb)]
    gates_in = [sm_ref[bb] for bb in range(nb)]
    states = [[st_ref[bb, h] for h in range(SSM_HEADS)] for bb in range(nb)]
    stores = []
    for bb in range(nb):
        acc = jnp.broadcast_to(cb_ref[...], (CHUNK, SSM_CONV_CH))
        for k in range(SSM_CONV):
            acc = acc + taps[bb][k] * cw_ref[k:k + 1, :]
        xc = acc * _sigmoid(acc)
        xs = xc[:, :SSM_WIDTH]
        bm = xc[:, SSM_WIDTH:SSM_WIDTH + SSM_BC]
        cm = xc[:, SSM_WIDTH + SSM_BC:]

        dt = _softplus(gates_in[bb] + par_ref[0:1, :])
        la = dt * (-jnp.exp(par_ref[1:2, :]))
        acum = _cumsum_scan_order(prec, la)
        acum_t = acum.T
        total = jnp.sum(la, axis=0, keepdims=True)

        for g in range(SSM_GROUPS):
            cg = cm[:, g * SSM_STATE:(g + 1) * SSM_STATE].astype(BF16)
            bg = bm[:, g * SSM_STATE:(g + 1) * SSM_STATE].astype(BF16)
            cb = _dot_nt(cg, bg)
            for hh in range(rep):
                h = g * rep + hh
                a_col = acum[:, h:h + 1]
                a_row = acum_t[h:h + 1, :]
                tot = total[:, h:h + 1]
                x_h = xs[:, h * SSM_HEAD_DIM:(h + 1) * SSM_HEAD_DIM]
                xdt = x_h * dt[:, h:h + 1]
                lmat = jnp.exp(jnp.where(prec, a_col - a_row, -jnp.inf))
                h_in = states[bb][h]
                y = _dot((cb * lmat).astype(BF16), xdt.astype(BF16))
                y = y + _dot(cg, h_in.astype(BF16)) * jnp.exp(a_col)
                y = y + x_h * par_ref[2:3, h:h + 1]
                xw = (xdt * jnp.exp(tot - a_col)).astype(BF16)
                stores.append((bb, h, y, jnp.exp(tot) * h_in + _dot_tn(bg, xw)))

    for bb, h, y, h_new in stores:
        y_ref[bb, :, h * SSM_HEAD_DIM:(h + 1) * SSM_HEAD_DIM] = y
        st_ref[bb, h] = h_new


def _scan_batch(bsz):
    return math.gcd(bsz, SCAN_BATCH)


def _ssd(big, small, conv_w, conv_b, par, n_ctx):
    bsz, s_len, _ = big.shape
    ncc, nc = n_ctx // CHUNK, s_len // CHUNK
    per = CHUNK // HALO
    nb = _scan_batch(bsz)
    ch = functools.partial(_scan_chunk, ncc=ncc, nc=nc)
    cblk = COL_XBC // SSM_CONV_CH
    return pl.pallas_call(
        functools.partial(_ssd_kernel, ncc=ncc, nc=nc),
        out_shape=jax.ShapeDtypeStruct((2, bsz, s_len, SSM_WIDTH), F32),
        grid=(bsz // nb, 2, nc),
        in_specs=[
            pl.BlockSpec((nb, CHUNK, SSM_CONV_CH), lambda b, d, t: (b, ch(d, t), cblk)),
            pl.BlockSpec((nb, HALO, SSM_CONV_CH),
                         lambda b, d, t: (b, jnp.maximum(ch(d, t) * per - 1, 0), cblk)),
            pl.BlockSpec((nb, HALO, SSM_CONV_CH),
                         lambda b, d, t: (b, jnp.minimum((ch(d, t) + 1) * per, nc * per - 1), cblk)),
            pl.BlockSpec((nb, CHUNK, LANE), lambda b, d, t: (b, ch(d, t), d)),
            pl.BlockSpec(conv_w.shape, lambda b, d, t: (0, 0)),
            pl.BlockSpec(conv_b.shape, lambda b, d, t: (0, 0)),
            pl.BlockSpec((None, 8, LANE), lambda b, d, t: (d, 0, 0)),
        ],
        out_specs=pl.BlockSpec((None, nb, CHUNK, SSM_WIDTH), lambda b, d, t: (d, b, ch(d, t), 0)),
        scratch_shapes=[pltpu.VMEM((nb, SSM_HEADS, SSM_STATE, SSM_HEAD_DIM), F32),
                        pltpu.VMEM((nb, CHUNK + 2 * HALO, SSM_CONV_CH), F32)],
        compiler_params=_cparams(("arbitrary", "arbitrary", "arbitrary")),
        name="ssd",
    )(big, big, big, small, conv_w, conv_b, par)


def _mlstm_kernel(q_ref, k_ref, v_ref, sm_ref, par_ref, y_ref, ct_ref, n_ref, m_ref, *, ncc, nc):
    d = pl.program_id(1)
    t = pl.program_id(2)

    @pl.when(t == 0)
    def _():
        ct_ref[...] = jnp.zeros_like(ct_ref)
        n_ref[...] = jnp.zeros_like(n_ref)
        m_ref[...] = jnp.zeros_like(m_ref)

    prec = _precedes(d)
    nb = q_ref.shape[0]
    loaded = [(sm_ref[bb], q_ref[bb], k_ref[bb], v_ref[bb],
               [(m_ref[bb, h][0:1, 0:1], ct_ref[bb, h], n_ref[bb, h][0:1, :]) for h in range(ML_HEADS)])
              for bb in range(nb)]
    stores = []
    for bb in range(nb):
        sm, q_all, k_all, v_all, states = loaded[bb]
        pre = sm + par_ref[0:1, :]
        lf = -_softplus(-pre)
        bcum = _cumsum_scan_order(prec, lf)
        bcum_t = bcum.T
        li_t = pre.T
        gsum = jnp.sum(lf, axis=0, keepdims=True)

        for h in range(ML_HEADS):
            hs = slice(h * ML_HEAD_DIM, (h + 1) * ML_HEAD_DIM)
            ci, cf = ML_HEADS + h, 2 * ML_HEADS + h
            b_col = bcum[:, cf:cf + 1]
            b_row = bcum_t[cf:cf + 1, :]
            gt = gsum[:, cf:cf + 1]
            li_col = pre[:, ci:ci + 1]
            li_row = li_t[ci:ci + 1, :]
            qh = q_all[:, hs]
            kh = k_all[:, hs]
            vh = v_all[:, hs]
            m_in, ct_in, n_in = states[h]

            log_d = jnp.where(prec, b_col - b_row + li_row, -jnp.inf)
            log_inter = b_col + m_in
            m_i = jnp.maximum(jnp.max(log_d, axis=-1, keepdims=True), log_inter)
            s = _dot_nt(qh, kh) * jnp.exp(log_d - m_i)
            inter = jnp.exp(log_inter - m_i)
            num = _dot(s.astype(BF16), vh) + _dot(qh, ct_in.astype(BF16)) * inter
            den = (jnp.sum(s, axis=-1, keepdims=True)
                   + jnp.sum(qh.astype(F32) * n_in, axis=-1, keepdims=True) * inter)
            y = num / jnp.maximum(jnp.abs(den), jnp.exp(-m_i))

            w_end = gt - b_col + li_col
            m_loc = jnp.max(w_end, axis=0, keepdims=True)
            ke = kh.astype(F32) * jnp.exp(w_end - m_loc)
            m_new = jnp.maximum(gt + m_in, m_loc)
            a_old = jnp.exp(gt + m_in - m_new)
            a_new = jnp.exp(m_loc - m_new)
            ct_new = a_old * ct_in + a_new * _dot_tn(ke.astype(BF16), vh)
            n_new = a_old * n_in + a_new * jnp.sum(ke, axis=0, keepdims=True)
            stores.append((bb, h, hs, y, ct_new, n_new, m_new))

    for bb, h, hs, y, ct_new, n_new, m_new in stores:
        y_ref[bb, :, hs] = y
        ct_ref[bb, h] = ct_new
        n_ref[bb, h] = jnp.broadcast_to(n_new, n_ref.shape[2:])
        m_ref[bb, h] = jnp.broadcast_to(m_new, m_ref.shape[2:])


def _mlstm(big, small, par, n_ctx):
    bsz, s_len, _ = big.shape
    ncc, nc = n_ctx // CHUNK, s_len // CHUNK
    nb = 1
    ch = functools.partial(_scan_chunk, ncc=ncc, nc=nc)

    def col_spec(col):
        return pl.BlockSpec((nb, CHUNK, ML_WIDTH), lambda b, d, t: (b, ch(d, t), col // ML_WIDTH))

    return pl.pallas_call(
        functools.partial(_mlstm_kernel, ncc=ncc, nc=nc),
        out_shape=jax.ShapeDtypeStruct((2, bsz, s_len, ML_WIDTH), F32),
        grid=(bsz // nb, 2, nc),
        in_specs=[col_spec(COL_MQ), col_spec(COL_MK), col_spec(COL_MV),
                  pl.BlockSpec((nb, CHUNK, LANE), lambda b, d, t: (b, ch(d, t), d)),
                  pl.BlockSpec((None, 8, LANE), lambda b, d, t: (d, 0, 0))],
        out_specs=pl.BlockSpec((None, nb, CHUNK, ML_WIDTH), lambda b, d, t: (d, b, ch(d, t), 0)),
        scratch_shapes=[pltpu.VMEM((nb, ML_HEADS, ML_HEAD_DIM, ML_HEAD_DIM), F32),
                        pltpu.VMEM((nb, ML_HEADS, 8, ML_HEAD_DIM), F32),
                        pltpu.VMEM((nb, ML_HEADS, 8, LANE), F32)],
        compiler_params=_cparams(("arbitrary", "arbitrary", "arbitrary")),
        name="mlstm",
    )(big, big, big, small, par)


def _merge_kernel(ya_ref, yb_ref, z_ref, yc_ref, o_ref, h_ref, g1_ref, sh_ref, sc_ref, ng_ref,
                  sg_ref, mg_ref, bd_ref, wo_ref, rwh_ref, rwl_ref, rb_ref,
                  hout_ref, v_ref, gates_ref):
    z = z_ref[...].astype(F32)
    yb = (yb_ref[0] + yb_ref[1]) * (z * _sigmoid(z))
    yb = yb * lax.rsqrt(jnp.mean(yb * yb, axis=-1, keepdims=True) + EPS) * sg_ref[...]
    yc = _sigmoid(o_ref[...].astype(F32)) * (yc_ref[0] + yc_ref[1])
    yc = yc * lax.rsqrt(_segment_mean_sq(yc, bd_ref[...]) + EPS) * mg_ref[...]
    y = (_dot(ya_ref[...], wo_ref[0:NA_WIDTH, :])
         + _dot(yb.astype(BF16), wo_ref[NA_WIDTH:NA_WIDTH + SSM_WIDTH, :])
         + _dot(yc.astype(BF16), wo_ref[NA_WIDTH + SSM_WIDTH:, :]))
    hn = h_ref[...] + g1_ref[...] * y
    hout_ref[...] = hn
    v = hn * lax.rsqrt(jnp.mean(hn * hn, axis=-1, keepdims=True) + EPS) * ng_ref[...]
    v = v * (1.0 + sc_ref[...]) + sh_ref[...]
    v_hi, v_lo = _split2(v)
    v_ref[...] = v_hi
    rwh = rwh_ref[...]
    logits = _dot(v_hi, rwh) + _dot(v_lo, rwh) + _dot(v_hi, rwl_ref[...]) + rb_ref[...]

    lane = lax.broadcasted_iota(jnp.int32, logits.shape, 1).astype(F32)
    work = logits
    top = None
    gates = jnp.zeros_like(logits)
    den = jnp.zeros((logits.shape[0], 1), F32)
    for _ in range(TOP_K):
        mx = jnp.max(work, axis=-1, keepdims=True)
        idx = jnp.min(jnp.where(work == mx, lane, float(LANE)), axis=-1, keepdims=True)
        hit = lane == idx
        top = mx if top is None else top
        e = jnp.exp(mx - top)
        gates = gates + jnp.where(hit, e, 0.0)
        den = den + e
        work = jnp.where(hit, -jnp.inf, work)
    gates_ref[...] = gates / den


def _merge(ya, yb, yc, big, h, g1, sh2, sc2, ng, sg, mg, bd, wo, rwh, rwl, rb, n_ctx, ctx_out):
    bsz, s_len, d = h.shape
    tm = TOK_TILE
    first = 0 if ctx_out else n_ctx // tm
    n_out = s_len - first * tm
    mod_spec = pl.BlockSpec((None, None, 1, d), lambda b, i: (b, jnp.minimum(i + first, 1), 0, 0))
    full = lambda shape: pl.BlockSpec(shape, lambda b, i: (0,) * len(shape))
    pair = lambda w: pl.BlockSpec((2, None, tm, w), lambda b, i: (0, b, i + first, 0))
    nl = (s_len - n_ctx) // tm

    def flat_row(b, i):
        if ctx_out:
            return jnp.where(i == 0, bsz * nl + b, b * nl + i - 1)
        return b * nl + i

    flat_tok = lambda w: pl.BlockSpec((tm, w), lambda b, i: (flat_row(b, i), 0))
    return pl.pallas_call(
        _merge_kernel,
        out_shape=(jax.ShapeDtypeStruct((bsz, n_out, d), F32),
                   jax.ShapeDtypeStruct((bsz * n_out, d), BF16),
                   jax.ShapeDtypeStruct((bsz * n_out, LANE), F32)),
        grid=(bsz, n_out // tm),
        in_specs=[pl.BlockSpec((None, tm, NA_WIDTH), lambda b, i: (b, i + first, 0)),
                  pair(SSM_WIDTH),
                  pl.BlockSpec((None, tm, SSM_WIDTH), lambda b, i: (b, i + first, COL_Z // SSM_WIDTH)),
                  pair(ML_WIDTH),
                  pl.BlockSpec((None, tm, ML_WIDTH), lambda b, i: (b, i + first, COL_MO // ML_WIDTH)),
                  pl.BlockSpec((None, tm, d), lambda b, i: (b, i + first, 0)),
                  mod_spec, mod_spec, mod_spec,
                  full((1, d)), full((1, SSM_WIDTH)), full((1, ML_WIDTH)), full(bd.shape),
                  full(wo.shape), full(rwh.shape), full(rwl.shape), full((1, LANE))],
        out_specs=(pl.BlockSpec((None, tm, d), lambda b, i: (b, i, 0)), flat_tok(d), flat_tok(LANE)),
        compiler_params=_cparams(("arbitrary", "arbitrary")),
        name="merge",
    )(ya, yb, big, yc, big, h, g1, sh2, sc2, ng, sg, mg, bd, wo, rwh, rwl, rb)


def _moe_kernel(x_ref, gates_ref, w1_ref, b1_ref, w2_ref, b2_ref, f_ref, rank_ref, gate_ref, *, nblk):
    e = pl.program_id(1)
    tb = x_ref.shape[0] // nblk
    d_ff = w2_ref.shape[0]

    @pl.when(e == 0)
    def _():
        f_ref[...] = jnp.zeros_like(f_ref)
        ii = lax.broadcasted_iota(jnp.int32, (tb, tb), 0)
        jj = lax.broadcasted_iota(jnp.int32, (tb, tb), 1)
        tri = jnp.where(jj <= ii, 1.0, 0.0).astype(BF16)
        for k in range(nblk):
            gates = gates_ref[k * tb:(k + 1) * tb, :]
            picked = gates > 0.0
            incl = _dot(tri, jnp.where(picked, 1.0, 0.0).astype(BF16))
            rank_ref[k] = jnp.where(picked, incl - 1.0, -1.0).T
            gate_ref[k] = gates.T

    for k in range(nblk):
        rows = slice(k * tb, (k + 1) * tb)
        slot = rank_ref[k, pl.ds(e, 1), :]
        gate = gate_ref[k, pl.ds(e, 1), :]

        def sub_tile(s, rows=rows, slot=slot, gate=gate):
            row = (lax.broadcasted_iota(jnp.int32, (MOE_CAP, tb), 0) + s * MOE_CAP).astype(F32)
            hit = slot == row
            onehot = jnp.where(hit, 1.0, 0.0).astype(BF16)
            xe = _dot(onehot, x_ref[rows, :]).astype(BF16)
            w = jnp.sum(jnp.where(hit, gate, 0.0), axis=-1, keepdims=True)
            y = jnp.broadcast_to(b2_ref[...], (MOE_CAP, w2_ref.shape[1]))
            for c0 in range(0, d_ff, 2 * LANE):
                x_glu = _dot(xe, w1_ref[:, c0:c0 + 2 * LANE]) + b1_ref[:, c0:c0 + 2 * LANE]
                x_lin = (_dot(xe, w1_ref[:, d_ff + c0:d_ff + c0 + 2 * LANE])
                         + b1_ref[:, d_ff + c0:d_ff + c0 + 2 * LANE])
                x_glu = jnp.minimum(x_glu, SWIGLU_LIMIT)
                x_lin = jnp.clip(x_lin, -SWIGLU_LIMIT, SWIGLU_LIMIT)
                act = x_glu * _sigmoid(SWIGLU_ALPHA * x_glu) * (x_lin + 1.0)
                y = y + _dot(act.astype(BF16), w2_ref[c0:c0 + 2 * LANE, :])
            f_ref[rows, :] += _dot_tn(onehot, (y * w).astype(BF16))

        sub_tile(0)
        count = jnp.max(slot).astype(jnp.int32) + 1

        def overflow(s, carry, sub_tile=sub_tile):
            sub_tile(s)
            return carry

        lax.fori_loop(1, (count + MOE_CAP - 1) // MOE_CAP, overflow, 0)


def _moe(x, gates, w1, b1, w2, b2, layer, first_tok, n_tok):
    d = x.shape[1]
    nblk = MOE_GROUP if n_tok % (MOE_GROUP * MOE_BLOCK) == 0 else 1
    tb = math.gcd(n_tok, MOE_BLOCK)
    rows = nblk * tb
    d_ff = w2.shape[1]
    off = first_tok // rows
    assert first_tok % rows == 0 and n_tok % rows == 0
    expert = lambda i, e: (layer * N_EXPERTS + e, 0, 0)
    return pl.pallas_call(
        functools.partial(_moe_kernel, nblk=nblk),
        out_shape=jax.ShapeDtypeStruct((n_tok, d), F32),
        grid=(n_tok // rows, N_EXPERTS),
        in_specs=[pl.BlockSpec((rows, d), lambda i, e: (i + off, 0)),
                  pl.BlockSpec((rows, LANE), lambda i, e: (i + off, 0)),
                  pl.BlockSpec((None, d, 2 * d_ff), expert),
                  pl.BlockSpec((None, 1, 2 * d_ff), expert),
                  pl.BlockSpec((None, d_ff, d), expert),
                  pl.BlockSpec((None, 1, d), expert)],
        out_specs=pl.BlockSpec((rows, d), lambda i, e: (i, 0)),
        scratch_shapes=[pltpu.VMEM((nblk, LANE, tb), F32), pltpu.VMEM((nblk, LANE, tb), F32)],
        compiler_params=pltpu.CompilerParams(dimension_semantics=("arbitrary", "arbitrary"),
                                             vmem_limit_bytes=MOE_VMEM_LIMIT),
        name="moe",
    )(x, gates, w1, b1, w2, b2)


def _resid_kernel(h_ref, g_ref, fl_ref, *rest, first):
    if first == 0:
        fc_ref, o_ref = rest
        f = jnp.where(pl.program_id(1) == 0, fc_ref[...], fl_ref[...])
    else:
        (o_ref,) = rest
        f = fl_ref[...]
    o_ref[...] = h_ref[...] + g_ref[...] * f


def _resid(h, g2, f_lat, f_ctx):
    bsz, n_tok, d = h.shape
    tm = TOK_TILE
    first = 0 if f_ctx is not None else 1
    nl = n_tok // tm - (1 - first)
    in_specs = [pl.BlockSpec((None, tm, d), lambda b, i: (b, i, 0)),
                pl.BlockSpec((None, None, 1, d), lambda b, i: (b, jnp.minimum(i + first, 1), 0, 0)),
                pl.BlockSpec((tm, d), lambda b, i: (b * nl + jnp.maximum(i + first - 1, 0), 0))]
    args = [h, g2, f_lat]
    if f_ctx is not None:
        in_specs.append(pl.BlockSpec((tm, d), lambda b, i: (b, 0)))
        args.append(f_ctx)
    return pl.pallas_call(
        functools.partial(_resid_kernel, first=first),
        out_shape=jax.ShapeDtypeStruct((bsz, n_tok, d), F32),
        grid=(bsz, n_tok // tm),
        in_specs=in_specs,
        out_specs=pl.BlockSpec((None, tm, d), lambda b, i: (b, i, 0)),
        compiler_params=_cparams(("arbitrary", "arbitrary")),
        name="resid",
    )(*args)


def _in_weight_perm():
    o = IN_OFFSETS
    big = np.concatenate([np.arange(o[0], o[5]), np.arange(o[6], o[10])])
    small = np.full((2, LANE), -1, np.int64)
    for d in range(2):
        for h in range(SSM_HEADS):
            small[d, h] = o[5] + d * SSM_HEADS + h
        for h in range(ML_HEADS):
            small[d, ML_HEADS + h] = o[10] + d * ML_HEADS + h
            small[d, 2 * ML_HEADS + h] = o[11] + d * ML_HEADS + h
    return np.concatenate([big, small.reshape(-1)])


def _rope_tables(n_ctx, n_lat):
    pos = np.arange(n_lat)
    row = (pos // GRID_W).astype(np.float32)
    col = (pos % GRID_W).astype(np.float32)
    n_freq = ML_HEAD_DIM // 4
    inv_freq = jnp.asarray(ROPE_THETA, F32) ** (-jnp.arange(n_freq, dtype=F32) / n_freq)
    ang = jnp.concatenate([jnp.asarray(row)[:, None] * inv_freq, jnp.asarray(col)[:, None] * inv_freq], axis=-1)
    cos, sin = jnp.cos(ang), jnp.sin(ang)
    cosf = jnp.tile(jnp.concatenate([cos, cos], axis=-1), (1, ML_HEADS))
    sinf = jnp.tile(jnp.concatenate([-sin, sin], axis=-1), (1, ML_HEADS))
    cosf = jnp.concatenate([jnp.ones((n_ctx, ML_WIDTH), F32), cosf], axis=0)
    sinf = jnp.concatenate([jnp.zeros((n_ctx, ML_WIDTH), F32), sinf], axis=0)
    return cosf, sinf


def _lane_rows(rows):
    out = jnp.zeros((8, LANE), F32)
    for r, (off, vec) in enumerate(rows):
        out = out.at[r, off:off + vec.shape[0]].set(vec.astype(F32))
    return out


def kernel(x, c, ctx, c_ctx, ada_w, ada_b, norm_g, w_in, na_qk_g, na_rpb, ssm_conv_w, ssm_conv_b,
           ssm_dt_bias, ssm_a_log, ssm_d, ssm_norm_g, ml_gate_b, ml_norm_g, w_out,
           router_w, router_b, exp_w1, exp_b1, exp_w2, exp_b2):
    bsz, n_lat, d = x.shape
    n_ctx = ctx.shape[1]
    rows = n_lat // GRID_W
    depth = w_in.shape[0]
    assert bsz < 8 and n_ctx == TOK_TILE and n_lat % TOK_TILE == 0 and rows >= NA_KROWS

    h = jnp.concatenate([ctx, x], axis=1)
    cond = jnp.zeros((8, d), F32).at[:bsz].set(c).at[bsz].set(c_ctx)
    cosf, sinf = _rope_tables(n_ctx, n_lat)
    perm = _in_weight_perm()
    seg = np.arange(2 * LANE) // NA_HEAD_DIM
    bd = jnp.asarray(seg[:, None] == seg[None, :], BF16)

    n_exp = exp_w1.shape[0] * exp_w1.shape[1]
    w1_all = exp_w1.astype(BF16).reshape((n_exp,) + exp_w1.shape[2:])
    w2_all = exp_w2.astype(BF16).reshape((n_exp,) + exp_w2.shape[2:])
    b1_all = exp_b1.reshape(n_exp, 1, -1)
    b2_all = exp_b2.reshape(n_exp, 1, -1)

    out = None
    for layer in range(depth):
        ctx_out = layer < depth - 1
        mod = _ada(cond, ada_w[layer], ada_b[layer])
        mods = []
        for m in jnp.split(mod, 6, axis=-1):
            lat = m[:bsz]
            cx = jnp.broadcast_to(m[bsz][None], (bsz, d))
            mods.append(jnp.stack([cx, lat], axis=1).reshape(bsz, 2, 1, d))
        sh1, sc1, g1, sh2, sc2, g2 = mods

        w = jnp.where(perm[None, :] >= 0, w_in[layer][:, np.maximum(perm, 0)], 0.0).astype(BF16)
        qkg = jnp.concatenate([jnp.tile(na_qk_g[layer, 0] * NA_HEAD_DIM ** -0.5, NA_HEADS),
                               jnp.tile(na_qk_g[layer, 1], NA_HEADS)]).reshape(1, 2 * NA_WIDTH)
        big, small = _inproj(h, sh1, sc1, norm_g[layer, 0].reshape(1, d), w, qkg, bd, cosf, sinf)

        bias = _attention_bias(na_rpb[layer], rows)
        ya = _attention(big, bias, n_ctx, rows, ctx_out)

        conv_w = jnp.zeros((8, SSM_CONV_CH), F32).at[:SSM_CONV].set(ssm_conv_w[layer])
        ssm_par = jnp.stack([_lane_rows([(0, ssm_dt_bias[layer, dr]), (0, ssm_a_log[layer, dr]),
                                         (0, ssm_d[layer, dr])]) for dr in range(2)])
        yb = _ssd(big, small, conv_w, ssm_conv_b[layer].reshape(1, -1), ssm_par, n_ctx)

        gate_row = [jnp.concatenate([jnp.zeros((ML_HEADS,), F32), ml_gate_b[layer, dr, 0],
                                     ml_gate_b[layer, dr, 1]]) for dr in range(2)]
        ml_par = jnp.stack([_lane_rows([(0, gate_row[dr])]) for dr in range(2)])
        yc = _mlstm(big, small, ml_par, n_ctx)

        rw = jnp.zeros((d, LANE), F32).at[:, :N_EXPERTS].set(router_w[layer])
        rwh = rw.astype(BF16)
        rwl = (rw - rwh.astype(F32)).astype(BF16)
        rb = jnp.full((1, LANE), NEG, F32).at[0, :N_EXPERTS].set(router_b[layer])
        hn, v, gates = _merge(ya, yb, yc, big, h, g1, sh2, sc2, norm_g[layer, 1].reshape(1, d),
                              ssm_norm_g[layer].reshape(1, -1), ml_norm_g[layer].reshape(1, -1), bd,
                              w_out[layer].astype(BF16), rwh, rwl, rb, n_ctx, ctx_out)

        n_lat_tok = bsz * n_lat
        f_lat = _moe(v, gates, w1_all, b1_all, w2_all, b2_all, layer, 0, n_lat_tok)
        f_ctx = (_moe(v, gates, w1_all, b1_all, w2_all, b2_all, layer, n_lat_tok, bsz * n_ctx)
                 if ctx_out else None)
        h = _resid(hn, g2, f_lat, f_ctx)
        out = h
    return out
```

```python
import functools
import math

import numpy as np
import jax
import jax.numpy as jnp
from jax import lax
from jax.experimental import pallas as pl
from jax.experimental.pallas import tpu as pltpu

F32 = jnp.float32
BF16 = jnp.bfloat16

GRID_W = 64
EPS = 1e-6
NA_HEADS = 8
NA_HEAD_DIM = 64
NA_WIN_H = 8
NA_WIN_W = 16
SSM_HEADS = 4
SSM_HEAD_DIM = 64
SSM_GROUPS = 2
SSM_STATE = 128
SSM_CONV = 5
ML_HEADS = 4
ML_HEAD_DIM = 64
ROPE_THETA = 10000.0
N_EXPERTS = 32
TOP_K = 4
SWIGLU_ALPHA = 1.702
SWIGLU_LIMIT = 7.0

NA_WIDTH = NA_HEADS * NA_HEAD_DIM
SSM_WIDTH = SSM_HEADS * SSM_HEAD_DIM
SSM_BC = SSM_GROUPS * SSM_STATE
SSM_CONV_CH = SSM_WIDTH + 2 * SSM_BC
ML_WIDTH = ML_HEADS * ML_HEAD_DIM
IN_SPLITS = (NA_WIDTH, NA_WIDTH, NA_WIDTH, SSM_CONV_CH, SSM_WIDTH, 2 * SSM_HEADS,
             ML_WIDTH, ML_WIDTH, ML_WIDTH, ML_WIDTH, 2 * ML_HEADS, 2 * ML_HEADS)
IN_OFFSETS = tuple(int(o) for o in np.cumsum((0,) + IN_SPLITS))

LANE = 128
CHUNK = 128
HALO = 16
TOK_TILE = 256
NA_ROWS = 2
NA_KROWS = NA_ROWS + NA_WIN_H
BIG_W = 3 * NA_WIDTH + SSM_CONV_CH + SSM_WIDTH + 4 * ML_WIDTH
SMALL_W = 2 * LANE
MOE_BLOCK = 1024
MOE_GROUP = 2
MOE_CAP = 160
MOE_CAP_PAD = 256
SCAN_BATCH = 4
NEG = -1e30
VMEM_LIMIT = 48 * 1024 * 1024
MOE_VMEM_LIMIT = 56 * 1024 * 1024

COL_Q, COL_K, COL_V = 0, NA_WIDTH, 2 * NA_WIDTH
COL_XBC = 3 * NA_WIDTH
COL_Z = COL_XBC + SSM_CONV_CH
COL_MQ = COL_Z + SSM_WIDTH
COL_MK = COL_MQ + ML_WIDTH
COL_MV = COL_MK + ML_WIDTH
COL_MO = COL_MV + ML_WIDTH


def _cparams(sem):
    return pltpu.CompilerParams(dimension_semantics=sem, vmem_limit_bytes=VMEM_LIMIT)


def _sigmoid(x):
    return 1.0 / (1.0 + jnp.exp(-x))


def _softplus(x):
    return jnp.maximum(x, 0.0) + jnp.log1p(jnp.exp(-jnp.abs(x)))


def _dot(a, b):
    return jnp.dot(a, b, preferred_element_type=F32)


def _dot_nt(a, b):
    return lax.dot_general(a, b, (((1,), (1,)), ((), ())), preferred_element_type=F32)


def _dot_tn(a, b):
    return lax.dot_general(a, b, (((0,), (0,)), ((), ())), preferred_element_type=F32)


def _split3(x):
    x1 = x.astype(BF16)
    r1 = x - x1.astype(F32)
    x2 = r1.astype(BF16)
    x3 = (r1 - x2.astype(F32)).astype(BF16)
    return x1, x2, x3


def _split2(x):
    x1 = x.astype(BF16)
    return x1, (x - x1.astype(F32)).astype(BF16)


def _segment_mean_sq(x, bd):
    hi, lo = _split2(x * x)
    return (_dot(hi, bd) + _dot(lo, bd)) * (1.0 / NA_HEAD_DIM)


def _ada_kernel(c_ref, w_ref, b_ref, o_ref):
    c = c_ref[...]
    s = c * _sigmoid(c)
    o_ref[...] = jnp.dot(s, w_ref[...], preferred_element_type=F32,
                         precision=lax.Precision.HIGHEST) + b_ref[...]


def _ada(cond, w, b):
    n, d = cond.shape
    wcols = w.shape[1]
    bn = wcols // 4
    return pl.pallas_call(
        _ada_kernel,
        out_shape=jax.ShapeDtypeStruct((n, wcols), F32),
        grid=(4,),
        in_specs=[pl.BlockSpec((n, d), lambda j: (0, 0)),
                  pl.BlockSpec((d, bn), lambda j: (0, j)),
                  pl.BlockSpec((1, bn), lambda j: (0, j))],
        out_specs=pl.BlockSpec((n, bn), lambda j: (0, j)),
        compiler_params=_cparams(("arbitrary",)),
        name="ada",
    )(cond, w, b.reshape(1, wcols))


def _inproj_kernel(h_ref, sh_ref, sc_ref, g_ref, w_ref, qkg_ref, bd_ref, cos_ref, sin_ref,
                   big_ref, small_ref):
    x = h_ref[...]
    ms = jnp.mean(x * x, axis=-1, keepdims=True)
    u = x * lax.rsqrt(ms + EPS) * g_ref[...]
    u = u * (1.0 + sc_ref[...]) + sh_ref[...]
    ub = u.astype(BF16)
    bd = bd_ref[...]

    for c0 in range(0, 2 * NA_WIDTH, 2 * LANE):
        acc = _dot(ub, w_ref[:, c0:c0 + 2 * LANE])
        msq = _segment_mean_sq(acc, bd)
        y = acc * lax.rsqrt(msq + EPS) * qkg_ref[:, c0:c0 + 2 * LANE]
        big_ref[:, c0:c0 + 2 * LANE] = y.astype(BF16)
    for c0 in range(COL_V, COL_MQ, 2 * LANE):
        big_ref[:, c0:c0 + 2 * LANE] = _dot(ub, w_ref[:, c0:c0 + 2 * LANE]).astype(BF16)
    lane = lax.broadcasted_iota(jnp.int32, (x.shape[0], LANE), 1)
    first_half = (lane % ML_HEAD_DIM) < (ML_HEAD_DIM // 2)
    for base, scale in ((COL_MQ, ML_HEAD_DIM ** -0.5), (COL_MK, 1.0)):
        for j in range(ML_WIDTH // LANE):
            c0 = base + j * LANE
            acc = _dot(ub, w_ref[:, c0:c0 + LANE])
            swapped = jnp.where(first_half,
                                pltpu.roll(acc, LANE - ML_HEAD_DIM // 2, 1),
                                pltpu.roll(acc, ML_HEAD_DIM // 2, 1))
            y = acc * cos_ref[:, j * LANE:(j + 1) * LANE] + swapped * sin_ref[:, j * LANE:(j + 1) * LANE]
            big_ref[:, c0:c0 + LANE] = (y * scale).astype(BF16)
    for c0 in range(COL_MV, BIG_W, 2 * LANE):
        big_ref[:, c0:c0 + 2 * LANE] = _dot(ub, w_ref[:, c0:c0 + 2 * LANE]).astype(BF16)
    small_ref[...] = _dot(ub, w_ref[:, BIG_W:BIG_W + SMALL_W])


def _inproj(h, sh, sc, g, w, qkg, bd, cosf, sinf):
    bsz, s_len, d = h.shape
    tm = TOK_TILE
    mod_spec = pl.BlockSpec((None, None, 1, d), lambda b, i: (b, jnp.minimum(i, 1), 0, 0))
    full = lambda shape: pl.BlockSpec(shape, lambda b, i: (0,) * len(shape))
    return pl.pallas_call(
        _inproj_kernel,
        out_shape=(jax.ShapeDtypeStruct((bsz, s_len, BIG_W), BF16),
                   jax.ShapeDtypeStruct((bsz, s_len, SMALL_W), F32)),
        grid=(bsz, s_len // tm),
        in_specs=[pl.BlockSpec((None, tm, d), lambda b, i: (b, i, 0)),
                  mod_spec, mod_spec, full((1, d)), full(w.shape), full(qkg.shape), full(bd.shape),
                  pl.BlockSpec((tm, ML_WIDTH), lambda b, i: (i, 0)),
                  pl.BlockSpec((tm, ML_WIDTH), lambda b, i: (i, 0))],
        out_specs=(pl.BlockSpec((None, tm, BIG_W), lambda b, i: (b, i, 0)),
                   pl.BlockSpec((None, tm, SMALL_W), lambda b, i: (b, i, 0))),
        compiler_params=_cparams(("arbitrary", "arbitrary")),
        name="inproj",
    )(h, sh, sc, g, w, qkg, bd, cosf, sinf)


def _na_kernel(q_ref, k0, k1, k2, k3, k4, v0, v1, v2, v3, v4, kc_ref, vc_ref, bias_ref, o_ref):
    nq = q_ref.shape[0]
    lane = lax.broadcasted_iota(jnp.int32, (nq, LANE), 1)
    low = lane < NA_HEAD_DIM
    zero = jnp.zeros((), BF16)
    for p in range(NA_WIDTH // LANE):
        cs = slice(p * LANE, (p + 1) * LANE)
        qp = q_ref[:, cs]
        q01 = jnp.concatenate([jnp.where(low, qp, zero), jnp.where(low, zero, qp)], axis=0)
        kw = jnp.concatenate([r[:, cs] for r in (k0, k1, k2, k3, k4)], axis=0)
        vw = jnp.concatenate([r[:, cs] for r in (v0, v1, v2, v3, v4)], axis=0)
        s_w = _dot_nt(q01, kw) + bias_ref[p]
        s_c = _dot_nt(q01, kc_ref[:, cs])
        m = jnp.maximum(jnp.max(s_w, axis=-1, keepdims=True), jnp.max(s_c, axis=-1, keepdims=True))
        p_w = jnp.exp(s_w - m)
        p_c = jnp.exp(s_c - m)
        den = jnp.sum(p_w, axis=-1, keepdims=True) + jnp.sum(p_c, axis=-1, keepdims=True)
        o01 = (_dot(p_w.astype(BF16), vw) + _dot(p_c.astype(BF16), vc_ref[:, cs])) / den
        o_ref[:, cs] = jnp.where(low, o01[:nq], o01[nq:]).astype(o_ref.dtype)


def _attention(big, bias, n_ctx, rows, ctx_out):
    bsz, s_len, _ = big.shape
    nq = NA_ROWS * GRID_W
    ncb = n_ctx // nq
    nrp = rows // NA_ROWS
    first = 0 if ctx_out else ncb
    n_win = NA_KROWS // NA_ROWS

    def lat(i):
        return jnp.maximum(i + first - ncb, 0)

    def kv_spec(j, col):
        return pl.BlockSpec(
            (None, nq, NA_WIDTH),
            lambda b, i: (b, ncb + jnp.clip(lat(i) - 2, 0, nrp - n_win) + j, col))

    def bias_type(b, i):
        step = i + first
        r = lat(i)
        t = jnp.where(r < 2, r, jnp.where(r >= nrp - 2, r - (nrp - 2) + 3, 2))
        return (jnp.where(step < ncb, 5, t), 0, 0, 0)

    return pl.pallas_call(
        _na_kernel,
        out_shape=jax.ShapeDtypeStruct((bsz, s_len, NA_WIDTH), BF16),
        grid=(bsz, ncb + nrp - first),
        in_specs=[pl.BlockSpec((None, nq, NA_WIDTH), lambda b, i: (b, i + first, 0))]
                 + [kv_spec(j, 1) for j in range(n_win)] + [kv_spec(j, 2) for j in range(n_win)]
                 + [pl.BlockSpec((None, n_ctx, NA_WIDTH), lambda b, i: (b, 0, 1)),
                    pl.BlockSpec((None, n_ctx, NA_WIDTH), lambda b, i: (b, 0, 2)),
                    pl.BlockSpec((None, NA_WIDTH // LANE, 2 * nq, NA_KROWS * GRID_W), bias_type)],
        out_specs=pl.BlockSpec((None, nq, NA_WIDTH), lambda b, i: (b, i + first, 0)),
        compiler_params=_cparams(("arbitrary", "arbitrary")),
        name="attention",
    )(big, *([big] * (2 * n_win + 2)), bias)


def _attention_bias(rpb, rows):
    nrp = rows // NA_ROWS
    n_win = NA_KROWS // NA_ROWS
    reps = np.array([0, 1, 2, nrp - 2, nrp - 1])
    qr = (NA_ROWS * reps)[:, None] + np.arange(NA_ROWS)[None, :]
    ws = NA_ROWS * np.clip(reps - 2, 0, nrp - n_win)
    kr = ws[:, None] + np.arange(NA_KROWS)[None, :]
    r0 = np.clip(qr - NA_WIN_H // 2, 0, rows - NA_WIN_H)
    ok_r = (kr[:, None, :] >= r0[:, :, None]) & (kr[:, None, :] < r0[:, :, None] + NA_WIN_H)
    dr = np.clip(kr[:, None, :] - qr[:, :, None] + (NA_WIN_H - 1), 0, 2 * NA_WIN_H - 2)
    qc = np.arange(GRID_W)
    c0 = np.clip(qc - NA_WIN_W // 2, 0, GRID_W - NA_WIN_W)
    kc = np.arange(GRID_W)
    ok_c = (kc[None, :] >= c0[:, None]) & (kc[None, :] < c0[:, None] + NA_WIN_W)
    dc = np.clip(kc[None, :] - qc[:, None] + (NA_WIN_W - 1), 0, 2 * NA_WIN_W - 2)
    sel_r = jnp.asarray(dr[..., None] == np.arange(2 * NA_WIN_H - 1), F32)
    sel_c = jnp.asarray(dc[..., None] == np.arange(2 * NA_WIN_W - 1), F32)
    hp = lax.Precision.HIGHEST
    g = jnp.einsum('hrc,taer->thaec', rpb.astype(F32), sel_r, precision=hp)
    g = jnp.einsum('thaec,qkc->thaqek', g, sel_c, precision=hp)
    ok = ok_r[:, None, :, None, :, None] & ok_c[None, None, None, :, None, :]
    g = jnp.where(ok, g, NEG)
    g = g.reshape(5, NA_HEADS * NA_ROWS * GRID_W, NA_KROWS * GRID_W)
    g = g.reshape(5, NA_WIDTH // LANE, 2 * NA_ROWS * GRID_W, NA_KROWS * GRID_W)
    masked = jnp.full((1,) + g.shape[1:], NEG, F32)
    return jnp.concatenate([g, masked], axis=0)


def _scan_chunk(d, t, ncc, nc):
    bwd = jnp.where(t < ncc, ncc - 1 - t, ncc + (nc - 1 - t))
    return jnp.where(d == 0, t, bwd)


def _precedes(d):
    ii = lax.broadcasted_iota(jnp.int32, (CHUNK, CHUNK), 0)
    jj = lax.broadcasted_iota(jnp.int32, (CHUNK, CHUNK), 1)
    return (jj - ii) * (1 - 2 * d) <= 0


def _cumsum_scan_order(prec, x):
    tri = jnp.where(prec, 1.0, 0.0).astype(BF16)
    x1, x2, x3 = _split3(x)
    return _dot(tri, x1) + _dot(tri, x2) + _dot(tri, x3)


def _ssd_kernel(x_ref, xp_ref, xn_ref, sm_ref, cw_ref, cb_ref, par_ref, y_ref, st_ref, cbuf_ref,
                *, ncc, nc):
    d = pl.program_id(1)
    t = pl.program_id(2)
    chunk = _scan_chunk(d, t, ncc, nc)

    @pl.when(t == 0)
    def _():
        st_ref[...] = jnp.zeros_like(st_ref)

    prev_ok = jnp.where(jnp.logical_and(chunk != 0, chunk != ncc), 1.0, 0.0)
    next_ok = jnp.where(jnp.logical_and(chunk != ncc - 1, chunk != nc - 1), 1.0, 0.0)
    prec = _precedes(d)
    rep = SSM_HEADS // SSM_GROUPS
    nb = x_ref.shape[0]
    for bb in range(nb):
        cbuf_ref[bb, 0:HALO, :] = xp_ref[bb].astype(F32) * prev_ok
        cbuf_ref[bb, HALO:HALO + CHUNK, :] = x_ref[bb].astype(F32)
        cbuf_ref[bb, HALO + CHUNK:, :] = xn_ref[bb].astype(F32) * next_ok
    taps = [[cbuf_ref[bb, HALO - SSM_CONV // 2 + k:HALO - SSM_CONV // 2 + k + CHUNK, :]
             for k in range(SSM_CONV)] for bb in range(nb)]
    gates_in = [sm_ref[bb] for bb in range(nb)]
    states = [[st_ref[bb, h] for h in range(SSM_HEADS)] for bb in range(nb)]
    stores = []
    for bb in range(nb):
        acc = jnp.broadcast_to(cb_ref[...], (CHUNK, SSM_CONV_CH))
        for k in range(SSM_CONV):
            acc = acc + taps[bb][k] * cw_ref[k:k + 1, :]
        xc = acc * _sigmoid(acc)
        xs = xc[:, :SSM_WIDTH]
        bm = xc[:, SSM_WIDTH:SSM_WIDTH + SSM_BC]
        cm = xc[:, SSM_WIDTH + SSM_BC:]

        dt = _softplus(gates_in[bb] + par_ref[0:1, :])
        la = dt * (-jnp.exp(par_ref[1:2, :]))
        acum = _cumsum_scan_order(prec, la)
        acum_t = acum.T
        total = jnp.sum(la, axis=0, keepdims=True)

        for g in range(SSM_GROUPS):
            cg = cm[:, g * SSM_STATE:(g + 1) * SSM_STATE].astype(BF16)
            bg = bm[:, g * SSM_STATE:(g + 1) * SSM_STATE].astype(BF16)
            cb = _dot_nt(cg, bg)
            for hh in range(rep):
                h = g * rep + hh
                a_col = acum[:, h:h + 1]
                a_row = acum_t[h:h + 1, :]
                tot = total[:, h:h + 1]
                x_h = xs[:, h * SSM_HEAD_DIM:(h + 1) * SSM_HEAD_DIM]
                xdt = x_h * dt[:, h:h + 1]
                lmat = jnp.exp(jnp.where(prec, a_col - a_row, -jnp.inf))
                h_in = states[bb][h]
                y = _dot((cb * lmat).astype(BF16), xdt.astype(BF16))
                y = y + _dot(cg, h_in.astype(BF16)) * jnp.exp(a_col)
                y = y + x_h * par_ref[2:3, h:h + 1]
                xw = (xdt * jnp.exp(tot - a_col)).astype(BF16)
                stores.append((bb, h, y, jnp.exp(tot) * h_in + _dot_tn(bg, xw)))

    for bb, h, y, h_new in stores:
        y_ref[bb, :, h * SSM_HEAD_DIM:(h + 1) * SSM_HEAD_DIM] = y
        st_ref[bb, h] = h_new


def _scan_batch(bsz):
    return math.gcd(bsz, SCAN_BATCH)


def _ssd(big, small, conv_w, conv_b, par, n_ctx):
    bsz, s_len, _ = big.shape
    ncc, nc = n_ctx // CHUNK, s_len // CHUNK
    per = CHUNK // HALO
    nb = _scan_batch(bsz)
    ch = functools.partial(_scan_chunk, ncc=ncc, nc=nc)
    cblk = COL_XBC // SSM_CONV_CH
    return pl.pallas_call(
        functools.partial(_ssd_kernel, ncc=ncc, nc=nc),
        out_shape=jax.ShapeDtypeStruct((2, bsz, s_len, SSM_WIDTH), F32),
        grid=(bsz // nb, 2, nc),
        in_specs=[
            pl.BlockSpec((nb, CHUNK, SSM_CONV_CH), lambda b, d, t: (b, ch(d, t), cblk)),
            pl.BlockSpec((nb, HALO, SSM_CONV_CH),
                         lambda b, d, t: (b, jnp.maximum(ch(d, t) * per - 1, 0), cblk)),
            pl.BlockSpec((nb, HALO, SSM_CONV_CH),
                         lambda b, d, t: (b, jnp.minimum((ch(d, t) + 1) * per, nc * per - 1), cblk)),
            pl.BlockSpec((nb, CHUNK, LANE), lambda b, d, t: (b, ch(d, t), d)),
            pl.BlockSpec(conv_w.shape, lambda b, d, t: (0, 0)),
            pl.BlockSpec(conv_b.shape, lambda b, d, t: (0, 0)),
            pl.BlockSpec((None, 8, LANE), lambda b, d, t: (d, 0, 0)),
        ],
        out_specs=pl.BlockSpec((None, nb, CHUNK, SSM_WIDTH), lambda b, d, t: (d, b, ch(d, t), 0)),
        scratch_shapes=[pltpu.VMEM((nb, SSM_HEADS, SSM_STATE, SSM_HEAD_DIM), F32),
                        pltpu.VMEM((nb, CHUNK + 2 * HALO, SSM_CONV_CH), F32)],
        compiler_params=_cparams(("arbitrary", "arbitrary", "arbitrary")),
        name="ssd",
    )(big, big, big, small, conv_w, conv_b, par)


def _mlstm_kernel(q_ref, k_ref, v_ref, sm_ref, par_ref, y_ref, ct_ref, n_ref, m_ref, *, ncc, nc):
    d = pl.program_id(1)
    t = pl.program_id(2)

    @pl.when(t == 0)
    def _():
        ct_ref[...] = jnp.zeros_like(ct_ref)
        n_ref[...] = jnp.zeros_like(n_ref)
        m_ref[...] = jnp.zeros_like(m_ref)

    prec = _precedes(d)
    nb = q_ref.shape[0]
    loaded = [(sm_ref[bb], q_ref[bb], k_ref[bb], v_ref[bb],
               [(m_ref[bb, h][0:1, 0:1], ct_ref[bb, h], n_ref[bb, h][0:1, :]) for h in range(ML_HEADS)])
              for bb in range(nb)]
    stores = []
    for bb in range(nb):
        sm, q_all, k_all, v_all, states = loaded[bb]
        pre = sm + par_ref[0:1, :]
        lf = -_softplus(-pre)
        bcum = _cumsum_scan_order(prec, lf)
        bcum_t = bcum.T
        li_t = pre.T
        gsum = jnp.sum(lf, axis=0, keepdims=True)

        for h in range(ML_HEADS):
            hs = slice(h * ML_HEAD_DIM, (h + 1) * ML_HEAD_DIM)
            ci, cf = ML_HEADS + h, 2 * ML_HEADS + h
            b_col = bcum[:, cf:cf + 1]
            b_row = bcum_t[cf:cf + 1, :]
            gt = gsum[:, cf:cf + 1]
            li_col = pre[:, ci:ci + 1]
            li_row = li_t[ci:ci + 1, :]
            qh = q_all[:, hs]
            kh = k_all[:, hs]
            vh = v_all[:, hs]
            m_in, ct_in, n_in = states[h]

            log_d = jnp.where(prec, b_col - b_row + li_row, -jnp.inf)
            log_inter = b_col + m_in
            m_i = jnp.maximum(jnp.max(log_d, axis=-1, keepdims=True), log_inter)
            s = _dot_nt(qh, kh) * jnp.exp(log_d - m_i)
            inter = jnp.exp(log_inter - m_i)
            num = _dot(s.astype(BF16), vh) + _dot(qh, ct_in.astype(BF16)) * inter
            den = (jnp.sum(s, axis=-1, keepdims=True)
                   + jnp.sum(qh.astype(F32) * n_in, axis=-1, keepdims=True) * inter)
            y = num / jnp.maximum(jnp.abs(den), jnp.exp(-m_i))

            w_end = gt - b_col + li_col
            m_loc = jnp.max(w_end, axis=0, keepdims=True)
            ke = kh.astype(F32) * jnp.exp(w_end - m_loc)
            m_new = jnp.maximum(gt + m_in, m_loc)
            a_old = jnp.exp(gt + m_in - m_new)
            a_new = jnp.exp(m_loc - m_new)
            ct_new = a_old * ct_in + a_new * _dot_tn(ke.astype(BF16), vh)
            n_new = a_old * n_in + a_new * jnp.sum(ke, axis=0, keepdims=True)
            stores.append((bb, h, hs, y, ct_new, n_new, m_new))

    for bb, h, hs, y, ct_new, n_new, m_new in stores:
        y_ref[bb, :, hs] = y
        ct_ref[bb, h] = ct_new
        n_ref[bb, h] = jnp.broadcast_to(n_new, n_ref.shape[2:])
        m_ref[bb, h] = jnp.broadcast_to(m_new, m_ref.shape[2:])


def _mlstm(big, small, par, n_ctx):
    bsz, s_len, _ = big.shape
    ncc, nc = n_ctx // CHUNK, s_len // CHUNK
    nb = 1
    ch = functools.partial(_scan_chunk, ncc=ncc, nc=nc)

    def col_spec(col):
        return pl.BlockSpec((nb, CHUNK, ML_WIDTH), lambda b, d, t: (b, ch(d, t), col // ML_WIDTH))

    return pl.pallas_call(
        functools.partial(_mlstm_kernel, ncc=ncc, nc=nc),
        out_shape=jax.ShapeDtypeStruct((2, bsz, s_len, ML_WIDTH), F32),
        grid=(bsz // nb, 2, nc),
        in_specs=[col_spec(COL_MQ), col_spec(COL_MK), col_spec(COL_MV),
                  pl.BlockSpec((nb, CHUNK, LANE), lambda b, d, t: (b, ch(d, t), d)),
                  pl.BlockSpec((None, 8, LANE), lambda b, d, t: (d, 0, 0))],
        out_specs=pl.BlockSpec((None, nb, CHUNK, ML_WIDTH), lambda b, d, t: (d, b, ch(d, t), 0)),
        scratch_shapes=[pltpu.VMEM((nb, ML_HEADS, ML_HEAD_DIM, ML_HEAD_DIM), F32),
                        pltpu.VMEM((nb, ML_HEADS, 8, ML_HEAD_DIM), F32),
                        pltpu.VMEM((nb, ML_HEADS, 8, LANE), F32)],
        compiler_params=_cparams(("arbitrary", "arbitrary", "arbitrary")),
        name="mlstm",
    )(big, big, big, small, par)


def _merge_kernel(ya_ref, yb_ref, z_ref, yc_ref, o_ref, h_ref, g1_ref, sh_ref, sc_ref, ng_ref,
                  sg_ref, mg_ref, bd_ref, wo_ref, rwh_ref, rwl_ref, rb_ref,
                  hout_ref, v_ref, gates_ref):
    z = z_ref[...].astype(F32)
    yb = (yb_ref[0] + yb_ref[1]) * (z * _sigmoid(z))
    yb = yb * lax.rsqrt(jnp.mean(yb * yb, axis=-1, keepdims=True) + EPS) * sg_ref[...]
    yc = _sigmoid(o_ref[...].astype(F32)) * (yc_ref[0] + yc_ref[1])
    yc = yc * lax.rsqrt(_segment_mean_sq(yc, bd_ref[...]) + EPS) * mg_ref[...]
    y = (_dot(ya_ref[...], wo_ref[0:NA_WIDTH, :])
         + _dot(yb.astype(BF16), wo_ref[NA_WIDTH:NA_WIDTH + SSM_WIDTH, :])
         + _dot(yc.astype(BF16), wo_ref[NA_WIDTH + SSM_WIDTH:, :]))
    hn = h_ref[...] + g1_ref[...] * y
    hout_ref[...] = hn
    v = hn * lax.rsqrt(jnp.mean(hn * hn, axis=-1, keepdims=True) + EPS) * ng_ref[...]
    v = v * (1.0 + sc_ref[...]) + sh_ref[...]
    v_hi, v_lo = _split2(v)
    v_ref[...] = v_hi
    rwh = rwh_ref[...]
    logits = _dot(v_hi, rwh) + _dot(v_lo, rwh) + _dot(v_hi, rwl_ref[...]) + rb_ref[...]

    lane = lax.broadcasted_iota(jnp.int32, logits.shape, 1).astype(F32)
    work = logits
    top = None
    gates = jnp.zeros_like(logits)
    den = jnp.zeros((logits.shape[0], 1), F32)
    for _ in range(TOP_K):
        mx = jnp.max(work, axis=-1, keepdims=True)
        idx = jnp.min(jnp.where(work == mx, lane, float(LANE)), axis=-1, keepdims=True)
        hit = lane == idx
        top = mx if top is None else top
        e = jnp.exp(mx - top)
        gates = gates + jnp.where(hit, e, 0.0)
        den = den + e
        work = jnp.where(hit, -jnp.inf, work)
    gates_ref[...] = gates / den


def _merge(ya, yb, yc, big, h, g1, sh2, sc2, ng, sg, mg, bd, wo, rwh, rwl, rb, n_ctx, ctx_out):
    bsz, s_len, d = h.shape
    tm = TOK_TILE
    first = 0 if ctx_out else n_ctx // tm
    n_out = s_len - first * tm
    mod_spec = pl.BlockSpec((None, None, 1, d), lambda b, i: (b, jnp.minimum(i + first, 1), 0, 0))
    full = lambda shape: pl.BlockSpec(shape, lambda b, i: (0,) * len(shape))
    pair = lambda w: pl.BlockSpec((2, None, tm, w), lambda b, i: (0, b, i + first, 0))
    nl = (s_len - n_ctx) // tm

    def flat_row(b, i):
        if ctx_out:
            return jnp.where(i == 0, bsz * nl + b, b * nl + i - 1)
        return b * nl + i

    flat_tok = lambda w: pl.BlockSpec((tm, w), lambda b, i: (flat_row(b, i), 0))
    return pl.pallas_call(
        _merge_kernel,
        out_shape=(jax.ShapeDtypeStruct((bsz, n_out, d), F32),
                   jax.ShapeDtypeStruct((bsz * n_out, d), BF16),
                   jax.ShapeDtypeStruct((bsz * n_out, LANE), F32)),
        grid=(bsz, n_out // tm),
        in_specs=[pl.BlockSpec((None, tm, NA_WIDTH), lambda b, i: (b, i + first, 0)),
                  pair(SSM_WIDTH),
                  pl.BlockSpec((None, tm, SSM_WIDTH), lambda b, i: (b, i + first, COL_Z // SSM_WIDTH)),
                  pair(ML_WIDTH),
                  pl.BlockSpec((None, tm, ML_WIDTH), lambda b, i: (b, i + first, COL_MO // ML_WIDTH)),
                  pl.BlockSpec((None, tm, d), lambda b, i: (b, i + first, 0)),
                  mod_spec, mod_spec, mod_spec,
                  full((1, d)), full((1, SSM_WIDTH)), full((1, ML_WIDTH)), full(bd.shape),
                  full(wo.shape), full(rwh.shape), full(rwl.shape), full((1, LANE))],
        out_specs=(pl.BlockSpec((None, tm, d), lambda b, i: (b, i, 0)), flat_tok(d), flat_tok(LANE)),
        compiler_params=_cparams(("arbitrary", "arbitrary")),
        name="merge",
    )(ya, yb, big, yc, big, h, g1, sh2, sc2, ng, sg, mg, bd, wo, rwh, rwl, rb)


def _moe_kernel(x_ref, gates_ref, w1_ref, b1_ref, w2_ref, b2_ref, f_ref, rank_ref, gate_ref, rankc_ref,
                *, nblk):
    e = pl.program_id(1)
    tb = x_ref.shape[0] // nblk
    d_ff = w2_ref.shape[0]

    @pl.when(e == 0)
    def _():
        f_ref[...] = jnp.zeros_like(f_ref)
        ii = lax.broadcasted_iota(jnp.int32, (tb, tb), 0)
        jj = lax.broadcasted_iota(jnp.int32, (tb, tb), 1)
        tri = jnp.where(jj <= ii, 1.0, 0.0).astype(BF16)
        for k in range(nblk):
            gates = gates_ref[k * tb:(k + 1) * tb, :]
            picked = gates > 0.0
            incl = _dot(tri, jnp.where(picked, 1.0, 0.0).astype(BF16))
            rank = jnp.where(picked, incl - 1.0, -1.0)
            rankc_ref[k] = rank
            rank_ref[k] = rank.T
            gate_ref[k] = gates.T

    pick_e = jnp.where(lax.broadcasted_iota(jnp.int32, (LANE, MOE_CAP_PAD), 0) == e, 1.0, 0.0).astype(BF16)
    for k in range(nblk):
        rows = slice(k * tb, (k + 1) * tb)
        slot = rank_ref[k, pl.ds(e, 1), :]
        gate = gate_ref[k, pl.ds(e, 1), :]
        r_hi, r_lo = _split2(rankc_ref[k])
        slot_col = _dot(r_hi, pick_e) + _dot(r_lo, pick_e)

        def sub_tile(s, rows=rows, slot=slot, gate=gate, slot_col=slot_col):
            row = (lax.broadcasted_iota(jnp.int32, (MOE_CAP, tb), 0) + s * MOE_CAP).astype(F32)
            hit = slot == row
            onehot = jnp.where(hit, 1.0, 0.0).astype(BF16)
            col = lax.broadcasted_iota(jnp.int32, (tb, MOE_CAP_PAD), 1)
            hit_t = jnp.logical_and(slot_col == (col + s * MOE_CAP).astype(F32), col < MOE_CAP)
            onehot_t = jnp.where(hit_t, 1.0, 0.0).astype(BF16)
            xe = _dot(onehot, x_ref[rows, :]).astype(BF16)
            w = jnp.sum(jnp.where(hit, gate, 0.0), axis=-1, keepdims=True)
            y = jnp.broadcast_to(b2_ref[...], (MOE_CAP, w2_ref.shape[1]))
            for c0 in range(0, d_ff, 2 * LANE):
                x_glu = _dot(xe, w1_ref[:, c0:c0 + 2 * LANE]) + b1_ref[:, c0:c0 + 2 * LANE]
                x_lin = (_dot(xe, w1_ref[:, d_ff + c0:d_ff + c0 + 2 * LANE])
                         + b1_ref[:, d_ff + c0:d_ff + c0 + 2 * LANE])
                x_glu = jnp.minimum(x_glu, SWIGLU_LIMIT)
                x_lin = jnp.clip(x_lin, -SWIGLU_LIMIT, SWIGLU_LIMIT)
                act = x_glu * _sigmoid(SWIGLU_ALPHA * x_glu) * (x_lin + 1.0)
                y = y + _dot(act.astype(BF16), w2_ref[c0:c0 + 2 * LANE, :])
            yw = jnp.concatenate([(y * w).astype(BF16),
                                  jnp.zeros((MOE_CAP_PAD - MOE_CAP, y.shape[1]), BF16)], axis=0)
            f_ref[rows, :] += _dot(onehot_t, yw)

        sub_tile(0)
        count = jnp.max(slot).astype(jnp.int32) + 1

        def overflow(s, carry, sub_tile=sub_tile):
            sub_tile(s)
            return carry

        lax.fori_loop(1, (count + MOE_CAP - 1) // MOE_CAP, overflow, 0)


def _moe(x, gates, w1, b1, w2, b2, layer, first_tok, n_tok):
    d = x.shape[1]
    nblk = MOE_GROUP if n_tok % (MOE_GROUP * MOE_BLOCK) == 0 else 1
    tb = math.gcd(n_tok, MOE_BLOCK)
    rows = nblk * tb
    d_ff = w2.shape[1]
    off = first_tok // rows
    assert first_tok % rows == 0 and n_tok % rows == 0
    expert = lambda i, e: (layer * N_EXPERTS + e, 0, 0)
    return pl.pallas_call(
        functools.partial(_moe_kernel, nblk=nblk),
        out_shape=jax.ShapeDtypeStruct((n_tok, d), F32),
        grid=(n_tok // rows, N_EXPERTS),
        in_specs=[pl.BlockSpec((rows, d), lambda i, e: (i + off, 0)),
                  pl.BlockSpec((rows, LANE), lambda i, e: (i + off, 0)),
                  pl.BlockSpec((None, d, 2 * d_ff), expert),
                  pl.BlockSpec((None, 1, 2 * d_ff), expert),
                  pl.BlockSpec((None, d_ff, d), expert),
                  pl.BlockSpec((None, 1, d), expert)],
        out_specs=pl.BlockSpec((rows, d), lambda i, e: (i, 0)),
        scratch_shapes=[pltpu.VMEM((nblk, LANE, tb), F32), pltpu.VMEM((nblk, LANE, tb), F32),
                        pltpu.VMEM((nblk, tb, LANE), F32)],
        compiler_params=pltpu.CompilerParams(dimension_semantics=("arbitrary", "arbitrary"),
                                             vmem_limit_bytes=MOE_VMEM_LIMIT),
        name="moe",
    )(x, gates, w1, b1, w2, b2)


def _resid_kernel(h_ref, g_ref, fl_ref, *rest, first):
    if first == 0:
        fc_ref, o_ref = rest
        f = jnp.where(pl.program_id(1) == 0, fc_ref[...], fl_ref[...])
    else:
        (o_ref,) = rest
        f = fl_ref[...]
    o_ref[...] = h_ref[...] + g_ref[...] * f


def _resid(h, g2, f_lat, f_ctx):
    bsz, n_tok, d = h.shape
    tm = TOK_TILE
    first = 0 if f_ctx is not None else 1
    nl = n_tok // tm - (1 - first)
    in_specs = [pl.BlockSpec((None, tm, d), lambda b, i: (b, i, 0)),
                pl.BlockSpec((None, None, 1, d), lambda b, i: (b, jnp.minimum(i + first, 1), 0, 0)),
                pl.BlockSpec((tm, d), lambda b, i: (b * nl + jnp.maximum(i + first - 1, 0), 0))]
    args = [h, g2, f_lat]
    if f_ctx is not None:
        in_specs.append(pl.BlockSpec((tm, d), lambda b, i: (b, 0)))
        args.append(f_ctx)
    return pl.pallas_call(
        functools.partial(_resid_kernel, first=first),
        out_shape=jax.ShapeDtypeStruct((bsz, n_tok, d), F32),
        grid=(bsz, n_tok // tm),
        in_specs=in_specs,
        out_specs=pl.BlockSpec((None, tm, d), lambda b, i: (b, i, 0)),
        compiler_params=_cparams(("arbitrary", "arbitrary")),
        name="resid",
    )(*args)


def _in_weight_perm():
    o = IN_OFFSETS
    big = np.concatenate([np.arange(o[0], o[5]), np.arange(o[6], o[10])])
    small = np.full((2, LANE), -1, np.int64)
    for d in range(2):
        for h in range(SSM_HEADS):
            small[d, h] = o[5] + d * SSM_HEADS + h
        for h in range(ML_HEADS):
            small[d, ML_HEADS + h] = o[10] + d * ML_HEADS + h
            small[d, 2 * ML_HEADS + h] = o[11] + d * ML_HEADS + h
    return np.concatenate([big, small.reshape(-1)])


def _rope_tables(n_ctx, n_lat):
    pos = np.arange(n_lat)
    row = (pos // GRID_W).astype(np.float32)
    col = (pos % GRID_W).astype(np.float32)
    n_freq = ML_HEAD_DIM // 4
    inv_freq = jnp.asarray(ROPE_THETA, F32) ** (-jnp.arange(n_freq, dtype=F32) / n_freq)
    ang = jnp.concatenate([jnp.asarray(row)[:, None] * inv_freq, jnp.asarray(col)[:, None] * inv_freq], axis=-1)
    cos, sin = jnp.cos(ang), jnp.sin(ang)
    cosf = jnp.tile(jnp.concatenate([cos, cos], axis=-1), (1, ML_HEADS))
    sinf = jnp.tile(jnp.concatenate([-sin, sin], axis=-1), (1, ML_HEADS))
    cosf = jnp.concatenate([jnp.ones((n_ctx, ML_WIDTH), F32), cosf], axis=0)
    sinf = jnp.concatenate([jnp.zeros((n_ctx, ML_WIDTH), F32), sinf], axis=0)
    return cosf, sinf


def _lane_rows(rows):
    out = jnp.zeros((8, LANE), F32)
    for r, (off, vec) in enumerate(rows):
        out = out.at[r, off:off + vec.shape[0]].set(vec.astype(F32))
    return out


def kernel(x, c, ctx, c_ctx, ada_w, ada_b, norm_g, w_in, na_qk_g, na_rpb, ssm_conv_w, ssm_conv_b,
           ssm_dt_bias, ssm_a_log, ssm_d, ssm_norm_g, ml_gate_b, ml_norm_g, w_out,
           router_w, router_b, exp_w1, exp_b1, exp_w2, exp_b2):
    bsz, n_lat, d = x.shape
    n_ctx = ctx.shape[1]
    rows = n_lat // GRID_W
    depth = w_in.shape[0]
    assert bsz < 8 and n_ctx == TOK_TILE and n_lat % TOK_TILE == 0 and rows >= NA_KROWS

    h = jnp.concatenate([ctx, x], axis=1)
    cond = jnp.zeros((8, d), F32).at[:bsz].set(c).at[bsz].set(c_ctx)
    cosf, sinf = _rope_tables(n_ctx, n_lat)
    perm = _in_weight_perm()
    seg = np.arange(2 * LANE) // NA_HEAD_DIM
    bd = jnp.asarray(seg[:, None] == seg[None, :], BF16)

    n_exp = exp_w1.shape[0] * exp_w1.shape[1]
    w1_all = exp_w1.astype(BF16).reshape((n_exp,) + exp_w1.shape[2:])
    w2_all = exp_w2.astype(BF16).reshape((n_exp,) + exp_w2.shape[2:])
    b1_all = exp_b1.reshape(n_exp, 1, -1)
    b2_all = exp_b2.reshape(n_exp, 1, -1)

    out = None
    for layer in range(depth):
        ctx_out = layer < depth - 1
        mod = _ada(cond, ada_w[layer], ada_b[layer])
        mods = []
        for m in jnp.split(mod, 6, axis=-1):
            lat = m[:bsz]
            cx = jnp.broadcast_to(m[bsz][None], (bsz, d))
            mods.append(jnp.stack([cx, lat], axis=1).reshape(bsz, 2, 1, d))
        sh1, sc1, g1, sh2, sc2, g2 = mods

        w = jnp.where(perm[None, :] >= 0, w_in[layer][:, np.maximum(perm, 0)], 0.0).astype(BF16)
        qkg = jnp.concatenate([jnp.tile(na_qk_g[layer, 0] * NA_HEAD_DIM ** -0.5, NA_HEADS),
                               jnp.tile(na_qk_g[layer, 1], NA_HEADS)]).reshape(1, 2 * NA_WIDTH)
        big, small = _inproj(h, sh1, sc1, norm_g[layer, 0].reshape(1, d), w, qkg, bd, cosf, sinf)

        bias = _attention_bias(na_rpb[layer], rows)
        ya = _attention(big, bias, n_ctx, rows, ctx_out)

        conv_w = jnp.zeros((8, SSM_CONV_CH), F32).at[:SSM_CONV].set(ssm_conv_w[layer])
        ssm_par = jnp.stack([_lane_rows([(0, ssm_dt_bias[layer, dr]), (0, ssm_a_log[layer, dr]),
                                         (0, ssm_d[layer, dr])]) for dr in range(2)])
        yb = _ssd(big, small, conv_w, ssm_conv_b[layer].reshape(1, -1), ssm_par, n_ctx)

        gate_row = [jnp.concatenate([jnp.zeros((ML_HEADS,), F32), ml_gate_b[layer, dr, 0],
                                     ml_gate_b[layer, dr, 1]]) for dr in range(2)]
        ml_par = jnp.stack([_lane_rows([(0, gate_row[dr])]) for dr in range(2)])
        yc = _mlstm(big, small, ml_par, n_ctx)

        rw = jnp.zeros((d, LANE), F32).at[:, :N_EXPERTS].set(router_w[layer])
        rwh = rw.astype(BF16)
        rwl = (rw - rwh.astype(F32)).astype(BF16)
        rb = jnp.full((1, LANE), NEG, F32).at[0, :N_EXPERTS].set(router_b[layer])
        hn, v, gates = _merge(ya, yb, yc, big, h, g1, sh2, sc2, norm_g[layer, 1].reshape(1, d),
                              ssm_norm_g[layer].reshape(1, -1), ml_norm_g[layer].reshape(1, -1), bd,
                              w_out[layer].astype(BF16), rwh, rwl, rb, n_ctx, ctx_out)

        n_lat_tok = bsz * n_lat
        f_lat = _moe(v, gates, w1_all, b1_all, w2_all, b2_all, layer, 0, n_lat_tok)
        f_ctx = (_moe(v, gates, w1_all, b1_all, w2_all, b2_all, layer, n_lat_tok, bsz * n_ctx)
                 if ctx_out else None)
        h = _resid(hn, g2, f_lat, f_ctx)
        out = h
    return out
```

```python
import functools
import math

import numpy as np
import jax
import jax.numpy as jnp
from jax import lax
from jax.experimental import pallas as pl
from jax.experimental.pallas import tpu as pltpu

F32 = jnp.float32
BF16 = jnp.bfloat16

GRID_W = 64
EPS = 1e-6
NA_HEADS = 8
NA_HEAD_DIM = 64
NA_WIN_H = 8
NA_WIN_W = 16
SSM_HEADS = 4
SSM_HEAD_DIM = 64
SSM_GROUPS = 2
SSM_STATE = 128
SSM_CONV = 5
ML_HEADS = 4
ML_HEAD_DIM = 64
ROPE_THETA = 10000.0
N_EXPERTS = 32
TOP_K = 4
SWIGLU_ALPHA = 1.702
SWIGLU_LIMIT = 7.0

NA_WIDTH = NA_HEADS * NA_HEAD_DIM
SSM_WIDTH = SSM_HEADS * SSM_HEAD_DIM
SSM_BC = SSM_GROUPS * SSM_STATE
SSM_CONV_CH = SSM_WIDTH + 2 * SSM_BC
ML_WIDTH = ML_HEADS * ML_HEAD_DIM
IN_SPLITS = (NA_WIDTH, NA_WIDTH, NA_WIDTH, SSM_CONV_CH, SSM_WIDTH, 2 * SSM_HEADS,
             ML_WIDTH, ML_WIDTH, ML_WIDTH, ML_WIDTH, 2 * ML_HEADS, 2 * ML_HEADS)
IN_OFFSETS = tuple(int(o) for o in np.cumsum((0,) + IN_SPLITS))

LANE = 128
CHUNK = 128
HALO = 16
TOK_TILE = 256
NA_ROWS = 2
NA_KROWS = NA_ROWS + NA_WIN_H
BIG_W = 3 * NA_WIDTH + SSM_CONV_CH + SSM_WIDTH + 4 * ML_WIDTH
SMALL_W = 2 * LANE
MOE_BLOCK = 1024
MOE_GROUP = 2
MOE_CAP = 160
SCAN_BATCH = 4
NEG = -1e30
VMEM_LIMIT = 48 * 1024 * 1024
MOE_VMEM_LIMIT = 56 * 1024 * 1024

COL_Q, COL_K, COL_V = 0, NA_WIDTH, 2 * NA_WIDTH
COL_XBC = 3 * NA_WIDTH
COL_Z = COL_XBC + SSM_CONV_CH
COL_MQ = COL_Z + SSM_WIDTH
COL_MK = COL_MQ + ML_WIDTH
COL_MV = COL_MK + ML_WIDTH
COL_MO = COL_MV + ML_WIDTH


def _cparams(sem):
    return pltpu.CompilerParams(dimension_semantics=sem, vmem_limit_bytes=VMEM_LIMIT)


def _sigmoid(x):
    return 1.0 / (1.0 + jnp.exp(-x))


def _softplus(x):
    return jnp.maximum(x, 0.0) + jnp.log1p(jnp.exp(-jnp.abs(x)))


def _dot(a, b):
    return jnp.dot(a, b, preferred_element_type=F32)


def _dot_nt(a, b):
    return lax.dot_general(a, b, (((1,), (1,)), ((), ())), preferred_element_type=F32)


def _dot_tn(a, b):
    return lax.dot_general(a, b, (((0,), (0,)), ((), ())), preferred_element_type=F32)


def _split3(x):
    x1 = x.astype(BF16)
    r1 = x - x1.astype(F32)
    x2 = r1.astype(BF16)
    x3 = (r1 - x2.astype(F32)).astype(BF16)
    return x1, x2, x3


def _split2(x):
    x1 = x.astype(BF16)
    return x1, (x - x1.astype(F32)).astype(BF16)


def _segment_mean_sq(x, bd):
    hi, lo = _split2(x * x)
    return (_dot(hi, bd) + _dot(lo, bd)) * (1.0 / NA_HEAD_DIM)


def _ada_kernel(c_ref, w_ref, b_ref, o_ref):
    c = c_ref[...]
    s = c * _sigmoid(c)
    o_ref[...] = jnp.dot(s, w_ref[...], preferred_element_type=F32,
                         precision=lax.Precision.HIGHEST) + b_ref[...]


def _ada(cond, w, b):
    n, d = cond.shape
    wcols = w.shape[1]
    bn = wcols // 4
    return pl.pallas_call(
        _ada_kernel,
        out_shape=jax.ShapeDtypeStruct((n, wcols), F32),
        grid=(4,),
        in_specs=[pl.BlockSpec((n, d), lambda j: (0, 0)),
                  pl.BlockSpec((d, bn), lambda j: (0, j)),
                  pl.BlockSpec((1, bn), lambda j: (0, j))],
        out_specs=pl.BlockSpec((n, bn), lambda j: (0, j)),
        compiler_params=_cparams(("arbitrary",)),
        name="ada",
    )(cond, w, b.reshape(1, wcols))


def _inproj_kernel(h_ref, sh_ref, sc_ref, g_ref, w_ref, qkg_ref, bd_ref, cos_ref, sin_ref,
                   big_ref, small_ref):
    x = h_ref[...]
    ms = jnp.mean(x * x, axis=-1, keepdims=True)
    u = x * lax.rsqrt(ms + EPS) * g_ref[...]
    u = u * (1.0 + sc_ref[...]) + sh_ref[...]
    ub = u.astype(BF16)
    bd = bd_ref[...]

    for c0 in range(0, 2 * NA_WIDTH, 2 * LANE):
        acc = _dot(ub, w_ref[:, c0:c0 + 2 * LANE])
        msq = _segment_mean_sq(acc, bd)
        y = acc * lax.rsqrt(msq + EPS) * qkg_ref[:, c0:c0 + 2 * LANE]
        big_ref[:, c0:c0 + 2 * LANE] = y.astype(BF16)
    for c0 in range(COL_V, COL_MQ, 2 * LANE):
        big_ref[:, c0:c0 + 2 * LANE] = _dot(ub, w_ref[:, c0:c0 + 2 * LANE]).astype(BF16)
    lane = lax.broadcasted_iota(jnp.int32, (x.shape[0], LANE), 1)
    first_half = (lane % ML_HEAD_DIM) < (ML_HEAD_DIM // 2)
    for base, scale in ((COL_MQ, ML_HEAD_DIM ** -0.5), (COL_MK, 1.0)):
        for j in range(ML_WIDTH // LANE):
            c0 = base + j * LANE
            acc = _dot(ub, w_ref[:, c0:c0 + LANE])
            swapped = jnp.where(first_half,
                                pltpu.roll(acc, LANE - ML_HEAD_DIM // 2, 1),
                                pltpu.roll(acc, ML_HEAD_DIM // 2, 1))
            y = acc * cos_ref[:, j * LANE:(j + 1) * LANE] + swapped * sin_ref[:, j * LANE:(j + 1) * LANE]
            big_ref[:, c0:c0 + LANE] = (y * scale).astype(BF16)
    for c0 in range(COL_MV, BIG_W, 2 * LANE):
        big_ref[:, c0:c0 + 2 * LANE] = _dot(ub, w_ref[:, c0:c0 + 2 * LANE]).astype(BF16)
    small_ref[...] = _dot(ub, w_ref[:, BIG_W:BIG_W + SMALL_W])


def _inproj(h, sh, sc, g, w, qkg, bd, cosf, sinf):
    bsz, s_len, d = h.shape
    tm = TOK_TILE
    mod_spec = pl.BlockSpec((None, None, 1, d), lambda b, i: (b, jnp.minimum(i, 1), 0, 0))
    full = lambda shape: pl.BlockSpec(shape, lambda b, i: (0,) * len(shape))
    return pl.pallas_call(
        _inproj_kernel,
        out_shape=(jax.ShapeDtypeStruct((bsz, s_len, BIG_W), BF16),
                   jax.ShapeDtypeStruct((bsz, s_len, SMALL_W), F32)),
        grid=(bsz, s_len // tm),
        in_specs=[pl.BlockSpec((None, tm, d), lambda b, i: (b, i, 0)),
                  mod_spec, mod_spec, full((1, d)), full(w.shape), full(qkg.shape), full(bd.shape),
                  pl.BlockSpec((tm, ML_WIDTH), lambda b, i: (i, 0)),
                  pl.BlockSpec((tm, ML_WIDTH), lambda b, i: (i, 0))],
        out_specs=(pl.BlockSpec((None, tm, BIG_W), lambda b, i: (b, i, 0)),
                   pl.BlockSpec((None, tm, SMALL_W), lambda b, i: (b, i, 0))),
        compiler_params=_cparams(("arbitrary", "arbitrary")),
        name="inproj",
    )(h, sh, sc, g, w, qkg, bd, cosf, sinf)


def _na_kernel(q_ref, k0, k1, k2, k3, k4, v0, v1, v2, v3, v4, kc_ref, vc_ref, bias_ref, o_ref):
    nq = q_ref.shape[0]
    lane = lax.broadcasted_iota(jnp.int32, (nq, LANE), 1)
    low = lane < NA_HEAD_DIM
    zero = jnp.zeros((), BF16)
    for p in range(NA_WIDTH // LANE):
        cs = slice(p * LANE, (p + 1) * LANE)
        qp = q_ref[:, cs]
        q01 = jnp.concatenate([jnp.where(low, qp, zero), jnp.where(low, zero, qp)], axis=0)
        kw = jnp.concatenate([r[:, cs] for r in (k0, k1, k2, k3, k4)], axis=0)
        vw = jnp.concatenate([r[:, cs] for r in (v0, v1, v2, v3, v4)], axis=0)
        s_w = _dot_nt(q01, kw) + bias_ref[p]
        s_c = _dot_nt(q01, kc_ref[:, cs])
        m = jnp.maximum(jnp.max(s_w, axis=-1, keepdims=True), jnp.max(s_c, axis=-1, keepdims=True))
        p_w = jnp.exp(s_w - m)
        p_c = jnp.exp(s_c - m)
        den = jnp.sum(p_w, axis=-1, keepdims=True) + jnp.sum(p_c, axis=-1, keepdims=True)
        o01 = (_dot(p_w.astype(BF16), vw) + _dot(p_c.astype(BF16), vc_ref[:, cs])) / den
        o_ref[:, cs] = jnp.where(low, o01[:nq], o01[nq:]).astype(o_ref.dtype)


def _attention(big, bias, n_ctx, rows, ctx_out):
    bsz, s_len, _ = big.shape
    nq = NA_ROWS * GRID_W
    ncb = n_ctx // nq
    nrp = rows // NA_ROWS
    first = 0 if ctx_out else ncb
    n_win = NA_KROWS // NA_ROWS

    def lat(i):
        return jnp.maximum(i + first - ncb, 0)

    def kv_spec(j, col):
        return pl.BlockSpec(
            (None, nq, NA_WIDTH),
            lambda b, i: (b, ncb + jnp.clip(lat(i) - 2, 0, nrp - n_win) + j, col))

    def bias_type(b, i):
        step = i + first
        r = lat(i)
        t = jnp.where(r < 2, r, jnp.where(r >= nrp - 2, r - (nrp - 2) + 3, 2))
        return (jnp.where(step < ncb, 5, t), 0, 0, 0)

    return pl.pallas_call(
        _na_kernel,
        out_shape=jax.ShapeDtypeStruct((bsz, s_len, NA_WIDTH), BF16),
        grid=(bsz, ncb + nrp - first),
        in_specs=[pl.BlockSpec((None, nq, NA_WIDTH), lambda b, i: (b, i + first, 0))]
                 + [kv_spec(j, 1) for j in range(n_win)] + [kv_spec(j, 2) for j in range(n_win)]
                 + [pl.BlockSpec((None, n_ctx, NA_WIDTH), lambda b, i: (b, 0, 1)),
                    pl.BlockSpec((None, n_ctx, NA_WIDTH), lambda b, i: (b, 0, 2)),
                    pl.BlockSpec((None, NA_WIDTH // LANE, 2 * nq, NA_KROWS * GRID_W), bias_type)],
        out_specs=pl.BlockSpec((None, nq, NA_WIDTH), lambda b, i: (b, i + first, 0)),
        compiler_params=_cparams(("arbitrary", "arbitrary")),
        name="attention",
    )(big, *([big] * (2 * n_win + 2)), bias)


def _attention_bias(rpb, rows):
    nrp = rows // NA_ROWS
    n_win = NA_KROWS // NA_ROWS
    reps = np.array([0, 1, 2, nrp - 2, nrp - 1])
    qr = (NA_ROWS * reps)[:, None] + np.arange(NA_ROWS)[None, :]
    ws = NA_ROWS * np.clip(reps - 2, 0, nrp - n_win)
    kr = ws[:, None] + np.arange(NA_KROWS)[None, :]
    r0 = np.clip(qr - NA_WIN_H // 2, 0, rows - NA_WIN_H)
    ok_r = (kr[:, None, :] >= r0[:, :, None]) & (kr[:, None, :] < r0[:, :, None] + NA_WIN_H)
    dr = np.clip(kr[:, None, :] - qr[:, :, None] + (NA_WIN_H - 1), 0, 2 * NA_WIN_H - 2)
    qc = np.arange(GRID_W)
    c0 = np.clip(qc - NA_WIN_W // 2, 0, GRID_W - NA_WIN_W)
    kc = np.arange(GRID_W)
    ok_c = (kc[None, :] >= c0[:, None]) & (kc[None, :] < c0[:, None] + NA_WIN_W)
    dc = np.clip(kc[None, :] - qc[:, None] + (NA_WIN_W - 1), 0, 2 * NA_WIN_W - 2)
    sel_r = jnp.asarray(dr[..., None] == np.arange(2 * NA_WIN_H - 1), F32)
    sel_c = jnp.asarray(dc[..., None] == np.arange(2 * NA_WIN_W - 1), F32)
    hp = lax.Precision.HIGHEST
    g = jnp.einsum('hrc,taer->thaec', rpb.astype(F32), sel_r, precision=hp)
    g = jnp.einsum('thaec,qkc->thaqek', g, sel_c, precision=hp)
    ok = ok_r[:, None, :, None, :, None] & ok_c[None, None, None, :, None, :]
    g = jnp.where(ok, g, NEG)
    g = g.reshape(5, NA_HEADS * NA_ROWS * GRID_W, NA_KROWS * GRID_W)
    g = g.reshape(5, NA_WIDTH // LANE, 2 * NA_ROWS * GRID_W, NA_KROWS * GRID_W)
    masked = jnp.full((1,) + g.shape[1:], NEG, F32)
    return jnp.concatenate([g, masked], axis=0)


def _scan_chunk(d, t, ncc, nc):
    bwd = jnp.where(t < ncc, ncc - 1 - t, ncc + (nc - 1 - t))
    return jnp.where(d == 0, t, bwd)


def _precedes(d):
    ii = lax.broadcasted_iota(jnp.int32, (CHUNK, CHUNK), 0)
    jj = lax.broadcasted_iota(jnp.int32, (CHUNK, CHUNK), 1)
    return (jj - ii) * (1 - 2 * d) <= 0


def _cumsum_scan_order(prec, x):
    tri = jnp.where(prec, 1.0, 0.0).astype(BF16)
    x1, x2, x3 = _split3(x)
    return _dot(tri, x1) + _dot(tri, x2) + _dot(tri, x3)


def _ssd_kernel(x_ref, xp_ref, xn_ref, sm_ref, cw_ref, cb_ref, par_ref, y_ref, st_ref, cbuf_ref,
                *, ncc, nc):
    d = pl.program_id(1)
    t = pl.program_id(2)
    chunk = _scan_chunk(d, t, ncc, nc)

    @pl.when(t == 0)
    def _():
        st_ref[...] = jnp.zeros_like(st_ref)

    prev_ok = jnp.where(jnp.logical_and(chunk != 0, chunk != ncc), 1.0, 0.0)
    next_ok = jnp.where(jnp.logical_and(chunk != ncc - 1, chunk != nc - 1), 1.0, 0.0)
    prec = _precedes(d)
    rep = SSM_HEADS // SSM_GROUPS
    nb = x_ref.shape[0]
    for bb in range(nb):
        cbuf_ref[bb, 0:HALO, :] = xp_ref[bb].astype(F32) * prev_ok
        cbuf_ref[bb, HALO:HALO + CHUNK, :] = x_ref[bb].astype(F32)
        cbuf_ref[bb, HALO + CHUNK:, :] = xn_ref[bb].astype(F32) * next_ok
    taps = [[cbuf_ref[bb, HALO - SSM_CONV // 2 + k:HALO - SSM_CONV // 2 + k + CHUNK, :]
             for k in range(SSM_CONV)] for bb in range(nb)]
    gates_in = [sm_ref[bb] for bb in range(nb)]
    states = [[st_ref[bb, h] for h in range(SSM_HEADS)] for bb in range(nb)]
    stores = []
    for bb in range(nb):
        acc = jnp.broadcast_to(cb_ref[...], (CHUNK, SSM_CONV_CH))
        for k in range(SSM_CONV):
            acc = acc + taps[bb][k] * cw_ref[k:k + 1, :]
        xc = acc * _sigmoid(acc)
        xs = xc[:, :SSM_WIDTH]
        bm = xc[:, SSM_WIDTH:SSM_WIDTH + SSM_BC]
        cm = xc[:, SSM_WIDTH + SSM_BC:]

        dt = _softplus(gates_in[bb] + par_ref[0:1, :])
        la = dt * (-jnp.exp(par_ref[1:2, :]))
        acum = _cumsum_scan_order(prec, la)
        acum_t = acum.T
        total = jnp.sum(la, axis=0, keepdims=True)

        for g in range(SSM_GROUPS):
            cg = cm[:, g * SSM_STATE:(g + 1) * SSM_STATE].astype(BF16)
            bg = bm[:, g * SSM_STATE:(g + 1) * SSM_STATE].astype(BF16)
            cb = _dot_nt(cg, bg)
            for hh in range(rep):
                h = g * rep + hh
                a_col = acum[:, h:h + 1]
                a_row = acum_t[h:h + 1, :]
                tot = total[:, h:h + 1]
                x_h = xs[:, h * SSM_HEAD_DIM:(h + 1) * SSM_HEAD_DIM]
                xdt = x_h * dt[:, h:h + 1]
                lmat = jnp.exp(jnp.where(prec, a_col - a_row, -jnp.inf))
                h_in = states[bb][h]
                y = _dot((cb * lmat).astype(BF16), xdt.astype(BF16))
                y = y + _dot(cg, h_in.astype(BF16)) * jnp.exp(a_col)
                y = y + x_h * par_ref[2:3, h:h + 1]
                xw = (xdt * jnp.exp(tot - a_col)).astype(BF16)
                stores.append((bb, h, y, jnp.exp(tot) * h_in + _dot_tn(bg, xw)))

    for bb, h, y, h_new in stores:
        y_ref[bb, :, h * SSM_HEAD_DIM:(h + 1) * SSM_HEAD_DIM] = y
        st_ref[bb, h] = h_new


def _scan_batch(bsz):
    return math.gcd(bsz, SCAN_BATCH)


def _ssd(big, small, conv_w, conv_b, par, n_ctx):
    bsz, s_len, _ = big.shape
    ncc, nc = n_ctx // CHUNK, s_len // CHUNK
    per = CHUNK // HALO
    nb = _scan_batch(bsz)
    ch = functools.partial(_scan_chunk, ncc=ncc, nc=nc)
    cblk = COL_XBC // SSM_CONV_CH
    return pl.pallas_call(
        functools.partial(_ssd_kernel, ncc=ncc, nc=nc),
        out_shape=jax.ShapeDtypeStruct((2, bsz, s_len, SSM_WIDTH), F32),
        grid=(bsz // nb, 2, nc),
        in_specs=[
            pl.BlockSpec((nb, CHUNK, SSM_CONV_CH), lambda b, d, t: (b, ch(d, t), cblk)),
            pl.BlockSpec((nb, HALO, SSM_CONV_CH),
                         lambda b, d, t: (b, jnp.maximum(ch(d, t) * per - 1, 0), cblk)),
            pl.BlockSpec((nb, HALO, SSM_CONV_CH),
                         lambda b, d, t: (b, jnp.minimum((ch(d, t) + 1) * per, nc * per - 1), cblk)),
            pl.BlockSpec((nb, CHUNK, LANE), lambda b, d, t: (b, ch(d, t), d)),
            pl.BlockSpec(conv_w.shape, lambda b, d, t: (0, 0)),
            pl.BlockSpec(conv_b.shape, lambda b, d, t: (0, 0)),
            pl.BlockSpec((None, 8, LANE), lambda b, d, t: (d, 0, 0)),
        ],
        out_specs=pl.BlockSpec((None, nb, CHUNK, SSM_WIDTH), lambda b, d, t: (d, b, ch(d, t), 0)),
        scratch_shapes=[pltpu.VMEM((nb, SSM_HEADS, SSM_STATE, SSM_HEAD_DIM), F32),
                        pltpu.VMEM((nb, CHUNK + 2 * HALO, SSM_CONV_CH), F32)],
        compiler_params=_cparams(("arbitrary", "arbitrary", "arbitrary")),
        name="ssd",
    )(big, big, big, small, conv_w, conv_b, par)


def _mlstm_kernel(q_ref, k_ref, v_ref, sm_ref, par_ref, y_ref, ct_ref, n_ref, m_ref, *, ncc, nc):
    d = pl.program_id(1)
    t = pl.program_id(2)

    @pl.when(t == 0)
    def _():
        ct_ref[...] = jnp.zeros_like(ct_ref)
        n_ref[...] = jnp.zeros_like(n_ref)
        m_ref[...] = jnp.zeros_like(m_ref)

    prec = _precedes(d)
    nb = q_ref.shape[0]
    loaded = [(sm_ref[bb], q_ref[bb], k_ref[bb], v_ref[bb],
               [(m_ref[bb, h][0:1, 0:1], ct_ref[bb, h], n_ref[bb, h][0:1, :]) for h in range(ML_HEADS)])
              for bb in range(nb)]
    stores = []
    for bb in range(nb):
        sm, q_all, k_all, v_all, states = loaded[bb]
        pre = sm + par_ref[0:1, :]
        lf = -_softplus(-pre)
        bcum = _cumsum_scan_order(prec, lf)
        bcum_t = bcum.T
        li_t = pre.T
        gsum = jnp.sum(lf, axis=0, keepdims=True)

        for h in range(ML_HEADS):
            hs = slice(h * ML_HEAD_DIM, (h + 1) * ML_HEAD_DIM)
            ci, cf = ML_HEADS + h, 2 * ML_HEADS + h
            b_col = bcum[:, cf:cf + 1]
            b_row = bcum_t[cf:cf + 1, :]
            gt = gsum[:, cf:cf + 1]
            li_col = pre[:, ci:ci + 1]
            li_row = li_t[ci:ci + 1, :]
            qh = q_all[:, hs]
            kh = k_all[:, hs]
            vh = v_all[:, hs]
            m_in, ct_in, n_in = states[h]

            log_d = jnp.where(prec, b_col - b_row + li_row, -jnp.inf)
            log_inter = b_col + m_in
            m_i = jnp.maximum(jnp.max(log_d, axis=-1, keepdims=True), log_inter)
            s = _dot_nt(qh, kh) * jnp.exp(log_d - m_i)
            inter = jnp.exp(log_inter - m_i)
            num = _dot(s.astype(BF16), vh) + _dot(qh, ct_in.astype(BF16)) * inter
            den = (jnp.sum(s, axis=-1, keepdims=True)
                   + jnp.sum(qh.astype(F32) * n_in, axis=-1, keepdims=True) * inter)
            y = num / jnp.maximum(jnp.abs(den), jnp.exp(-m_i))

            w_end = gt - b_col + li_col
            m_loc = jnp.max(w_end, axis=0, keepdims=True)
            ke = kh.astype(F32) * jnp.exp(w_end - m_loc)
            m_new = jnp.maximum(gt + m_in, m_loc)
            a_old = jnp.exp(gt + m_in - m_new)
            a_new = jnp.exp(m_loc - m_new)
            ct_new = a_old * ct_in + a_new * _dot_tn(ke.astype(BF16), vh)
            n_new = a_old * n_in + a_new * jnp.sum(ke, axis=0, keepdims=True)
            stores.append((bb, h, hs, y, ct_new, n_new, m_new))

    for bb, h, hs, y, ct_new, n_new, m_new in stores:
        y_ref[bb, :, hs] = y
        ct_ref[bb, h] = ct_new
        n_ref[bb, h] = jnp.broadcast_to(n_new, n_ref.shape[2:])
        m_ref[bb, h] = jnp.broadcast_to(m_new, m_ref.shape[2:])


def _mlstm(big, small, par, n_ctx):
    bsz, s_len, _ = big.shape
    ncc, nc = n_ctx // CHUNK, s_len // CHUNK
    nb = 1
    ch = functools.partial(_scan_chunk, ncc=ncc, nc=nc)

    def col_spec(col):
        return pl.BlockSpec((nb, CHUNK, ML_WIDTH), lambda b, d, t: (b, ch(d, t), col // ML_WIDTH))

    return pl.pallas_call(
        functools.partial(_mlstm_kernel, ncc=ncc, nc=nc),
        out_shape=jax.ShapeDtypeStruct((2, bsz, s_len, ML_WIDTH), F32),
        grid=(bsz // nb, 2, nc),
        in_specs=[col_spec(COL_MQ), col_spec(COL_MK), col_spec(COL_MV),
                  pl.BlockSpec((nb, CHUNK, LANE), lambda b, d, t: (b, ch(d, t), d)),
                  pl.BlockSpec((None, 8, LANE), lambda b, d, t: (d, 0, 0))],
        out_specs=pl.BlockSpec((None, nb, CHUNK, ML_WIDTH), lambda b, d, t: (d, b, ch(d, t), 0)),
        scratch_shapes=[pltpu.VMEM((nb, ML_HEADS, ML_HEAD_DIM, ML_HEAD_DIM), F32),
                        pltpu.VMEM((nb, ML_HEADS, 8, ML_HEAD_DIM), F32),
                        pltpu.VMEM((nb, ML_HEADS, 8, LANE), F32)],
        compiler_params=_cparams(("arbitrary", "arbitrary", "arbitrary")),
        name="mlstm",
    )(big, big, big, small, par)


def _merge_kernel(ya_ref, yb_ref, z_ref, yc_ref, o_ref, h_ref, g1_ref, sh_ref, sc_ref, ng_ref,
                  sg_ref, mg_ref, bd_ref, wo_ref, rwh_ref, rwl_ref, rb_ref,
                  hout_ref, v_ref, gates_ref):
    z = z_ref[...].astype(F32)
    yb = (yb_ref[0] + yb_ref[1]) * (z * _sigmoid(z))
    yb = yb * lax.rsqrt(jnp.mean(yb * yb, axis=-1, keepdims=True) + EPS) * sg_ref[...]
    yc = _sigmoid(o_ref[...].astype(F32)) * (yc_ref[0] + yc_ref[1])
    yc = yc * lax.rsqrt(_segment_mean_sq(yc, bd_ref[...]) + EPS) * mg_ref[...]
    y = (_dot(ya_ref[...], wo_ref[0:NA_WIDTH, :])
         + _dot(yb.astype(BF16), wo_ref[NA_WIDTH:NA_WIDTH + SSM_WIDTH, :])
         + _dot(yc.astype(BF16), wo_ref[NA_WIDTH + SSM_WIDTH:, :]))
    hn = h_ref[...] + g1_ref[...] * y
    hout_ref[...] = hn
    v = hn * lax.rsqrt(jnp.mean(hn * hn, axis=-1, keepdims=True) + EPS) * ng_ref[...]
    v = v * (1.0 + sc_ref[...]) + sh_ref[...]
    v_hi, v_lo = _split2(v)
    v_ref[...] = v_hi
    rwh = rwh_ref[...]
    logits = _dot(v_hi, rwh) + _dot(v_lo, rwh) + _dot(v_hi, rwl_ref[...]) + rb_ref[...]

    lane = lax.broadcasted_iota(jnp.int32, logits.shape, 1).astype(F32)
    work = logits
    top = None
    gates = jnp.zeros_like(logits)
    den = jnp.zeros((logits.shape[0], 1), F32)
    for _ in range(TOP_K):
        mx = jnp.max(work, axis=-1, keepdims=True)
        idx = jnp.min(jnp.where(work == mx, lane, float(LANE)), axis=-1, keepdims=True)
        hit = lane == idx
        top = mx if top is None else top
        e = jnp.exp(mx - top)
        gates = gates + jnp.where(hit, e, 0.0)
        den = den + e
        work = jnp.where(hit, -jnp.inf, work)
    gates_ref[...] = gates / den


def _merge(ya, yb, yc, big, h, g1, sh2, sc2, ng, sg, mg, bd, wo, rwh, rwl, rb, n_ctx, ctx_out):
    bsz, s_len, d = h.shape
    tm = TOK_TILE
    first = 0 if ctx_out else n_ctx // tm
    n_out = s_len - first * tm
    mod_spec = pl.BlockSpec((None, None, 1, d), lambda b, i: (b, jnp.minimum(i + first, 1), 0, 0))
    full = lambda shape: pl.BlockSpec(shape, lambda b, i: (0,) * len(shape))
    pair = lambda w: pl.BlockSpec((2, None, tm, w), lambda b, i: (0, b, i + first, 0))
    nl = (s_len - n_ctx) // tm

    def flat_row(b, i):
        if ctx_out:
            return jnp.where(i == 0, bsz * nl + b, (i - 1) * bsz + b)
        return i * bsz + b

    flat_tok = lambda w: pl.BlockSpec((tm, w), lambda b, i: (flat_row(b, i), 0))
    return pl.pallas_call(
        _merge_kernel,
        out_shape=(jax.ShapeDtypeStruct((bsz, n_out, d), F32),
                   jax.ShapeDtypeStruct((bsz * n_out, d), BF16),
                   jax.ShapeDtypeStruct((bsz * n_out, LANE), F32)),
        grid=(bsz, n_out // tm),
        in_specs=[pl.BlockSpec((None, tm, NA_WIDTH), lambda b, i: (b, i + first, 0)),
                  pair(SSM_WIDTH),
                  pl.BlockSpec((None, tm, SSM_WIDTH), lambda b, i: (b, i + first, COL_Z // SSM_WIDTH)),
                  pair(ML_WIDTH),
                  pl.BlockSpec((None, tm, ML_WIDTH), lambda b, i: (b, i + first, COL_MO // ML_WIDTH)),
                  pl.BlockSpec((None, tm, d), lambda b, i: (b, i + first, 0)),
                  mod_spec, mod_spec, mod_spec,
                  full((1, d)), full((1, SSM_WIDTH)), full((1, ML_WIDTH)), full(bd.shape),
                  full(wo.shape), full(rwh.shape), full(rwl.shape), full((1, LANE))],
        out_specs=(pl.BlockSpec((None, tm, d), lambda b, i: (b, i, 0)), flat_tok(d), flat_tok(LANE)),
        compiler_params=_cparams(("arbitrary", "arbitrary")),
        name="merge",
    )(ya, yb, big, yc, big, h, g1, sh2, sc2, ng, sg, mg, bd, wo, rwh, rwl, rb)


def _moe_kernel(x_ref, gates_ref, w1_ref, b1_ref, w2_ref, b2_ref, f_ref, rank_ref, gate_ref, *, nblk):
    e = pl.program_id(1)
    tb = x_ref.shape[0] // nblk
    d_ff = w2_ref.shape[0]

    @pl.when(e == 0)
    def _():
        f_ref[...] = jnp.zeros_like(f_ref)
        ii = lax.broadcasted_iota(jnp.int32, (tb, tb), 0)
        jj = lax.broadcasted_iota(jnp.int32, (tb, tb), 1)
        tri = jnp.where(jj <= ii, 1.0, 0.0).astype(BF16)
        for k in range(nblk):
            gates = gates_ref[k * tb:(k + 1) * tb, :]
            picked = gates > 0.0
            incl = _dot(tri, jnp.where(picked, 1.0, 0.0).astype(BF16))
            rank_ref[k] = jnp.where(picked, incl - 1.0, -1.0).T
            gate_ref[k] = gates.T

    def compact(k, s):
        slot = rank_ref[k, pl.ds(e, 1), :]
        gate = gate_ref[k, pl.ds(e, 1), :]
        row = (lax.broadcasted_iota(jnp.int32, (MOE_CAP, tb), 0) + s * MOE_CAP).astype(F32)
        hit = slot == row
        onehot = jnp.where(hit, 1.0, 0.0).astype(BF16)
        xe = _dot(onehot, x_ref[k * tb:(k + 1) * tb, :]).astype(BF16)
        w = jnp.sum(jnp.where(hit, gate, 0.0), axis=-1, keepdims=True)
        return onehot, xe, w

    def expert(xe):
        y = jnp.broadcast_to(b2_ref[...], (xe.shape[0], w2_ref.shape[1]))
        for c0 in range(0, d_ff, 2 * LANE):
            x_glu = _dot(xe, w1_ref[:, c0:c0 + 2 * LANE]) + b1_ref[:, c0:c0 + 2 * LANE]
            x_lin = (_dot(xe, w1_ref[:, d_ff + c0:d_ff + c0 + 2 * LANE])
                     + b1_ref[:, d_ff + c0:d_ff + c0 + 2 * LANE])
            x_glu = jnp.minimum(x_glu, SWIGLU_LIMIT)
            x_lin = jnp.clip(x_lin, -SWIGLU_LIMIT, SWIGLU_LIMIT)
            act = x_glu * _sigmoid(SWIGLU_ALPHA * x_glu) * (x_lin + 1.0)
            y = y + _dot(act.astype(BF16), w2_ref[c0:c0 + 2 * LANE, :])
        return y

    def scatter(k, onehot, yw):
        f_ref[k * tb:(k + 1) * tb, :] += _dot_tn(onehot, yw.astype(BF16))

    first = [compact(k, 0) for k in range(nblk)]
    y = expert(jnp.concatenate([xe for _, xe, _ in first], axis=0))
    for k, (onehot, _, w) in enumerate(first):
        scatter(k, onehot, y[k * MOE_CAP:(k + 1) * MOE_CAP] * w)

    for k in range(nblk):
        count = jnp.max(rank_ref[k, pl.ds(e, 1), :]).astype(jnp.int32) + 1

        def overflow(s, carry, k=k):
            onehot, xe, w = compact(k, s)
            scatter(k, onehot, expert(xe) * w)
            return carry

        lax.fori_loop(1, (count + MOE_CAP - 1) // MOE_CAP, overflow, 0)


def _moe(x, gates, w1, b1, w2, b2, layer, first_tok, n_tok):
    d = x.shape[1]
    nblk = MOE_GROUP if n_tok % (MOE_GROUP * MOE_BLOCK) == 0 else 1
    tb = math.gcd(n_tok, MOE_BLOCK)
    rows = nblk * tb
    d_ff = w2.shape[1]
    off = first_tok // rows
    assert first_tok % rows == 0 and n_tok % rows == 0
    expert = lambda i, e: (layer * N_EXPERTS + e, 0, 0)
    return pl.pallas_call(
        functools.partial(_moe_kernel, nblk=nblk),
        out_shape=jax.ShapeDtypeStruct((n_tok, d), F32),
        grid=(n_tok // rows, N_EXPERTS),
        in_specs=[pl.BlockSpec((rows, d), lambda i, e: (i + off, 0)),
                  pl.BlockSpec((rows, LANE), lambda i, e: (i + off, 0)),
                  pl.BlockSpec((None, d, 2 * d_ff), expert),
                  pl.BlockSpec((None, 1, 2 * d_ff), expert),
                  pl.BlockSpec((None, d_ff, d), expert),
                  pl.BlockSpec((None, 1, d), expert)],
        out_specs=pl.BlockSpec((rows, d), lambda i, e: (i, 0)),
        scratch_shapes=[pltpu.VMEM((nblk, LANE, tb), F32), pltpu.VMEM((nblk, LANE, tb), F32)],
        compiler_params=pltpu.CompilerParams(dimension_semantics=("arbitrary", "arbitrary"),
                                             vmem_limit_bytes=MOE_VMEM_LIMIT),
        name="moe",
    )(x, gates, w1, b1, w2, b2)


def _resid_kernel(h_ref, g_ref, fl_ref, *rest, first):
    if first == 0:
        fc_ref, o_ref = rest
        f = jnp.where(pl.program_id(1) == 0, fc_ref[...], fl_ref[...])
    else:
        (o_ref,) = rest
        f = fl_ref[...]
    o_ref[...] = h_ref[...] + g_ref[...] * f


def _resid(h, g2, f_lat, f_ctx):
    bsz, n_tok, d = h.shape
    tm = TOK_TILE
    first = 0 if f_ctx is not None else 1
    in_specs = [pl.BlockSpec((None, tm, d), lambda b, i: (b, i, 0)),
                pl.BlockSpec((None, None, 1, d), lambda b, i: (b, jnp.minimum(i + first, 1), 0, 0)),
                pl.BlockSpec((tm, d), lambda b, i: (jnp.maximum(i + first - 1, 0) * bsz + b, 0))]
    args = [h, g2, f_lat]
    if f_ctx is not None:
        in_specs.append(pl.BlockSpec((tm, d), lambda b, i: (b, 0)))
        args.append(f_ctx)
    return pl.pallas_call(
        functools.partial(_resid_kernel, first=first),
        out_shape=jax.ShapeDtypeStruct((bsz, n_tok, d), F32),
        grid=(bsz, n_tok // tm),
        in_specs=in_specs,
        out_specs=pl.BlockSpec((None, tm, d), lambda b, i: (b, i, 0)),
        compiler_params=_cparams(("arbitrary", "arbitrary")),
        name="resid",
    )(*args)


def _in_weight_perm():
    o = IN_OFFSETS
    big = np.concatenate([np.arange(o[0], o[5]), np.arange(o[6], o[10])])
    small = np.full((2, LANE), -1, np.int64)
    for d in range(2):
        for h in range(SSM_HEADS):
            small[d, h] = o[5] + d * SSM_HEADS + h
        for h in range(ML_HEADS):
            small[d, ML_HEADS + h] = o[10] + d * ML_HEADS + h
            small[d, 2 * ML_HEADS + h] = o[11] + d * ML_HEADS + h
    return np.concatenate([big, small.reshape(-1)])


def _rope_tables(n_ctx, n_lat):
    pos = np.arange(n_lat)
    row = (pos // GRID_W).astype(np.float32)
    col = (pos % GRID_W).astype(np.float32)
    n_freq = ML_HEAD_DIM // 4
    inv_freq = jnp.asarray(ROPE_THETA, F32) ** (-jnp.arange(n_freq, dtype=F32) / n_freq)
    ang = jnp.concatenate([jnp.asarray(row)[:, None] * inv_freq, jnp.asarray(col)[:, None] * inv_freq], axis=-1)
    cos, sin = jnp.cos(ang), jnp.sin(ang)
    cosf = jnp.tile(jnp.concatenate([cos, cos], axis=-1), (1, ML_HEADS))
    sinf = jnp.tile(jnp.concatenate([-sin, sin], axis=-1), (1, ML_HEADS))
    cosf = jnp.concatenate([jnp.ones((n_ctx, ML_WIDTH), F32), cosf], axis=0)
    sinf = jnp.concatenate([jnp.zeros((n_ctx, ML_WIDTH), F32), sinf], axis=0)
    return cosf, sinf


def _lane_rows(rows):
    out = jnp.zeros((8, LANE), F32)
    for r, (off, vec) in enumerate(rows):
        out = out.at[r, off:off + vec.shape[0]].set(vec.astype(F32))
    return out


def kernel(x, c, ctx, c_ctx, ada_w, ada_b, norm_g, w_in, na_qk_g, na_rpb, ssm_conv_w, ssm_conv_b,
           ssm_dt_bias, ssm_a_log, ssm_d, ssm_norm_g, ml_gate_b, ml_norm_g, w_out,
           router_w, router_b, exp_w1, exp_b1, exp_w2, exp_b2):
    bsz, n_lat, d = x.shape
    n_ctx = ctx.shape[1]
    rows = n_lat // GRID_W
    depth = w_in.shape[0]
    assert bsz < 8 and n_ctx == TOK_TILE and n_lat % TOK_TILE == 0 and rows >= NA_KROWS

    h = jnp.concatenate([ctx, x], axis=1)
    cond = jnp.zeros((8, d), F32).at[:bsz].set(c).at[bsz].set(c_ctx)
    cosf, sinf = _rope_tables(n_ctx, n_lat)
    perm = _in_weight_perm()
    seg = np.arange(2 * LANE) // NA_HEAD_DIM
    bd = jnp.asarray(seg[:, None] == seg[None, :], BF16)

    n_exp = exp_w1.shape[0] * exp_w1.shape[1]
    w1_all = exp_w1.astype(BF16).reshape((n_exp,) + exp_w1.shape[2:])
    w2_all = exp_w2.astype(BF16).reshape((n_exp,) + exp_w2.shape[2:])
    b1_all = exp_b1.reshape(n_exp, 1, -1)
    b2_all = exp_b2.reshape(n_exp, 1, -1)

    out = None
    for layer in range(depth):
        ctx_out = layer < depth - 1
        mod = _ada(cond, ada_w[layer], ada_b[layer])
        mods = []
        for m in jnp.split(mod, 6, axis=-1):
            lat = m[:bsz]
            cx = jnp.broadcast_to(m[bsz][None], (bsz, d))
            mods.append(jnp.stack([cx, lat], axis=1).reshape(bsz, 2, 1, d))
        sh1, sc1, g1, sh2, sc2, g2 = mods

        w = jnp.where(perm[None, :] >= 0, w_in[layer][:, np.maximum(perm, 0)], 0.0).astype(BF16)
        qkg = jnp.concatenate([jnp.tile(na_qk_g[layer, 0] * NA_HEAD_DIM ** -0.5, NA_HEADS),
                               jnp.tile(na_qk_g[layer, 1], NA_HEADS)]).reshape(1, 2 * NA_WIDTH)
        big, small = _inproj(h, sh1, sc1, norm_g[layer, 0].reshape(1, d), w, qkg, bd, cosf, sinf)

        bias = _attention_bias(na_rpb[layer], rows)
        ya = _attention(big, bias, n_ctx, rows, ctx_out)

        conv_w = jnp.zeros((8, SSM_CONV_CH), F32).at[:SSM_CONV].set(ssm_conv_w[layer])
        ssm_par = jnp.stack([_lane_rows([(0, ssm_dt_bias[layer, dr]), (0, ssm_a_log[layer, dr]),
                                         (0, ssm_d[layer, dr])]) for dr in range(2)])
        yb = _ssd(big, small, conv_w, ssm_conv_b[layer].reshape(1, -1), ssm_par, n_ctx)

        gate_row = [jnp.concatenate([jnp.zeros((ML_HEADS,), F32), ml_gate_b[layer, dr, 0],
                                     ml_gate_b[layer, dr, 1]]) for dr in range(2)]
        ml_par = jnp.stack([_lane_rows([(0, gate_row[dr])]) for dr in range(2)])
        yc = _mlstm(big, small, ml_par, n_ctx)

        rw = jnp.zeros((d, LANE), F32).at[:, :N_EXPERTS].set(router_w[layer])
        rwh = rw.astype(BF16)
        rwl = (rw - rwh.astype(F32)).astype(BF16)
        rb = jnp.full((1, LANE), NEG, F32).at[0, :N_EXPERTS].set(router_b[layer])
        hn, v, gates = _merge(ya, yb, yc, big, h, g1, sh2, sc2, norm_g[layer, 1].reshape(1, d),
                              ssm_norm_g[layer].reshape(1, -1), ml_norm_g[layer].reshape(1, -1), bd,
                              w_out[layer].astype(BF16), rwh, rwl, rb, n_ctx, ctx_out)

        n_lat_tok = bsz * n_lat
        f_lat = _moe(v, gates, w1_all, b1_all, w2_all, b2_all, layer, 0, n_lat_tok)
        f_ctx = (_moe(v, gates, w1_all, b1_all, w2_all, b2_all, layer, n_lat_tok, bsz * n_ctx)
                 if ctx_out else None)
        h = _resid(hn, g2, f_lat, f_ctx)
        out = h
    return out
```

```python
import functools
import math

import numpy as np
import jax
import jax.numpy as jnp
from jax import lax
from jax.experimental import pallas as pl
from jax.experimental.pallas import tpu as pltpu

F32 = jnp.float32
BF16 = jnp.bfloat16

GRID_W = 64
EPS = 1e-6
NA_HEADS = 8
NA_HEAD_DIM = 64
NA_WIN_H = 8
NA_WIN_W = 16
SSM_HEADS = 4
SSM_HEAD_DIM = 64
SSM_GROUPS = 2
SSM_STATE = 128
SSM_CONV = 5
ML_HEADS = 4
ML_HEAD_DIM = 64
ROPE_THETA = 10000.0
N_EXPERTS = 32
TOP_K = 4
SWIGLU_ALPHA = 1.702
SWIGLU_LIMIT = 7.0

NA_WIDTH = NA_HEADS * NA_HEAD_DIM
SSM_WIDTH = SSM_HEADS * SSM_HEAD_DIM
SSM_BC = SSM_GROUPS * SSM_STATE
SSM_CONV_CH = SSM_WIDTH + 2 * SSM_BC
ML_WIDTH = ML_HEADS * ML_HEAD_DIM
IN_SPLITS = (NA_WIDTH, NA_WIDTH, NA_WIDTH, SSM_CONV_CH, SSM_WIDTH, 2 * SSM_HEADS,
             ML_WIDTH, ML_WIDTH, ML_WIDTH, ML_WIDTH, 2 * ML_HEADS, 2 * ML_HEADS)
IN_OFFSETS = tuple(int(o) for o in np.cumsum((0,) + IN_SPLITS))

LANE = 128
CHUNK = 128
HALO = 16
TOK_TILE = 256
NA_ROWS = 2
NA_KROWS = NA_ROWS + NA_WIN_H
BIG_W = 3 * NA_WIDTH + SSM_CONV_CH + SSM_WIDTH + 4 * ML_WIDTH
SMALL_W = 2 * LANE
MOE_BLOCK = 1024
MOE_GROUP = 2
MOE_CAP = 160
SCAN_BATCH = 4
NEG = -1e30
VMEM_LIMIT = 48 * 1024 * 1024
MOE_VMEM_LIMIT = 56 * 1024 * 1024

COL_Q, COL_K, COL_V = 0, NA_WIDTH, 2 * NA_WIDTH
COL_XBC = 3 * NA_WIDTH
COL_Z = COL_XBC + SSM_CONV_CH
COL_MQ = COL_Z + SSM_WIDTH
COL_MK = COL_MQ + ML_WIDTH
COL_MV = COL_MK + ML_WIDTH
COL_MO = COL_MV + ML_WIDTH


def _cparams(sem):
    return pltpu.CompilerParams(dimension_semantics=sem, vmem_limit_bytes=VMEM_LIMIT)


def _sigmoid(x):
    return 1.0 / (1.0 + jnp.exp(-x))


def _softplus(x):
    return jnp.maximum(x, 0.0) + jnp.log1p(jnp.exp(-jnp.abs(x)))


def _dot(a, b):
    return jnp.dot(a, b, preferred_element_type=F32)


def _dot_nt(a, b):
    return lax.dot_general(a, b, (((1,), (1,)), ((), ())), preferred_element_type=F32)


def _dot_tn(a, b):
    return lax.dot_general(a, b, (((0,), (0,)), ((), ())), preferred_element_type=F32)


def _split3(x):
    x1 = x.astype(BF16)
    r1 = x - x1.astype(F32)
    x2 = r1.astype(BF16)
    x3 = (r1 - x2.astype(F32)).astype(BF16)
    return x1, x2, x3


def _split2(x):
    x1 = x.astype(BF16)
    return x1, (x - x1.astype(F32)).astype(BF16)


def _segment_mean_sq(x, bd):
    hi, lo = _split2(x * x)
    return (_dot(hi, bd) + _dot(lo, bd)) * (1.0 / NA_HEAD_DIM)


def _ada_kernel(c_ref, w_ref, b_ref, o_ref):
    c = c_ref[...]
    s = c * _sigmoid(c)
    o_ref[...] = jnp.dot(s, w_ref[...], preferred_element_type=F32,
                         precision=lax.Precision.HIGHEST) + b_ref[...]


def _ada(cond, w, b):
    n, d = cond.shape
    wcols = w.shape[1]
    bn = wcols // 4
    return pl.pallas_call(
        _ada_kernel,
        out_shape=jax.ShapeDtypeStruct((n, wcols), F32),
        grid=(4,),
        in_specs=[pl.BlockSpec((n, d), lambda j: (0, 0)),
                  pl.BlockSpec((d, bn), lambda j: (0, j)),
                  pl.BlockSpec((1, bn), lambda j: (0, j))],
        out_specs=pl.BlockSpec((n, bn), lambda j: (0, j)),
        compiler_params=_cparams(("arbitrary",)),
        name="ada",
    )(cond, w, b.reshape(1, wcols))


def _inproj_kernel(h_ref, sh_ref, sc_ref, g_ref, w_ref, qkg_ref, bd_ref, cos_ref, sin_ref,
                   big_ref, small_ref):
    x = h_ref[...]
    ms = jnp.mean(x * x, axis=-1, keepdims=True)
    u = x * lax.rsqrt(ms + EPS) * g_ref[...]
    u = u * (1.0 + sc_ref[...]) + sh_ref[...]
    ub = u.astype(BF16)
    bd = bd_ref[...]

    for c0 in range(0, 2 * NA_WIDTH, 2 * LANE):
        acc = _dot(ub, w_ref[:, c0:c0 + 2 * LANE])
        msq = _segment_mean_sq(acc, bd)
        y = acc * lax.rsqrt(msq + EPS) * qkg_ref[:, c0:c0 + 2 * LANE]
        big_ref[:, c0:c0 + 2 * LANE] = y.astype(BF16)
    for c0 in range(COL_V, COL_MQ, 2 * LANE):
        big_ref[:, c0:c0 + 2 * LANE] = _dot(ub, w_ref[:, c0:c0 + 2 * LANE]).astype(BF16)
    lane = lax.broadcasted_iota(jnp.int32, (x.shape[0], LANE), 1)
    first_half = (lane % ML_HEAD_DIM) < (ML_HEAD_DIM // 2)
    for base, scale in ((COL_MQ, ML_HEAD_DIM ** -0.5), (COL_MK, 1.0)):
        for j in range(ML_WIDTH // LANE):
            c0 = base + j * LANE
            acc = _dot(ub, w_ref[:, c0:c0 + LANE])
            swapped = jnp.where(first_half,
                                pltpu.roll(acc, LANE - ML_HEAD_DIM // 2, 1),
                                pltpu.roll(acc, ML_HEAD_DIM // 2, 1))
            y = acc * cos_ref[:, j * LANE:(j + 1) * LANE] + swapped * sin_ref[:, j * LANE:(j + 1) * LANE]
            big_ref[:, c0:c0 + LANE] = (y * scale).astype(BF16)
    for c0 in range(COL_MV, BIG_W, 2 * LANE):
        big_ref[:, c0:c0 + 2 * LANE] = _dot(ub, w_ref[:, c0:c0 + 2 * LANE]).astype(BF16)
    small_ref[...] = _dot(ub, w_ref[:, BIG_W:BIG_W + SMALL_W])


def _inproj(h, sh, sc, g, w, qkg, bd, cosf, sinf):
    bsz, s_len, d = h.shape
    tm = TOK_TILE
    mod_spec = pl.BlockSpec((None, None, 1, d), lambda b, i: (b, jnp.minimum(i, 1), 0, 0))
    full = lambda shape: pl.BlockSpec(shape, lambda b, i: (0,) * len(shape))
    return pl.pallas_call(
        _inproj_kernel,
        out_shape=(jax.ShapeDtypeStruct((bsz, s_len, BIG_W), BF16),
                   jax.ShapeDtypeStruct((bsz, s_len, SMALL_W), F32)),
        grid=(bsz, s_len // tm),
        in_specs=[pl.BlockSpec((None, tm, d), lambda b, i: (b, i, 0)),
                  mod_spec, mod_spec, full((1, d)), full(w.shape), full(qkg.shape), full(bd.shape),
                  pl.BlockSpec((tm, ML_WIDTH), lambda b, i: (i, 0)),
                  pl.BlockSpec((tm, ML_WIDTH), lambda b, i: (i, 0))],
        out_specs=(pl.BlockSpec((None, tm, BIG_W), lambda b, i: (b, i, 0)),
                   pl.BlockSpec((None, tm, SMALL_W), lambda b, i: (b, i, 0))),
        compiler_params=_cparams(("arbitrary", "arbitrary")),
        name="inproj",
    )(h, sh, sc, g, w, qkg, bd, cosf, sinf)


def _na_kernel(q_ref, k0, k1, k2, k3, k4, v0, v1, v2, v3, v4, kc_ref, vc_ref, bias_ref, o_ref):
    nq = q_ref.shape[0]
    lane = lax.broadcasted_iota(jnp.int32, (nq, LANE), 1)
    low = lane < NA_HEAD_DIM
    zero = jnp.zeros((), BF16)
    outs = []
    for p in range(NA_WIDTH // LANE):
        cs = slice(p * LANE, (p + 1) * LANE)
        qp = q_ref[:, cs]
        q01 = jnp.concatenate([jnp.where(low, qp, zero), jnp.where(low, zero, qp)], axis=0)
        kw = jnp.concatenate([r[:, cs] for r in (k0, k1, k2, k3, k4)], axis=0)
        vw = jnp.concatenate([r[:, cs] for r in (v0, v1, v2, v3, v4)], axis=0)
        vc = vc_ref[:, cs]
        vw = jnp.concatenate([vw, jnp.ones(vw.shape, BF16)], axis=1)
        vc = jnp.concatenate([vc, jnp.ones(vc.shape, BF16)], axis=1)
        s_w = _dot_nt(q01, kw) + bias_ref[p]
        s_c = _dot_nt(q01, kc_ref[:, cs])
        m = jnp.maximum(jnp.max(s_w, axis=-1, keepdims=True), jnp.max(s_c, axis=-1, keepdims=True))
        p_w = jnp.exp(s_w - m).astype(BF16)
        p_c = jnp.exp(s_c - m).astype(BF16)
        o01 = _dot(p_w, vw) + _dot(p_c, vc)
        o01 = o01[:, :LANE] / o01[:, LANE:]
        outs.append(jnp.where(low, o01[:nq], o01[nq:]).astype(o_ref.dtype))
    o_ref[...] = jnp.concatenate(outs, axis=1)


def _attention(big, bias, n_ctx, rows, ctx_out):
    bsz, s_len, _ = big.shape
    nq = NA_ROWS * GRID_W
    ncb = n_ctx // nq
    nrp = rows // NA_ROWS
    first = 0 if ctx_out else ncb
    n_win = NA_KROWS // NA_ROWS

    def lat(i):
        return jnp.maximum(i + first - ncb, 0)

    def kv_spec(j, col):
        return pl.BlockSpec(
            (None, nq, NA_WIDTH),
            lambda b, i: (b, ncb + jnp.clip(lat(i) - 2, 0, nrp - n_win) + j, col))

    def bias_type(b, i):
        step = i + first
        r = lat(i)
        t = jnp.where(r < 2, r, jnp.where(r >= nrp - 2, r - (nrp - 2) + 3, 2))
        return (jnp.where(step < ncb, 5, t), 0, 0, 0)

    return pl.pallas_call(
        _na_kernel,
        out_shape=jax.ShapeDtypeStruct((bsz, s_len, NA_WIDTH), BF16),
        grid=(bsz, ncb + nrp - first),
        in_specs=[pl.BlockSpec((None, nq, NA_WIDTH), lambda b, i: (b, i + first, 0))]
                 + [kv_spec(j, 1) for j in range(n_win)] + [kv_spec(j, 2) for j in range(n_win)]
                 + [pl.BlockSpec((None, n_ctx, NA_WIDTH), lambda b, i: (b, 0, 1)),
                    pl.BlockSpec((None, n_ctx, NA_WIDTH), lambda b, i: (b, 0, 2)),
                    pl.BlockSpec((None, NA_WIDTH // LANE, 2 * nq, NA_KROWS * GRID_W), bias_type)],
        out_specs=pl.BlockSpec((None, nq, NA_WIDTH), lambda b, i: (b, i + first, 0)),
        compiler_params=_cparams(("arbitrary", "arbitrary")),
        name="attention",
    )(big, *([big] * (2 * n_win + 2)), bias)


def _attention_bias(rpb, rows):
    nrp = rows // NA_ROWS
    n_win = NA_KROWS // NA_ROWS
    reps = np.array([0, 1, 2, nrp - 2, nrp - 1])
    qr = (NA_ROWS * reps)[:, None] + np.arange(NA_ROWS)[None, :]
    ws = NA_ROWS * np.clip(reps - 2, 0, nrp - n_win)
    kr = ws[:, None] + np.arange(NA_KROWS)[None, :]
    r0 = np.clip(qr - NA_WIN_H // 2, 0, rows - NA_WIN_H)
    ok_r = (kr[:, None, :] >= r0[:, :, None]) & (kr[:, None, :] < r0[:, :, None] + NA_WIN_H)
    dr = np.clip(kr[:, None, :] - qr[:, :, None] + (NA_WIN_H - 1), 0, 2 * NA_WIN_H - 2)
    qc = np.arange(GRID_W)
    c0 = np.clip(qc - NA_WIN_W // 2, 0, GRID_W - NA_WIN_W)
    kc = np.arange(GRID_W)
    ok_c = (kc[None, :] >= c0[:, None]) & (kc[None, :] < c0[:, None] + NA_WIN_W)
    dc = np.clip(kc[None, :] - qc[:, None] + (NA_WIN_W - 1), 0, 2 * NA_WIN_W - 2)
    sel_r = jnp.asarray(dr[..., None] == np.arange(2 * NA_WIN_H - 1), F32)
    sel_c = jnp.asarray(dc[..., None] == np.arange(2 * NA_WIN_W - 1), F32)
    hp = lax.Precision.HIGHEST
    g = jnp.einsum('hrc,taer->thaec', rpb.astype(F32), sel_r, precision=hp)
    g = jnp.einsum('thaec,qkc->thaqek', g, sel_c, precision=hp)
    ok = ok_r[:, None, :, None, :, None] & ok_c[None, None, None, :, None, :]
    g = jnp.where(ok, g, NEG)
    g = g.reshape(5, NA_HEADS * NA_ROWS * GRID_W, NA_KROWS * GRID_W)
    g = g.reshape(5, NA_WIDTH // LANE, 2 * NA_ROWS * GRID_W, NA_KROWS * GRID_W)
    masked = jnp.full((1,) + g.shape[1:], NEG, F32)
    return jnp.concatenate([g, masked], axis=0)


def _scan_chunk(d, t, ncc, nc):
    bwd = jnp.where(t < ncc, ncc - 1 - t, ncc + (nc - 1 - t))
    return jnp.where(d == 0, t, bwd)


def _precedes(d):
    ii = lax.broadcasted_iota(jnp.int32, (CHUNK, CHUNK), 0)
    jj = lax.broadcasted_iota(jnp.int32, (CHUNK, CHUNK), 1)
    return (jj - ii) * (1 - 2 * d) <= 0


def _cumsum_scan_order(prec, x):
    tri = jnp.where(prec, 1.0, 0.0).astype(BF16)
    x1, x2, x3 = _split3(x)
    return _dot(tri, x1) + _dot(tri, x2) + _dot(tri, x3)


def _ssd_kernel(x_ref, xp_ref, xn_ref, sm_ref, cw_ref, cb_ref, par_ref, y_ref, st_ref, cbuf_ref,
                *, ncc, nc):
    d = pl.program_id(1)
    t = pl.program_id(2)
    chunk = _scan_chunk(d, t, ncc, nc)

    @pl.when(t == 0)
    def _():
        st_ref[...] = jnp.zeros_like(st_ref)

    prev_ok = jnp.where(jnp.logical_and(chunk != 0, chunk != ncc), 1.0, 0.0)
    next_ok = jnp.where(jnp.logical_and(chunk != ncc - 1, chunk != nc - 1), 1.0, 0.0)
    prec = _precedes(d)
    rep = SSM_HEADS // SSM_GROUPS
    nb = x_ref.shape[0]
    for bb in range(nb):
        cbuf_ref[bb, 0:HALO, :] = xp_ref[bb].astype(F32) * prev_ok
        cbuf_ref[bb, HALO:HALO + CHUNK, :] = x_ref[bb].astype(F32)
        cbuf_ref[bb, HALO + CHUNK:, :] = xn_ref[bb].astype(F32) * next_ok
    taps = [[cbuf_ref[bb, HALO - SSM_CONV // 2 + k:HALO - SSM_CONV // 2 + k + CHUNK, :]
             for k in range(SSM_CONV)] for bb in range(nb)]
    gates_in = [sm_ref[bb] for bb in range(nb)]
    states = [[st_ref[bb, h] for h in range(SSM_HEADS)] for bb in range(nb)]
    stores = []
    for bb in range(nb):
        acc = jnp.broadcast_to(cb_ref[...], (CHUNK, SSM_CONV_CH))
        for k in range(SSM_CONV):
            acc = acc + taps[bb][k] * cw_ref[k:k + 1, :]
        xc = acc * _sigmoid(acc)
        xs = xc[:, :SSM_WIDTH]
        bm = xc[:, SSM_WIDTH:SSM_WIDTH + SSM_BC]
        cm = xc[:, SSM_WIDTH + SSM_BC:]

        dt = _softplus(gates_in[bb] + par_ref[0:1, :])
        la = dt * (-jnp.exp(par_ref[1:2, :]))
        acum = _cumsum_scan_order(prec, la)
        acum_t = acum.T
        total = jnp.sum(la, axis=0, keepdims=True)

        for g in range(SSM_GROUPS):
            cg = cm[:, g * SSM_STATE:(g + 1) * SSM_STATE].astype(BF16)
            bg = bm[:, g * SSM_STATE:(g + 1) * SSM_STATE].astype(BF16)
            cb = _dot_nt(cg, bg)
            for hh in range(rep):
                h = g * rep + hh
                a_col = acum[:, h:h + 1]
                a_row = acum_t[h:h + 1, :]
                tot = total[:, h:h + 1]
                x_h = xs[:, h * SSM_HEAD_DIM:(h + 1) * SSM_HEAD_DIM]
                xdt = x_h * dt[:, h:h + 1]
                lmat = jnp.exp(jnp.where(prec, a_col - a_row, -jnp.inf))
                h_in = states[bb][h]
                y = _dot((cb * lmat).astype(BF16), xdt.astype(BF16))
                y = y + _dot(cg, h_in.astype(BF16)) * jnp.exp(a_col)
                y = y + x_h * par_ref[2:3, h:h + 1]
                xw = (xdt * jnp.exp(tot - a_col)).astype(BF16)
                stores.append((bb, h, y, jnp.exp(tot) * h_in + _dot_tn(bg, xw)))

    for bb, h, y, h_new in stores:
        y_ref[bb, :, h * SSM_HEAD_DIM:(h + 1) * SSM_HEAD_DIM] = y
        st_ref[bb, h] = h_new


def _scan_batch(bsz):
    return math.gcd(bsz, SCAN_BATCH)


def _ssd(big, small, conv_w, conv_b, par, n_ctx):
    bsz, s_len, _ = big.shape
    ncc, nc = n_ctx // CHUNK, s_len // CHUNK
    per = CHUNK // HALO
    nb = _scan_batch(bsz)
    ch = functools.partial(_scan_chunk, ncc=ncc, nc=nc)
    cblk = COL_XBC // SSM_CONV_CH
    return pl.pallas_call(
        functools.partial(_ssd_kernel, ncc=ncc, nc=nc),
        out_shape=jax.ShapeDtypeStruct((2, bsz, s_len, SSM_WIDTH), F32),
        grid=(bsz // nb, 2, nc),
        in_specs=[
            pl.BlockSpec((nb, CHUNK, SSM_CONV_CH), lambda b, d, t: (b, ch(d, t), cblk)),
            pl.BlockSpec((nb, HALO, SSM_CONV_CH),
                         lambda b, d, t: (b, jnp.maximum(ch(d, t) * per - 1, 0), cblk)),
            pl.BlockSpec((nb, HALO, SSM_CONV_CH),
                         lambda b, d, t: (b, jnp.minimum((ch(d, t) + 1) * per, nc * per - 1), cblk)),
            pl.BlockSpec((nb, CHUNK, LANE), lambda b, d, t: (b, ch(d, t), d)),
            pl.BlockSpec(conv_w.shape, lambda b, d, t: (0, 0)),
            pl.BlockSpec(conv_b.shape, lambda b, d, t: (0, 0)),
            pl.BlockSpec((None, 8, LANE), lambda b, d, t: (d, 0, 0)),
        ],
        out_specs=pl.BlockSpec((None, nb, CHUNK, SSM_WIDTH), lambda b, d, t: (d, b, ch(d, t), 0)),
        scratch_shapes=[pltpu.VMEM((nb, SSM_HEADS, SSM_STATE, SSM_HEAD_DIM), F32),
                        pltpu.VMEM((nb, CHUNK + 2 * HALO, SSM_CONV_CH), F32)],
        compiler_params=_cparams(("arbitrary", "arbitrary", "arbitrary")),
        name="ssd",
    )(big, big, big, small, conv_w, conv_b, par)


def _mlstm_kernel(q_ref, k_ref, v_ref, sm_ref, par_ref, y_ref, ct_ref, n_ref, m_ref, *, ncc, nc):
    d = pl.program_id(1)
    t = pl.program_id(2)

    @pl.when(t == 0)
    def _():
        ct_ref[...] = jnp.zeros_like(ct_ref)
        n_ref[...] = jnp.zeros_like(n_ref)
        m_ref[...] = jnp.zeros_like(m_ref)

    prec = _precedes(d)
    nb = q_ref.shape[0]
    loaded = [(sm_ref[bb], q_ref[bb], k_ref[bb], v_ref[bb],
               [(m_ref[bb, h][0:1, 0:1], ct_ref[bb, h], n_ref[bb, h][0:1, :]) for h in range(ML_HEADS)])
              for bb in range(nb)]
    stores = []
    for bb in range(nb):
        sm, q_all, k_all, v_all, states = loaded[bb]
        pre = sm + par_ref[0:1, :]
        lf = -_softplus(-pre)
        bcum = _cumsum_scan_order(prec, lf)
        bcum_t = bcum.T
        li_t = pre.T
        gsum = jnp.sum(lf, axis=0, keepdims=True)

        for h in range(ML_HEADS):
            hs = slice(h * ML_HEAD_DIM, (h + 1) * ML_HEAD_DIM)
            ci, cf = ML_HEADS + h, 2 * ML_HEADS + h
            b_col = bcum[:, cf:cf + 1]
            b_row = bcum_t[cf:cf + 1, :]
            gt = gsum[:, cf:cf + 1]
            li_col = pre[:, ci:ci + 1]
            li_row = li_t[ci:ci + 1, :]
            qh = q_all[:, hs]
            kh = k_all[:, hs]
            vh = v_all[:, hs]
            m_in, ct_in, n_in = states[h]

            log_d = jnp.where(prec, b_col - b_row + li_row, -jnp.inf)
            log_inter = b_col + m_in
            m_i = jnp.maximum(jnp.max(log_d, axis=-1, keepdims=True), log_inter)
            s = _dot_nt(qh, kh) * jnp.exp(log_d - m_i)
            inter = jnp.exp(log_inter - m_i)
            num = _dot(s.astype(BF16), vh) + _dot(qh, ct_in.astype(BF16)) * inter
            den = (jnp.sum(s, axis=-1, keepdims=True)
                   + jnp.sum(qh.astype(F32) * n_in, axis=-1, keepdims=True) * inter)
            y = num / jnp.maximum(jnp.abs(den), jnp.exp(-m_i))

            w_end = gt - b_col + li_col
            m_loc = jnp.max(w_end, axis=0, keepdims=True)
            ke = kh.astype(F32) * jnp.exp(w_end - m_loc)
            m_new = jnp.maximum(gt + m_in, m_loc)
            a_old = jnp.exp(gt + m_in - m_new)
            a_new = jnp.exp(m_loc - m_new)
            ct_new = a_old * ct_in + a_new * _dot_tn(ke.astype(BF16), vh)
            n_new = a_old * n_in + a_new * jnp.sum(ke, axis=0, keepdims=True)
            stores.append((bb, h, hs, y, ct_new, n_new, m_new))

    for bb, h, hs, y, ct_new, n_new, m_new in stores:
        y_ref[bb, :, hs] = y
        ct_ref[bb, h] = ct_new
        n_ref[bb, h] = jnp.broadcast_to(n_new, n_ref.shape[2:])
        m_ref[bb, h] = jnp.broadcast_to(m_new, m_ref.shape[2:])


def _mlstm(big, small, par, n_ctx):
    bsz, s_len, _ = big.shape
    ncc, nc = n_ctx // CHUNK, s_len // CHUNK
    nb = 1
    ch = functools.partial(_scan_chunk, ncc=ncc, nc=nc)

    def col_spec(col):
        return pl.BlockSpec((nb, CHUNK, ML_WIDTH), lambda b, d, t: (b, ch(d, t), col // ML_WIDTH))

    return pl.pallas_call(
        functools.partial(_mlstm_kernel, ncc=ncc, nc=nc),
        out_shape=jax.ShapeDtypeStruct((2, bsz, s_len, ML_WIDTH), F32),
        grid=(bsz // nb, 2, nc),
        in_specs=[col_spec(COL_MQ), col_spec(COL_MK), col_spec(COL_MV),
                  pl.BlockSpec((nb, CHUNK, LANE), lambda b, d, t: (b, ch(d, t), d)),
                  pl.BlockSpec((None, 8, LANE), lambda b, d, t: (d, 0, 0))],
        out_specs=pl.BlockSpec((None, nb, CHUNK, ML_WIDTH), lambda b, d, t: (d, b, ch(d, t), 0)),
        scratch_shapes=[pltpu.VMEM((nb, ML_HEADS, ML_HEAD_DIM, ML_HEAD_DIM), F32),
                        pltpu.VMEM((nb, ML_HEADS, 8, ML_HEAD_DIM), F32),
                        pltpu.VMEM((nb, ML_HEADS, 8, LANE), F32)],
        compiler_params=_cparams(("arbitrary", "arbitrary", "arbitrary")),
        name="mlstm",
    )(big, big, big, small, par)


def _merge_kernel(ya_ref, yb_ref, z_ref, yc_ref, o_ref, h_ref, g1_ref, sh_ref, sc_ref, ng_ref,
                  sg_ref, mg_ref, bd_ref, wo_ref, rwh_ref, rwl_ref, rb_ref,
                  hout_ref, v_ref, gates_ref):
    z = z_ref[...].astype(F32)
    yb = (yb_ref[0] + yb_ref[1]) * (z * _sigmoid(z))
    yb = yb * lax.rsqrt(jnp.mean(yb * yb, axis=-1, keepdims=True) + EPS) * sg_ref[...]
    yc = _sigmoid(o_ref[...].astype(F32)) * (yc_ref[0] + yc_ref[1])
    yc = yc * lax.rsqrt(_segment_mean_sq(yc, bd_ref[...]) + EPS) * mg_ref[...]
    y = (_dot(ya_ref[...], wo_ref[0:NA_WIDTH, :])
         + _dot(yb.astype(BF16), wo_ref[NA_WIDTH:NA_WIDTH + SSM_WIDTH, :])
         + _dot(yc.astype(BF16), wo_ref[NA_WIDTH + SSM_WIDTH:, :]))
    hn = h_ref[...] + g1_ref[...] * y
    hout_ref[...] = hn
    v = hn * lax.rsqrt(jnp.mean(hn * hn, axis=-1, keepdims=True) + EPS) * ng_ref[...]
    v = v * (1.0 + sc_ref[...]) + sh_ref[...]
    v_hi, v_lo = _split2(v)
    v_ref[...] = v_hi
    rwh = rwh_ref[...]
    logits = (_dot_nt(rwh, v_hi) + _dot_nt(rwh, v_lo) + _dot_nt(rwl_ref[...], v_hi))[:N_EXPERTS] + rb_ref[...]

    row = lax.broadcasted_iota(jnp.int32, logits.shape, 0).astype(F32)
    work = logits
    top = None
    gates = jnp.zeros_like(logits)
    den = jnp.zeros((1, logits.shape[1]), F32)
    for _ in range(TOP_K):
        mx = jnp.max(work, axis=0, keepdims=True)
        idx = jnp.min(jnp.where(work == mx, row, float(N_EXPERTS)), axis=0, keepdims=True)
        hit = row == idx
        top = mx if top is None else top
        e = jnp.exp(mx - top)
        gates = gates + jnp.where(hit, e, 0.0)
        den = den + e
        work = jnp.where(hit, -jnp.inf, work)
    gates_ref[...] = jnp.concatenate(
        [gates / den, jnp.zeros((LANE - N_EXPERTS, logits.shape[1]), F32)], axis=0)


def _merge(ya, yb, yc, big, h, g1, sh2, sc2, ng, sg, mg, bd, wo, rwh, rwl, rb, n_ctx, ctx_out):
    bsz, s_len, d = h.shape
    tm = TOK_TILE
    first = 0 if ctx_out else n_ctx // tm
    n_out = s_len - first * tm
    mod_spec = pl.BlockSpec((None, None, 1, d), lambda b, i: (b, jnp.minimum(i + first, 1), 0, 0))
    full = lambda shape: pl.BlockSpec(shape, lambda b, i: (0,) * len(shape))
    pair = lambda w: pl.BlockSpec((2, None, tm, w), lambda b, i: (0, b, i + first, 0))
    nl = (s_len - n_ctx) // tm

    def flat_row(b, i):
        if ctx_out:
            return jnp.where(i == 0, bsz * nl + b, (i - 1) * bsz + b)
        return i * bsz + b

    flat_tok = lambda w: pl.BlockSpec((tm, w), lambda b, i: (flat_row(b, i), 0))
    return pl.pallas_call(
        _merge_kernel,
        out_shape=(jax.ShapeDtypeStruct((bsz, n_out, d), F32),
                   jax.ShapeDtypeStruct((bsz * n_out, d), BF16),
                   jax.ShapeDtypeStruct((LANE, bsz * n_out), F32)),
        grid=(bsz, n_out // tm),
        in_specs=[pl.BlockSpec((None, tm, NA_WIDTH), lambda b, i: (b, i + first, 0)),
                  pair(SSM_WIDTH),
                  pl.BlockSpec((None, tm, SSM_WIDTH), lambda b, i: (b, i + first, COL_Z // SSM_WIDTH)),
                  pair(ML_WIDTH),
                  pl.BlockSpec((None, tm, ML_WIDTH), lambda b, i: (b, i + first, COL_MO // ML_WIDTH)),
                  pl.BlockSpec((None, tm, d), lambda b, i: (b, i + first, 0)),
                  mod_spec, mod_spec, mod_spec,
                  full((1, d)), full((1, SSM_WIDTH)), full((1, ML_WIDTH)), full(bd.shape),
                  full(wo.shape), full(rwh.shape), full(rwl.shape), full(rb.shape)],
        out_specs=(pl.BlockSpec((None, tm, d), lambda b, i: (b, i, 0)), flat_tok(d),
                   pl.BlockSpec((LANE, tm), lambda b, i: (0, flat_row(b, i)))),
        compiler_params=_cparams(("arbitrary", "arbitrary")),
        name="merge",
    )(ya, yb, big, yc, big, h, g1, sh2, sc2, ng, sg, mg, bd, wo, rwh, rwl, rb)


def _moe_kernel(x_ref, gates_ref, w1_ref, b1_ref, w2_ref, b2_ref, f_ref, rank_ref, *, nblk):
    e = pl.program_id(1)
    tb = x_ref.shape[0] // nblk
    d_ff = w2_ref.shape[0]

    @pl.when(e == 0)
    def _():
        f_ref[...] = jnp.zeros_like(f_ref)
        ii = lax.broadcasted_iota(jnp.int32, (tb, tb), 0)
        jj = lax.broadcasted_iota(jnp.int32, (tb, tb), 1)
        tri = jnp.where(ii <= jj, 1.0, 0.0).astype(BF16)
        for k in range(nblk):
            picked = gates_ref[:, k * tb:(k + 1) * tb] > 0.0
            incl = _dot(jnp.where(picked, 1.0, 0.0).astype(BF16), tri)
            rank_ref[k] = jnp.where(picked, incl - 1.0, -1.0)

    def compact(k, s):
        slot = rank_ref[k, pl.ds(e, 1), :]
        gate = gates_ref[pl.ds(e, 1), k * tb:(k + 1) * tb]
        row = (lax.broadcasted_iota(jnp.int32, (MOE_CAP, tb), 0) + s * MOE_CAP).astype(F32)
        hit = slot == row
        onehot = jnp.where(hit, 1.0, 0.0).astype(BF16)
        xe = _dot(onehot, x_ref[k * tb:(k + 1) * tb, :]).astype(BF16)
        w = jnp.sum(jnp.where(hit, gate, 0.0), axis=-1, keepdims=True)
        return onehot, xe, w

    def expert(xe):
        y = jnp.broadcast_to(b2_ref[...], (xe.shape[0], w2_ref.shape[1]))
        for c0 in range(0, d_ff, 2 * LANE):
            x_glu = _dot(xe, w1_ref[:, c0:c0 + 2 * LANE]) + b1_ref[:, c0:c0 + 2 * LANE]
            x_lin = (_dot(xe, w1_ref[:, d_ff + c0:d_ff + c0 + 2 * LANE])
                     + b1_ref[:, d_ff + c0:d_ff + c0 + 2 * LANE])
            x_glu = jnp.minimum(x_glu, SWIGLU_LIMIT)
            x_lin = jnp.clip(x_lin, -SWIGLU_LIMIT, SWIGLU_LIMIT)
            act = x_glu * _sigmoid(SWIGLU_ALPHA * x_glu) * (x_lin + 1.0)
            y = y + _dot(act.astype(BF16), w2_ref[c0:c0 + 2 * LANE, :])
        return y

    def scatter(k, onehot, yw):
        f_ref[k * tb:(k + 1) * tb, :] += _dot_tn(onehot, yw.astype(BF16))

    first = [compact(k, 0) for k in range(nblk)]
    y = expert(jnp.concatenate([xe for _, xe, _ in first], axis=0))
    for k, (onehot, _, w) in enumerate(first):
        scatter(k, onehot, y[k * MOE_CAP:(k + 1) * MOE_CAP] * w)

    for k in range(nblk):
        count = jnp.max(rank_ref[k, pl.ds(e, 1), :]).astype(jnp.int32) + 1

        def overflow(s, carry, k=k):
            onehot, xe, w = compact(k, s)
            scatter(k, onehot, expert(xe) * w)
            return carry

        lax.fori_loop(1, (count + MOE_CAP - 1) // MOE_CAP, overflow, 0)


def _moe(x, gates, w1, b1, w2, b2, layer, first_tok, n_tok):
    d = x.shape[1]
    nblk = MOE_GROUP if n_tok % (MOE_GROUP * MOE_BLOCK) == 0 else 1
    tb = math.gcd(n_tok, MOE_BLOCK)
    rows = nblk * tb
    d_ff = w2.shape[1]
    off = first_tok // rows
    assert first_tok % rows == 0 and n_tok % rows == 0
    expert = lambda i, e: (layer * N_EXPERTS + e, 0, 0)
    return pl.pallas_call(
        functools.partial(_moe_kernel, nblk=nblk),
        out_shape=jax.ShapeDtypeStruct((n_tok, d), F32),
        grid=(n_tok // rows, N_EXPERTS),
        in_specs=[pl.BlockSpec((rows, d), lambda i, e: (i + off, 0)),
                  pl.BlockSpec((LANE, rows), lambda i, e: (0, i + off)),
                  pl.BlockSpec((None, d, 2 * d_ff), expert),
                  pl.BlockSpec((None, 1, 2 * d_ff), expert),
                  pl.BlockSpec((None, d_ff, d), expert),
                  pl.BlockSpec((None, 1, d), expert)],
        out_specs=pl.BlockSpec((rows, d), lambda i, e: (i, 0)),
        scratch_shapes=[pltpu.VMEM((nblk, LANE, tb), F32)],
        compiler_params=pltpu.CompilerParams(dimension_semantics=("arbitrary", "arbitrary"),
                                             vmem_limit_bytes=MOE_VMEM_LIMIT),
        name="moe",
    )(x, gates, w1, b1, w2, b2)


def _resid_kernel(h_ref, g_ref, fl_ref, *rest, first):
    if first == 0:
        fc_ref, o_ref = rest
        f = jnp.where(pl.program_id(1) == 0, fc_ref[...], fl_ref[...])
    else:
        (o_ref,) = rest
        f = fl_ref[...]
    o_ref[...] = h_ref[...] + g_ref[...] * f


def _resid(h, g2, f_lat, f_ctx):
    bsz, n_tok, d = h.shape
    tm = TOK_TILE
    first = 0 if f_ctx is not None else 1
    in_specs = [pl.BlockSpec((None, tm, d), lambda b, i: (b, i, 0)),
                pl.BlockSpec((None, None, 1, d), lambda b, i: (b, jnp.minimum(i + first, 1), 0, 0)),
                pl.BlockSpec((tm, d), lambda b, i: (jnp.maximum(i + first - 1, 0) * bsz + b, 0))]
    args = [h, g2, f_lat]
    if f_ctx is not None:
        in_specs.append(pl.BlockSpec((tm, d), lambda b, i: (b, 0)))
        args.append(f_ctx)
    return pl.pallas_call(
        functools.partial(_resid_kernel, first=first),
        out_shape=jax.ShapeDtypeStruct((bsz, n_tok, d), F32),
        grid=(bsz, n_tok // tm),
        in_specs=in_specs,
        out_specs=pl.BlockSpec((None, tm, d), lambda b, i: (b, i, 0)),
        compiler_params=_cparams(("arbitrary", "arbitrary")),
        name="resid",
    )(*args)


def _in_weight_perm():
    o = IN_OFFSETS
    big = np.concatenate([np.arange(o[0], o[5]), np.arange(o[6], o[10])])
    small = np.full((2, LANE), -1, np.int64)
    for d in range(2):
        for h in range(SSM_HEADS):
            small[d, h] = o[5] + d * SSM_HEADS + h
        for h in range(ML_HEADS):
            small[d, ML_HEADS + h] = o[10] + d * ML_HEADS + h
            small[d, 2 * ML_HEADS + h] = o[11] + d * ML_HEADS + h
    return np.concatenate([big, small.reshape(-1)])


def _rope_tables(n_ctx, n_lat):
    pos = np.arange(n_lat)
    row = (pos // GRID_W).astype(np.float32)
    col = (pos % GRID_W).astype(np.float32)
    n_freq = ML_HEAD_DIM // 4
    inv_freq = jnp.asarray(ROPE_THETA, F32) ** (-jnp.arange(n_freq, dtype=F32) / n_freq)
    ang = jnp.concatenate([jnp.asarray(row)[:, None] * inv_freq, jnp.asarray(col)[:, None] * inv_freq], axis=-1)
    cos, sin = jnp.cos(ang), jnp.sin(ang)
    cosf = jnp.tile(jnp.concatenate([cos, cos], axis=-1), (1, ML_HEADS))
    sinf = jnp.tile(jnp.concatenate([-sin, sin], axis=-1), (1, ML_HEADS))
    cosf = jnp.concatenate([jnp.ones((n_ctx, ML_WIDTH), F32), cosf], axis=0)
    sinf = jnp.concatenate([jnp.zeros((n_ctx, ML_WIDTH), F32), sinf], axis=0)
    return cosf, sinf


def _lane_rows(rows):
    out = jnp.zeros((8, LANE), F32)
    for r, (off, vec) in enumerate(rows):
        out = out.at[r, off:off + vec.shape[0]].set(vec.astype(F32))
    return out


def kernel(x, c, ctx, c_ctx, ada_w, ada_b, norm_g, w_in, na_qk_g, na_rpb, ssm_conv_w, ssm_conv_b,
           ssm_dt_bias, ssm_a_log, ssm_d, ssm_norm_g, ml_gate_b, ml_norm_g, w_out,
           router_w, router_b, exp_w1, exp_b1, exp_w2, exp_b2):
    bsz, n_lat, d = x.shape
    n_ctx = ctx.shape[1]
    rows = n_lat // GRID_W
    depth = w_in.shape[0]
    assert bsz < 8 and n_ctx == TOK_TILE and n_lat % TOK_TILE == 0 and rows >= NA_KROWS

    h = jnp.concatenate([ctx, x], axis=1)
    cond = jnp.zeros((8, d), F32).at[:bsz].set(c).at[bsz].set(c_ctx)
    cosf, sinf = _rope_tables(n_ctx, n_lat)
    perm = _in_weight_perm()
    seg = np.arange(2 * LANE) // NA_HEAD_DIM
    bd = jnp.asarray(seg[:, None] == seg[None, :], BF16)

    n_exp = exp_w1.shape[0] * exp_w1.shape[1]
    w1_all = exp_w1.astype(BF16).reshape((n_exp,) + exp_w1.shape[2:])
    w2_all = exp_w2.astype(BF16).reshape((n_exp,) + exp_w2.shape[2:])
    b1_all = exp_b1.reshape(n_exp, 1, -1)
    b2_all = exp_b2.reshape(n_exp, 1, -1)

    out = None
    for layer in range(depth):
        ctx_out = layer < depth - 1
        mod = _ada(cond, ada_w[layer], ada_b[layer])
        mods = []
        for m in jnp.split(mod, 6, axis=-1):
            lat = m[:bsz]
            cx = jnp.broadcast_to(m[bsz][None], (bsz, d))
            mods.append(jnp.stack([cx, lat], axis=1).reshape(bsz, 2, 1, d))
        sh1, sc1, g1, sh2, sc2, g2 = mods

        w = jnp.where(perm[None, :] >= 0, w_in[layer][:, np.maximum(perm, 0)], 0.0).astype(BF16)
        qkg = jnp.concatenate([jnp.tile(na_qk_g[layer, 0] * NA_HEAD_DIM ** -0.5, NA_HEADS),
                               jnp.tile(na_qk_g[layer, 1], NA_HEADS)]).reshape(1, 2 * NA_WIDTH)
        big, small = _inproj(h, sh1, sc1, norm_g[layer, 0].reshape(1, d), w, qkg, bd, cosf, sinf)

        bias = _attention_bias(na_rpb[layer], rows)
        ya = _attention(big, bias, n_ctx, rows, ctx_out)

        conv_w = jnp.zeros((8, SSM_CONV_CH), F32).at[:SSM_CONV].set(ssm_conv_w[layer])
        ssm_par = jnp.stack([_lane_rows([(0, ssm_dt_bias[layer, dr]), (0, ssm_a_log[layer, dr]),
                                         (0, ssm_d[layer, dr])]) for dr in range(2)])
        yb = _ssd(big, small, conv_w, ssm_conv_b[layer].reshape(1, -1), ssm_par, n_ctx)

        gate_row = [jnp.concatenate([jnp.zeros((ML_HEADS,), F32), ml_gate_b[layer, dr, 0],
                                     ml_gate_b[layer, dr, 1]]) for dr in range(2)]
        ml_par = jnp.stack([_lane_rows([(0, gate_row[dr])]) for dr in range(2)])
        yc = _mlstm(big, small, ml_par, n_ctx)

        rw = jnp.zeros((LANE, d), F32).at[:N_EXPERTS].set(router_w[layer].T)
        rwh = rw.astype(BF16)
        rwl = (rw - rwh.astype(F32)).astype(BF16)
        rb = jnp.broadcast_to(router_b[layer].astype(F32)[:, None], (N_EXPERTS, TOK_TILE))
        hn, v, gates = _merge(ya, yb, yc, big, h, g1, sh2, sc2, norm_g[layer, 1].reshape(1, d),
                              ssm_norm_g[layer].reshape(1, -1), ml_norm_g[layer].reshape(1, -1), bd,
                              w_out[layer].astype(BF16), rwh, rwl, rb, n_ctx, ctx_out)

        n_lat_tok = bsz * n_lat
        f_lat = _moe(v, gates, w1_all, b1_all, w2_all, b2_all, layer, 0, n_lat_tok)
        f_ctx = (_moe(v, gates, w1_all, b1_all, w2_all, b2_all, layer, n_lat_tok, bsz * n_ctx)
                 if ctx_out else None)
        h = _resid(hn, g2, f_lat, f_ctx)
        out = h
    return out
```

```python
import functools
import math

import numpy as np
import jax
import jax.numpy as jnp
from jax import lax
from jax.experimental import pallas as pl
from jax.experimental.pallas import tpu as pltpu

F32 = jnp.float32
BF16 = jnp.bfloat16

GRID_W = 64
EPS = 1e-6
NA_HEADS = 8
NA_HEAD_DIM = 64
NA_WIN_H = 8
NA_WIN_W = 16
SSM_HEADS = 4
SSM_HEAD_DIM = 64
SSM_GROUPS = 2
SSM_STATE = 128
SSM_CONV = 5
ML_HEADS = 4
ML_HEAD_DIM = 64
ROPE_THETA = 10000.0
N_EXPERTS = 32
TOP_K = 4
SWIGLU_ALPHA = 1.702
SWIGLU_LIMIT = 7.0

NA_WIDTH = NA_HEADS * NA_HEAD_DIM
SSM_WIDTH = SSM_HEADS * SSM_HEAD_DIM
SSM_BC = SSM_GROUPS * SSM_STATE
SSM_CONV_CH = SSM_WIDTH + 2 * SSM_BC
ML_WIDTH = ML_HEADS * ML_HEAD_DIM
IN_SPLITS = (NA_WIDTH, NA_WIDTH, NA_WIDTH, SSM_CONV_CH, SSM_WIDTH, 2 * SSM_HEADS,
             ML_WIDTH, ML_WIDTH, ML_WIDTH, ML_WIDTH, 2 * ML_HEADS, 2 * ML_HEADS)
IN_OFFSETS = tuple(int(o) for o in np.cumsum((0,) + IN_SPLITS))

LANE = 128
CHUNK = 128
HALO = 16
TOK_TILE = 256
NA_ROWS = 2
NA_KROWS = NA_ROWS + NA_WIN_H
BIG_W = 3 * NA_WIDTH + SSM_CONV_CH + SSM_WIDTH + 4 * ML_WIDTH
SMALL_W = 2 * LANE
MOE_BLOCK = 1024
MOE_GROUP = 2
MOE_CAP = 160
SCAN_BATCH = 4
NEG = -1e30
VMEM_LIMIT = 48 * 1024 * 1024
MOE_VMEM_LIMIT = 56 * 1024 * 1024

COL_Q, COL_K, COL_V = 0, NA_WIDTH, 2 * NA_WIDTH
COL_XBC = 3 * NA_WIDTH
COL_Z = COL_XBC + SSM_CONV_CH
COL_MQ = COL_Z + SSM_WIDTH
COL_MK = COL_MQ + ML_WIDTH
COL_MV = COL_MK + ML_WIDTH
COL_MO = COL_MV + ML_WIDTH


def _cparams(sem):
    return pltpu.CompilerParams(dimension_semantics=sem, vmem_limit_bytes=VMEM_LIMIT)


def _sigmoid(x):
    return 1.0 / (1.0 + jnp.exp(-x))


def _softplus(x):
    return jnp.maximum(x, 0.0) + jnp.log1p(jnp.exp(-jnp.abs(x)))


def _dot(a, b):
    return jnp.dot(a, b, preferred_element_type=F32)


def _dot_nt(a, b):
    return lax.dot_general(a, b, (((1,), (1,)), ((), ())), preferred_element_type=F32)


def _dot_tn(a, b):
    return lax.dot_general(a, b, (((0,), (0,)), ((), ())), preferred_element_type=F32)


def _split3(x):
    x1 = x.astype(BF16)
    r1 = x - x1.astype(F32)
    x2 = r1.astype(BF16)
    x3 = (r1 - x2.astype(F32)).astype(BF16)
    return x1, x2, x3


def _split2(x):
    x1 = x.astype(BF16)
    return x1, (x - x1.astype(F32)).astype(BF16)


def _segment_mean_sq(x, bd):
    hi, lo = _split2(x * x)
    return (_dot(hi, bd) + _dot(lo, bd)) * (1.0 / NA_HEAD_DIM)


def _ada_kernel(c_ref, w_ref, b_ref, o_ref):
    c = c_ref[...]
    s = c * _sigmoid(c)
    o_ref[...] = jnp.dot(s, w_ref[...], preferred_element_type=F32,
                         precision=lax.Precision.HIGHEST) + b_ref[...]


def _ada(cond, w, b):
    n, d = cond.shape
    wcols = w.shape[1]
    bn = wcols // 4
    return pl.pallas_call(
        _ada_kernel,
        out_shape=jax.ShapeDtypeStruct((n, wcols), F32),
        grid=(4,),
        in_specs=[pl.BlockSpec((n, d), lambda j: (0, 0)),
                  pl.BlockSpec((d, bn), lambda j: (0, j)),
                  pl.BlockSpec((1, bn), lambda j: (0, j))],
        out_specs=pl.BlockSpec((n, bn), lambda j: (0, j)),
        compiler_params=_cparams(("arbitrary",)),
        name="ada",
    )(cond, w, b.reshape(1, wcols))


def _inproj_kernel(h_ref, sh_ref, sc_ref, g_ref, w_ref, qkg_ref, bd_ref, cos_ref, sin_ref,
                   big_ref, small_ref, kt_ref):
    x = h_ref[...]
    ms = jnp.mean(x * x, axis=-1, keepdims=True)
    u = x * lax.rsqrt(ms + EPS) * g_ref[...]
    u = u * (1.0 + sc_ref[...]) + sh_ref[...]
    ub = u.astype(BF16)
    bd = bd_ref[...]

    for c0 in range(0, 2 * NA_WIDTH, 2 * LANE):
        acc = _dot(ub, w_ref[:, c0:c0 + 2 * LANE])
        msq = _segment_mean_sq(acc, bd)
        y = acc * lax.rsqrt(msq + EPS) * qkg_ref[:, c0:c0 + 2 * LANE]
        big_ref[:, c0:c0 + 2 * LANE] = y.astype(BF16)
    for c0 in range(COL_V, COL_MQ, 2 * LANE):
        big_ref[:, c0:c0 + 2 * LANE] = _dot(ub, w_ref[:, c0:c0 + 2 * LANE]).astype(BF16)
    lane = lax.broadcasted_iota(jnp.int32, (x.shape[0], LANE), 1)
    first_half = (lane % ML_HEAD_DIM) < (ML_HEAD_DIM // 2)
    for base, scale in ((COL_MQ, ML_HEAD_DIM ** -0.5), (COL_MK, 1.0)):
        for j in range(ML_WIDTH // LANE):
            c0 = base + j * LANE
            acc = _dot(ub, w_ref[:, c0:c0 + LANE])
            swapped = jnp.where(first_half,
                                pltpu.roll(acc, LANE - ML_HEAD_DIM // 2, 1),
                                pltpu.roll(acc, ML_HEAD_DIM // 2, 1))
            y = acc * cos_ref[:, j * LANE:(j + 1) * LANE] + swapped * sin_ref[:, j * LANE:(j + 1) * LANE]
            big_ref[:, c0:c0 + LANE] = (y * scale).astype(BF16)
            if base == COL_MK:
                kt_ref[j * LANE:(j + 1) * LANE, :] = y.T.astype(BF16)
    for c0 in range(COL_MV, BIG_W, 2 * LANE):
        big_ref[:, c0:c0 + 2 * LANE] = _dot(ub, w_ref[:, c0:c0 + 2 * LANE]).astype(BF16)
    small_ref[...] = _dot(ub, w_ref[:, BIG_W:BIG_W + SMALL_W])


def _inproj(h, sh, sc, g, w, qkg, bd, cosf, sinf):
    bsz, s_len, d = h.shape
    tm = TOK_TILE
    mod_spec = pl.BlockSpec((None, None, 1, d), lambda b, i: (b, jnp.minimum(i, 1), 0, 0))
    full = lambda shape: pl.BlockSpec(shape, lambda b, i: (0,) * len(shape))
    return pl.pallas_call(
        _inproj_kernel,
        out_shape=(jax.ShapeDtypeStruct((bsz, s_len, BIG_W), BF16),
                   jax.ShapeDtypeStruct((bsz, s_len, SMALL_W), F32),
                   jax.ShapeDtypeStruct((bsz, ML_WIDTH, s_len), BF16)),
        grid=(bsz, s_len // tm),
        in_specs=[pl.BlockSpec((None, tm, d), lambda b, i: (b, i, 0)),
                  mod_spec, mod_spec, full((1, d)), full(w.shape), full(qkg.shape), full(bd.shape),
                  pl.BlockSpec((tm, ML_WIDTH), lambda b, i: (i, 0)),
                  pl.BlockSpec((tm, ML_WIDTH), lambda b, i: (i, 0))],
        out_specs=(pl.BlockSpec((None, tm, BIG_W), lambda b, i: (b, i, 0)),
                   pl.BlockSpec((None, tm, SMALL_W), lambda b, i: (b, i, 0)),
                   pl.BlockSpec((None, ML_WIDTH, tm), lambda b, i: (b, 0, i))),
        compiler_params=_cparams(("arbitrary", "arbitrary")),
        name="inproj",
    )(h, sh, sc, g, w, qkg, bd, cosf, sinf)


def _na_kernel(q_ref, k0, k1, k2, k3, k4, v0, v1, v2, v3, v4, kc_ref, vc_ref, bias_ref, o_ref):
    nq = q_ref.shape[0]
    lane = lax.broadcasted_iota(jnp.int32, (nq, LANE), 1)
    low = lane < NA_HEAD_DIM
    zero = jnp.zeros((), BF16)
    outs = []
    for p in range(NA_WIDTH // LANE):
        cs = slice(p * LANE, (p + 1) * LANE)
        qp = q_ref[:, cs]
        q01 = jnp.concatenate([jnp.where(low, qp, zero), jnp.where(low, zero, qp)], axis=0)
        kw = jnp.concatenate([r[:, cs] for r in (k0, k1, k2, k3, k4)], axis=0)
        vw = jnp.concatenate([r[:, cs] for r in (v0, v1, v2, v3, v4)], axis=0)
        vc = vc_ref[:, cs]
        vw = jnp.concatenate([vw, jnp.ones(vw.shape, BF16)], axis=1)
        vc = jnp.concatenate([vc, jnp.ones(vc.shape, BF16)], axis=1)
        s_w = _dot_nt(q01, kw) + bias_ref[p]
        s_c = _dot_nt(q01, kc_ref[:, cs])
        m = jnp.maximum(jnp.max(s_w, axis=-1, keepdims=True), jnp.max(s_c, axis=-1, keepdims=True))
        p_w = jnp.exp(s_w - m).astype(BF16)
        p_c = jnp.exp(s_c - m).astype(BF16)
        o01 = _dot(p_w, vw) + _dot(p_c, vc)
        o01 = o01[:, :LANE] / o01[:, LANE:]
        outs.append(jnp.where(low, o01[:nq], o01[nq:]).astype(o_ref.dtype))
    o_ref[...] = jnp.concatenate(outs, axis=1)


def _attention(big, bias, n_ctx, rows, ctx_out):
    bsz, s_len, _ = big.shape
    nq = NA_ROWS * GRID_W
    ncb = n_ctx // nq
    nrp = rows // NA_ROWS
    first = 0 if ctx_out else ncb
    n_win = NA_KROWS // NA_ROWS

    def lat(i):
        return jnp.maximum(i + first - ncb, 0)

    def kv_spec(j, col):
        return pl.BlockSpec(
            (None, nq, NA_WIDTH),
            lambda b, i: (b, ncb + jnp.clip(lat(i) - 2, 0, nrp - n_win) + j, col))

    def bias_type(b, i):
        step = i + first
        r = lat(i)
        t = jnp.where(r < 2, r, jnp.where(r >= nrp - 2, r - (nrp - 2) + 3, 2))
        return (jnp.where(step < ncb, 5, t), 0, 0, 0)

    return pl.pallas_call(
        _na_kernel,
        out_shape=jax.ShapeDtypeStruct((bsz, s_len, NA_WIDTH), BF16),
        grid=(bsz, ncb + nrp - first),
        in_specs=[pl.BlockSpec((None, nq, NA_WIDTH), lambda b, i: (b, i + first, 0))]
                 + [kv_spec(j, 1) for j in range(n_win)] + [kv_spec(j, 2) for j in range(n_win)]
                 + [pl.BlockSpec((None, n_ctx, NA_WIDTH), lambda b, i: (b, 0, 1)),
                    pl.BlockSpec((None, n_ctx, NA_WIDTH), lambda b, i: (b, 0, 2)),
                    pl.BlockSpec((None, NA_WIDTH // LANE, 2 * nq, NA_KROWS * GRID_W), bias_type)],
        out_specs=pl.BlockSpec((None, nq, NA_WIDTH), lambda b, i: (b, i + first, 0)),
        compiler_params=_cparams(("arbitrary", "arbitrary")),
        name="attention",
    )(big, *([big] * (2 * n_win + 2)), bias)


def _attention_bias(rpb, rows):
    nrp = rows // NA_ROWS
    n_win = NA_KROWS // NA_ROWS
    reps = np.array([0, 1, 2, nrp - 2, nrp - 1])
    qr = (NA_ROWS * reps)[:, None] + np.arange(NA_ROWS)[None, :]
    ws = NA_ROWS * np.clip(reps - 2, 0, nrp - n_win)
    kr = ws[:, None] + np.arange(NA_KROWS)[None, :]
    r0 = np.clip(qr - NA_WIN_H // 2, 0, rows - NA_WIN_H)
    ok_r = (kr[:, None, :] >= r0[:, :, None]) & (kr[:, None, :] < r0[:, :, None] + NA_WIN_H)
    dr = np.clip(kr[:, None, :] - qr[:, :, None] + (NA_WIN_H - 1), 0, 2 * NA_WIN_H - 2)
    qc = np.arange(GRID_W)
    c0 = np.clip(qc - NA_WIN_W // 2, 0, GRID_W - NA_WIN_W)
    kc = np.arange(GRID_W)
    ok_c = (kc[None, :] >= c0[:, None]) & (kc[None, :] < c0[:, None] + NA_WIN_W)
    dc = np.clip(kc[None, :] - qc[:, None] + (NA_WIN_W - 1), 0, 2 * NA_WIN_W - 2)
    sel_r = jnp.asarray(dr[..., None] == np.arange(2 * NA_WIN_H - 1), F32)
    sel_c = jnp.asarray(dc[..., None] == np.arange(2 * NA_WIN_W - 1), F32)
    hp = lax.Precision.HIGHEST
    g = jnp.einsum('hrc,taer->thaec', rpb.astype(F32), sel_r, precision=hp)
    g = jnp.einsum('thaec,qkc->thaqek', g, sel_c, precision=hp)
    ok = ok_r[:, None, :, None, :, None] & ok_c[None, None, None, :, None, :]
    g = jnp.where(ok, g, NEG)
    g = g.reshape(5, NA_HEADS * NA_ROWS * GRID_W, NA_KROWS * GRID_W)
    g = g.reshape(5, NA_WIDTH // LANE, 2 * NA_ROWS * GRID_W, NA_KROWS * GRID_W)
    masked = jnp.full((1,) + g.shape[1:], NEG, F32)
    return jnp.concatenate([g, masked], axis=0)


def _scan_chunk(d, t, ncc, nc):
    bwd = jnp.where(t < ncc, ncc - 1 - t, ncc + (nc - 1 - t))
    return jnp.where(d == 0, t, bwd)


def _precedes(d):
    ii = lax.broadcasted_iota(jnp.int32, (CHUNK, CHUNK), 0)
    jj = lax.broadcasted_iota(jnp.int32, (CHUNK, CHUNK), 1)
    return (jj - ii) * (1 - 2 * d) <= 0


def _cumsum_scan_order(prec, x):
    tri = jnp.where(prec, 1.0, 0.0).astype(BF16)
    x1, x2, x3 = _split3(x)
    return _dot(tri, x1) + _dot(tri, x2) + _dot(tri, x3)


def _ssd_kernel(x_ref, xp_ref, xn_ref, sm_ref, cw_ref, cb_ref, par_ref, y_ref, st_ref, cbuf_ref,
                *, ncc, nc):
    d = pl.program_id(1)
    t = pl.program_id(2)
    chunk = _scan_chunk(d, t, ncc, nc)

    @pl.when(t == 0)
    def _():
        st_ref[...] = jnp.zeros_like(st_ref)

    prev_ok = jnp.where(jnp.logical_and(chunk != 0, chunk != ncc), 1.0, 0.0)
    next_ok = jnp.where(jnp.logical_and(chunk != ncc - 1, chunk != nc - 1), 1.0, 0.0)
    prec = _precedes(d)
    rep = SSM_HEADS // SSM_GROUPS
    nb = x_ref.shape[0]
    for bb in range(nb):
        cbuf_ref[bb, 0:HALO, :] = xp_ref[bb].astype(F32) * prev_ok
        cbuf_ref[bb, HALO:HALO + CHUNK, :] = x_ref[bb].astype(F32)
        cbuf_ref[bb, HALO + CHUNK:, :] = xn_ref[bb].astype(F32) * next_ok
    taps = [[cbuf_ref[bb, HALO - SSM_CONV // 2 + k:HALO - SSM_CONV // 2 + k + CHUNK, :]
             for k in range(SSM_CONV)] for bb in range(nb)]
    gates_in = [sm_ref[bb] for bb in range(nb)]
    states = [[st_ref[bb, h] for h in range(SSM_HEADS)] for bb in range(nb)]
    stores = []
    for bb in range(nb):
        acc = jnp.broadcast_to(cb_ref[...], (CHUNK, SSM_CONV_CH))
        for k in range(SSM_CONV):
            acc = acc + taps[bb][k] * cw_ref[k:k + 1, :]
        xc = acc * _sigmoid(acc)
        xs = xc[:, :SSM_WIDTH]
        bm = xc[:, SSM_WIDTH:SSM_WIDTH + SSM_BC]
        cm = xc[:, SSM_WIDTH + SSM_BC:]

        dt = _softplus(gates_in[bb] + par_ref[0:1, :])
        la = dt * (-jnp.exp(par_ref[1:2, :]))
        acum = _cumsum_scan_order(prec, la)
        acum_t = acum.T
        total = jnp.sum(la, axis=0, keepdims=True)

        for g in range(SSM_GROUPS):
            cg = cm[:, g * SSM_STATE:(g + 1) * SSM_STATE].astype(BF16)
            bg = bm[:, g * SSM_STATE:(g + 1) * SSM_STATE].astype(BF16)
            cb = _dot_nt(cg, bg)
            for hh in range(rep):
                h = g * rep + hh
                a_col = acum[:, h:h + 1]
                a_row = acum_t[h:h + 1, :]
                tot = total[:, h:h + 1]
                x_h = xs[:, h * SSM_HEAD_DIM:(h + 1) * SSM_HEAD_DIM]
                xdt = x_h * dt[:, h:h + 1]
                lmat = jnp.exp(jnp.where(prec, a_col - a_row, -jnp.inf))
                h_in = states[bb][h]
                y = _dot((cb * lmat).astype(BF16), xdt.astype(BF16))
                y = y + _dot(cg, h_in.astype(BF16)) * jnp.exp(a_col)
                y = y + x_h * par_ref[2:3, h:h + 1]
                xw = (xdt * jnp.exp(tot - a_col)).astype(BF16)
                stores.append((bb, h, y, jnp.exp(tot) * h_in + _dot_tn(bg, xw)))

    for bb, h, y, h_new in stores:
        y_ref[bb, :, h * SSM_HEAD_DIM:(h + 1) * SSM_HEAD_DIM] = y
        st_ref[bb, h] = h_new


def _scan_batch(bsz):
    return math.gcd(bsz, SCAN_BATCH)


def _ssd(big, small, conv_w, conv_b, par, n_ctx):
    bsz, s_len, _ = big.shape
    ncc, nc = n_ctx // CHUNK, s_len // CHUNK
    per = CHUNK // HALO
    nb = _scan_batch(bsz)
    ch = functools.partial(_scan_chunk, ncc=ncc, nc=nc)
    cblk = COL_XBC // SSM_CONV_CH
    return pl.pallas_call(
        functools.partial(_ssd_kernel, ncc=ncc, nc=nc),
        out_shape=jax.ShapeDtypeStruct((2, bsz, s_len, SSM_WIDTH), F32),
        grid=(bsz // nb, 2, nc),
        in_specs=[
            pl.BlockSpec((nb, CHUNK, SSM_CONV_CH), lambda b, d, t: (b, ch(d, t), cblk)),
            pl.BlockSpec((nb, HALO, SSM_CONV_CH),
                         lambda b, d, t: (b, jnp.maximum(ch(d, t) * per - 1, 0), cblk)),
            pl.BlockSpec((nb, HALO, SSM_CONV_CH),
                         lambda b, d, t: (b, jnp.minimum((ch(d, t) + 1) * per, nc * per - 1), cblk)),
            pl.BlockSpec((nb, CHUNK, LANE), lambda b, d, t: (b, ch(d, t), d)),
            pl.BlockSpec(conv_w.shape, lambda b, d, t: (0, 0)),
            pl.BlockSpec(conv_b.shape, lambda b, d, t: (0, 0)),
            pl.BlockSpec((None, 8, LANE), lambda b, d, t: (d, 0, 0)),
        ],
        out_specs=pl.BlockSpec((None, nb, CHUNK, SSM_WIDTH), lambda b, d, t: (d, b, ch(d, t), 0)),
        scratch_shapes=[pltpu.VMEM((nb, SSM_HEADS, SSM_STATE, SSM_HEAD_DIM), F32),
                        pltpu.VMEM((nb, CHUNK + 2 * HALO, SSM_CONV_CH), F32)],
        compiler_params=_cparams(("arbitrary", "arbitrary", "arbitrary")),
        name="ssd",
    )(big, big, big, small, conv_w, conv_b, par)


def _mlstm_kernel(q_ref, k_ref, v_ref, kt_ref, sm_ref, par_ref, y_ref, st_ref, m_ref, *, ncc, nc):
    d = pl.program_id(1)
    t = pl.program_id(2)

    @pl.when(t == 0)
    def _():
        st_ref[...] = jnp.zeros_like(st_ref)
        m_ref[...] = jnp.zeros_like(m_ref)

    prec = _precedes(d)
    nb = q_ref.shape[0]
    dh = ML_HEAD_DIM
    loaded = [(sm_ref[bb], q_ref[bb], k_ref[bb], v_ref[bb], kt_ref[bb],
               [(m_ref[bb, h][0:1, :], st_ref[bb, h]) for h in range(ML_HEADS)])
              for bb in range(nb)]
    sel_row = lax.broadcasted_iota(jnp.int32, (LANE, LANE), 0)
    low = lax.broadcasted_iota(jnp.int32, (CHUNK, LANE), 1) < dh
    ones_tile = jnp.ones((CHUNK, LANE), BF16)
    twice = lambda x: jnp.concatenate([x, x], axis=1)
    stores = []
    for bb in range(nb):
        sm, q_all, k_all, v_all, kt_all, states = loaded[bb]
        pre = sm + par_ref[0:1, :]
        lf = -_softplus(-pre)
        bcum = _cumsum_scan_order(prec, lf)
        bcum_t = bcum.T
        li_t = pre.T
        gsum = jnp.broadcast_to(jnp.sum(lf, axis=0, keepdims=True), (8, LANE))
        bcum_p, gsum_p = _split3(bcum), _split3(gsum)

        def lane_rep(pieces, c):
            sel = jnp.where(sel_row == c, 1.0, 0.0).astype(BF16)
            return _dot(pieces[0], sel) + _dot(pieces[1], sel) + _dot(pieces[2], sel)

        tiles = []
        for h in range(ML_HEADS):
            hs = slice(h * dh, (h + 1) * dh)
            ps = slice((h // 2) * LANE, (h // 2 + 1) * LANE)
            ci, cf = ML_HEADS + h, 2 * ML_HEADS + h
            b_rep = lane_rep(bcum_p, cf)
            gt = lane_rep(gsum_p, cf)[0:1, :]
            b_row = bcum_t[cf:cf + 1, :]
            li_row = li_t[ci:ci + 1, :]
            qh = q_all[:, hs]
            kh = k_all[:, hs]
            v_aug = jnp.concatenate([v_all[:, ps], ones_tile], axis=1)
            m_in, st_in = states[h]

            log_d = jnp.where(prec, b_rep - b_row + li_row, -jnp.inf)
            log_inter = b_rep + m_in
            m_i = jnp.maximum(jnp.max(log_d, axis=-1, keepdims=True), log_inter)
            s = _dot_nt(qh, kh) * jnp.exp(log_d - m_i)
            inter = jnp.exp(log_inter - m_i)
            numden = _dot(s.astype(BF16), v_aug) + _dot(qh, st_in.astype(BF16)) * twice(inter)
            tiles.append(numden[:, :LANE] / jnp.maximum(jnp.abs(numden[:, LANE:]), jnp.exp(-m_i)))

            w_row = gt - b_row + li_row
            m_loc = jnp.max(w_row, axis=-1, keepdims=True)
            ket = (kt_all[hs, :].astype(F32) * jnp.exp(w_row - m_loc)).astype(BF16)
            m_new = jnp.maximum(gt + m_in, m_loc)
            a_old = jnp.exp(gt + m_in - m_new)
            a_new = jnp.exp(m_loc - m_new)
            st_new = twice(a_old) * st_in + twice(a_new) * _dot(ket, v_aug)
            stores.append((bb, h, st_new, m_new))
        for p in range(ML_HEADS // 2):
            y_ref[bb, :, p * LANE:(p + 1) * LANE] = jnp.where(low, tiles[2 * p], tiles[2 * p + 1])

    for bb, h, st_new, m_new in stores:
        st_ref[bb, h] = st_new
        m_ref[bb, h] = jnp.broadcast_to(m_new, m_ref.shape[2:])


def _mlstm(big, kt, small, par, n_ctx):
    bsz, s_len, _ = big.shape
    ncc, nc = n_ctx // CHUNK, s_len // CHUNK
    nb = _scan_batch(bsz)
    ch = functools.partial(_scan_chunk, ncc=ncc, nc=nc)

    def col_spec(col):
        return pl.BlockSpec((nb, CHUNK, ML_WIDTH), lambda b, d, t: (b, ch(d, t), col // ML_WIDTH))

    return pl.pallas_call(
        functools.partial(_mlstm_kernel, ncc=ncc, nc=nc),
        out_shape=jax.ShapeDtypeStruct((2, bsz, s_len, ML_WIDTH), F32),
        grid=(bsz // nb, 2, nc),
        in_specs=[col_spec(COL_MQ), col_spec(COL_MK), col_spec(COL_MV),
                  pl.BlockSpec((nb, ML_WIDTH, CHUNK), lambda b, d, t: (b, 0, ch(d, t))),
                  pl.BlockSpec((nb, CHUNK, LANE), lambda b, d, t: (b, ch(d, t), d)),
                  pl.BlockSpec((None, 8, LANE), lambda b, d, t: (d, 0, 0))],
        out_specs=pl.BlockSpec((None, nb, CHUNK, ML_WIDTH), lambda b, d, t: (d, b, ch(d, t), 0)),
        scratch_shapes=[pltpu.VMEM((nb, ML_HEADS, ML_HEAD_DIM, 2 * LANE), F32),
                        pltpu.VMEM((nb, ML_HEADS, 8, LANE), F32)],
        compiler_params=_cparams(("arbitrary", "arbitrary", "arbitrary")),
        name="mlstm",
    )(big, big, big, kt, small, par)


def _merge_kernel(ya_ref, yb_ref, z_ref, yc_ref, o_ref, h_ref, g1_ref, sh_ref, sc_ref, ng_ref,
                  sg_ref, mg_ref, bd_ref, wo_ref, rwh_ref, rwl_ref, rb_ref,
                  hout_ref, v_ref, gates_ref):
    z = z_ref[...].astype(F32)
    yb = (yb_ref[0] + yb_ref[1]) * (z * _sigmoid(z))
    yb = yb * lax.rsqrt(jnp.mean(yb * yb, axis=-1, keepdims=True) + EPS) * sg_ref[...]
    yc = _sigmoid(o_ref[...].astype(F32)) * (yc_ref[0] + yc_ref[1])
    yc = yc * lax.rsqrt(_segment_mean_sq(yc, bd_ref[...]) + EPS) * mg_ref[...]
    y = (_dot(ya_ref[...], wo_ref[0:NA_WIDTH, :])
         + _dot(yb.astype(BF16), wo_ref[NA_WIDTH:NA_WIDTH + SSM_WIDTH, :])
         + _dot(yc.astype(BF16), wo_ref[NA_WIDTH + SSM_WIDTH:, :]))
    hn = h_ref[...] + g1_ref[...] * y
    hout_ref[...] = hn
    v = hn * lax.rsqrt(jnp.mean(hn * hn, axis=-1, keepdims=True) + EPS) * ng_ref[...]
    v = v * (1.0 + sc_ref[...]) + sh_ref[...]
    v_hi, v_lo = _split2(v)
    v_ref[...] = v_hi
    rwh = rwh_ref[...]
    logits = (_dot_nt(rwh, v_hi) + _dot_nt(rwh, v_lo) + _dot_nt(rwl_ref[...], v_hi))[:N_EXPERTS] + rb_ref[...]

    row = lax.broadcasted_iota(jnp.int32, logits.shape, 0).astype(F32)
    work = logits
    top = None
    gates = jnp.zeros_like(logits)
    den = jnp.zeros((1, logits.shape[1]), F32)
    for _ in range(TOP_K):
        mx = jnp.max(work, axis=0, keepdims=True)
        idx = jnp.min(jnp.where(work == mx, row, float(N_EXPERTS)), axis=0, keepdims=True)
        hit = row == idx
        top = mx if top is None else top
        e = jnp.exp(mx - top)
        gates = gates + jnp.where(hit, e, 0.0)
        den = den + e
        work = jnp.where(hit, -jnp.inf, work)
    gates_ref[...] = jnp.concatenate(
        [gates / den, jnp.zeros((LANE - N_EXPERTS, logits.shape[1]), F32)], axis=0)


def _merge(ya, yb, yc, big, h, g1, sh2, sc2, ng, sg, mg, bd, wo, rwh, rwl, rb, n_ctx, ctx_out):
    bsz, s_len, d = h.shape
    tm = TOK_TILE
    first = 0 if ctx_out else n_ctx // tm
    n_out = s_len - first * tm
    mod_spec = pl.BlockSpec((None, None, 1, d), lambda b, i: (b, jnp.minimum(i + first, 1), 0, 0))
    full = lambda shape: pl.BlockSpec(shape, lambda b, i: (0,) * len(shape))
    pair = lambda w: pl.BlockSpec((2, None, tm, w), lambda b, i: (0, b, i + first, 0))
    nl = (s_len - n_ctx) // tm

    def flat_row(b, i):
        if ctx_out:
            return jnp.where(i == 0, bsz * nl + b, (i - 1) * bsz + b)
        return i * bsz + b

    flat_tok = lambda w: pl.BlockSpec((tm, w), lambda b, i: (flat_row(b, i), 0))
    return pl.pallas_call(
        _merge_kernel,
        out_shape=(jax.ShapeDtypeStruct((bsz, n_out, d), F32),
                   jax.ShapeDtypeStruct((bsz * n_out, d), BF16),
                   jax.ShapeDtypeStruct((LANE, bsz * n_out), F32)),
        grid=(bsz, n_out // tm),
        in_specs=[pl.BlockSpec((None, tm, NA_WIDTH), lambda b, i: (b, i + first, 0)),
                  pair(SSM_WIDTH),
                  pl.BlockSpec((None, tm, SSM_WIDTH), lambda b, i: (b, i + first, COL_Z // SSM_WIDTH)),
                  pair(ML_WIDTH),
                  pl.BlockSpec((None, tm, ML_WIDTH), lambda b, i: (b, i + first, COL_MO // ML_WIDTH)),
                  pl.BlockSpec((None, tm, d), lambda b, i: (b, i + first, 0)),
                  mod_spec, mod_spec, mod_spec,
                  full((1, d)), full((1, SSM_WIDTH)), full((1, ML_WIDTH)), full(bd.shape),
                  full(wo.shape), full(rwh.shape), full(rwl.shape), full(rb.shape)],
        out_specs=(pl.BlockSpec((None, tm, d), lambda b, i: (b, i, 0)), flat_tok(d),
                   pl.BlockSpec((LANE, tm), lambda b, i: (0, flat_row(b, i)))),
        compiler_params=_cparams(("arbitrary", "arbitrary")),
        name="merge",
    )(ya, yb, big, yc, big, h, g1, sh2, sc2, ng, sg, mg, bd, wo, rwh, rwl, rb)


def _moe_kernel(x_ref, gates_ref, w1_ref, b1_ref, w2_ref, b2_ref, f_ref, rank_ref, *, nblk):
    e = pl.program_id(1)
    tb = x_ref.shape[0] // nblk
    d_ff = w2_ref.shape[0]

    @pl.when(e == 0)
    def _():
        f_ref[...] = jnp.zeros_like(f_ref)
        ii = lax.broadcasted_iota(jnp.int32, (tb, tb), 0)
        jj = lax.broadcasted_iota(jnp.int32, (tb, tb), 1)
        tri = jnp.where(ii <= jj, 1.0, 0.0).astype(BF16)
        for k in range(nblk):
            picked = gates_ref[:, k * tb:(k + 1) * tb] > 0.0
            incl = _dot(jnp.where(picked, 1.0, 0.0).astype(BF16), tri)
            rank_ref[k] = jnp.where(picked, incl - 1.0, -1.0)

    def compact(k, s):
        slot = rank_ref[k, pl.ds(e, 1), :]
        gate = gates_ref[pl.ds(e, 1), k * tb:(k + 1) * tb]
        row = (lax.broadcasted_iota(jnp.int32, (MOE_CAP, tb), 0) + s * MOE_CAP).astype(F32)
        hit = slot == row
        onehot = jnp.where(hit, 1.0, 0.0).astype(BF16)
        xe = _dot(onehot, x_ref[k * tb:(k + 1) * tb, :]).astype(BF16)
        w = jnp.sum(jnp.where(hit, gate, 0.0), axis=-1, keepdims=True)
        return onehot, xe, w

    def expert(xe):
        y = jnp.broadcast_to(b2_ref[...], (xe.shape[0], w2_ref.shape[1]))
        for c0 in range(0, d_ff, 2 * LANE):
            x_glu = _dot(xe, w1_ref[:, c0:c0 + 2 * LANE]) + b1_ref[:, c0:c0 + 2 * LANE]
            x_lin = (_dot(xe, w1_ref[:, d_ff + c0:d_ff + c0 + 2 * LANE])
                     + b1_ref[:, d_ff + c0:d_ff + c0 + 2 * LANE])
            x_glu = jnp.minimum(x_glu, SWIGLU_LIMIT)
            x_lin = jnp.clip(x_lin, -SWIGLU_LIMIT, SWIGLU_LIMIT)
            act = x_glu * _sigmoid(SWIGLU_ALPHA * x_glu) * (x_lin + 1.0)
            y = y + _dot(act.astype(BF16), w2_ref[c0:c0 + 2 * LANE, :])
        return y

    def scatter(k, onehot, yw):
        f_ref[k * tb:(k + 1) * tb, :] += _dot_tn(onehot, yw.astype(BF16))

    first = [compact(k, 0) for k in range(nblk)]
    y = expert(jnp.concatenate([xe for _, xe, _ in first], axis=0))
    for k, (onehot, _, w) in enumerate(first):
        scatter(k, onehot, y[k * MOE_CAP:(k + 1) * MOE_CAP] * w)

    for k in range(nblk):
        count = jnp.max(rank_ref[k, pl.ds(e, 1), :]).astype(jnp.int32) + 1

        def overflow(s, carry, k=k):
            onehot, xe, w = compact(k, s)
            scatter(k, onehot, expert(xe) * w)
            return carry

        lax.fori_loop(1, (count + MOE_CAP - 1) // MOE_CAP, overflow, 0)


def _moe(x, gates, w1, b1, w2, b2, layer, first_tok, n_tok):
    d = x.shape[1]
    nblk = MOE_GROUP if n_tok % (MOE_GROUP * MOE_BLOCK) == 0 else 1
    tb = math.gcd(n_tok, MOE_BLOCK)
    rows = nblk * tb
    d_ff = w2.shape[1]
    off = first_tok // rows
    assert first_tok % rows == 0 and n_tok % rows == 0
    expert = lambda i, e: (layer * N_EXPERTS + e, 0, 0)
    return pl.pallas_call(
        functools.partial(_moe_kernel, nblk=nblk),
        out_shape=jax.ShapeDtypeStruct((n_tok, d), F32),
        grid=(n_tok // rows, N_EXPERTS),
        in_specs=[pl.BlockSpec((rows, d), lambda i, e: (i + off, 0)),
                  pl.BlockSpec((LANE, rows), lambda i, e: (0, i + off)),
                  pl.BlockSpec((None, d, 2 * d_ff), expert),
                  pl.BlockSpec((None, 1, 2 * d_ff), expert),
                  pl.BlockSpec((None, d_ff, d), expert),
                  pl.BlockSpec((None, 1, d), expert)],
        out_specs=pl.BlockSpec((rows, d), lambda i, e: (i, 0)),
        scratch_shapes=[pltpu.VMEM((nblk, LANE, tb), F32)],
        compiler_params=pltpu.CompilerParams(dimension_semantics=("arbitrary", "arbitrary"),
                                             vmem_limit_bytes=MOE_VMEM_LIMIT),
        name="moe",
    )(x, gates, w1, b1, w2, b2)


def _resid_kernel(h_ref, g_ref, fl_ref, *rest, first):
    if first == 0:
        fc_ref, o_ref = rest
        f = jnp.where(pl.program_id(1) == 0, fc_ref[...], fl_ref[...])
    else:
        (o_ref,) = rest
        f = fl_ref[...]
    o_ref[...] = h_ref[...] + g_ref[...] * f


def _resid(h, g2, f_lat, f_ctx):
    bsz, n_tok, d = h.shape
    tm = TOK_TILE
    first = 0 if f_ctx is not None else 1
    in_specs = [pl.BlockSpec((None, tm, d), lambda b, i: (b, i, 0)),
                pl.BlockSpec((None, None, 1, d), lambda b, i: (b, jnp.minimum(i + first, 1), 0, 0)),
                pl.BlockSpec((tm, d), lambda b, i: (jnp.maximum(i + first - 1, 0) * bsz + b, 0))]
    args = [h, g2, f_lat]
    if f_ctx is not None:
        in_specs.append(pl.BlockSpec((tm, d), lambda b, i: (b, 0)))
        args.append(f_ctx)
    return pl.pallas_call(
        functools.partial(_resid_kernel, first=first),
        out_shape=jax.ShapeDtypeStruct((bsz, n_tok, d), F32),
        grid=(bsz, n_tok // tm),
        in_specs=in_specs,
        out_specs=pl.BlockSpec((None, tm, d), lambda b, i: (b, i, 0)),
        compiler_params=_cparams(("arbitrary", "arbitrary")),
        name="resid",
    )(*args)


def _in_weight_perm():
    o = IN_OFFSETS
    big = np.concatenate([np.arange(o[0], o[5]), np.arange(o[6], o[10])])
    small = np.full((2, LANE), -1, np.int64)
    for d in range(2):
        for h in range(SSM_HEADS):
            small[d, h] = o[5] + d * SSM_HEADS + h
        for h in range(ML_HEADS):
            small[d, ML_HEADS + h] = o[10] + d * ML_HEADS + h
            small[d, 2 * ML_HEADS + h] = o[11] + d * ML_HEADS + h
    return np.concatenate([big, small.reshape(-1)])


def _rope_tables(n_ctx, n_lat):
    pos = np.arange(n_lat)
    row = (pos // GRID_W).astype(np.float32)
    col = (pos % GRID_W).astype(np.float32)
    n_freq = ML_HEAD_DIM // 4
    inv_freq = jnp.asarray(ROPE_THETA, F32) ** (-jnp.arange(n_freq, dtype=F32) / n_freq)
    ang = jnp.concatenate([jnp.asarray(row)[:, None] * inv_freq, jnp.asarray(col)[:, None] * inv_freq], axis=-1)
    cos, sin = jnp.cos(ang), jnp.sin(ang)
    cosf = jnp.tile(jnp.concatenate([cos, cos], axis=-1), (1, ML_HEADS))
    sinf = jnp.tile(jnp.concatenate([-sin, sin], axis=-1), (1, ML_HEADS))
    cosf = jnp.concatenate([jnp.ones((n_ctx, ML_WIDTH), F32), cosf], axis=0)
    sinf = jnp.concatenate([jnp.zeros((n_ctx, ML_WIDTH), F32), sinf], axis=0)
    return cosf, sinf


def _lane_rows(rows):
    out = jnp.zeros((8, LANE), F32)
    for r, (off, vec) in enumerate(rows):
        out = out.at[r, off:off + vec.shape[0]].set(vec.astype(F32))
    return out


def kernel(x, c, ctx, c_ctx, ada_w, ada_b, norm_g, w_in, na_qk_g, na_rpb, ssm_conv_w, ssm_conv_b,
           ssm_dt_bias, ssm_a_log, ssm_d, ssm_norm_g, ml_gate_b, ml_norm_g, w_out,
           router_w, router_b, exp_w1, exp_b1, exp_w2, exp_b2):
    bsz, n_lat, d = x.shape
    n_ctx = ctx.shape[1]
    rows = n_lat // GRID_W
    depth = w_in.shape[0]
    assert bsz < 8 and n_ctx == TOK_TILE and n_lat % TOK_TILE == 0 and rows >= NA_KROWS

    h = jnp.concatenate([ctx, x], axis=1)
    cond = jnp.zeros((8, d), F32).at[:bsz].set(c).at[bsz].set(c_ctx)
    cosf, sinf = _rope_tables(n_ctx, n_lat)
    perm = _in_weight_perm()
    seg = np.arange(2 * LANE) // NA_HEAD_DIM
    bd = jnp.asarray(seg[:, None] == seg[None, :], BF16)

    n_exp = exp_w1.shape[0] * exp_w1.shape[1]
    w1_all = exp_w1.astype(BF16).reshape((n_exp,) + exp_w1.shape[2:])
    w2_all = exp_w2.astype(BF16).reshape((n_exp,) + exp_w2.shape[2:])
    b1_all = exp_b1.reshape(n_exp, 1, -1)
    b2_all = exp_b2.reshape(n_exp, 1, -1)

    out = None
    for layer in range(depth):
        ctx_out = layer < depth - 1
        mod = _ada(cond, ada_w[layer], ada_b[layer])
        mods = []
        for m in jnp.split(mod, 6, axis=-1):
            lat = m[:bsz]
            cx = jnp.broadcast_to(m[bsz][None], (bsz, d))
            mods.append(jnp.stack([cx, lat], axis=1).reshape(bsz, 2, 1, d))
        sh1, sc1, g1, sh2, sc2, g2 = mods

        w = jnp.where(perm[None, :] >= 0, w_in[layer][:, np.maximum(perm, 0)], 0.0).astype(BF16)
        qkg = jnp.concatenate([jnp.tile(na_qk_g[layer, 0] * NA_HEAD_DIM ** -0.5, NA_HEADS),
                               jnp.tile(na_qk_g[layer, 1], NA_HEADS)]).reshape(1, 2 * NA_WIDTH)
        big, small, kt = _inproj(h, sh1, sc1, norm_g[layer, 0].reshape(1, d), w, qkg, bd, cosf, sinf)

        bias = _attention_bias(na_rpb[layer], rows)
        ya = _attention(big, bias, n_ctx, rows, ctx_out)

        conv_w = jnp.zeros((8, SSM_CONV_CH), F32).at[:SSM_CONV].set(ssm_conv_w[layer])
        ssm_par = jnp.stack([_lane_rows([(0, ssm_dt_bias[layer, dr]), (0, ssm_a_log[layer, dr]),
                                         (0, ssm_d[layer, dr])]) for dr in range(2)])
        yb = _ssd(big, small, conv_w, ssm_conv_b[layer].reshape(1, -1), ssm_par, n_ctx)

        gate_row = [jnp.concatenate([jnp.zeros((ML_HEADS,), F32), ml_gate_b[layer, dr, 0],
                                     ml_gate_b[layer, dr, 1]]) for dr in range(2)]
        ml_par = jnp.stack([_lane_rows([(0, gate_row[dr])]) for dr in range(2)])
        yc = _mlstm(big, kt, small, ml_par, n_ctx)

        rw = jnp.zeros((LANE, d), F32).at[:N_EXPERTS].set(router_w[layer].T)
        rwh = rw.astype(BF16)
        rwl = (rw - rwh.astype(F32)).astype(BF16)
        rb = jnp.broadcast_to(router_b[layer].astype(F32)[:, None], (N_EXPERTS, TOK_TILE))
        hn, v, gates = _merge(ya, yb, yc, big, h, g1, sh2, sc2, norm_g[layer, 1].reshape(1, d),
                              ssm_norm_g[layer].reshape(1, -1), ml_norm_g[layer].reshape(1, -1), bd,
                              w_out[layer].astype(BF16), rwh, rwl, rb, n_ctx, ctx_out)

        n_lat_tok = bsz * n_lat
        f_lat = _moe(v, gates, w1_all, b1_all, w2_all, b2_all, layer, 0, n_lat_tok)
        f_ctx = (_moe(v, gates, w1_all, b1_all, w2_all, b2_all, layer, n_lat_tok, bsz * n_ctx)
                 if ctx_out else None)
        h = _resid(hn, g2, f_lat, f_ctx)
        out = h
    return out
```

```python
import functools
import math

import numpy as np
import jax
import jax.numpy as jnp
from jax import lax
from jax.experimental import pallas as pl
from jax.experimental.pallas import tpu as pltpu

F32 = jnp.float32
BF16 = jnp.bfloat16

GRID_W = 64
EPS = 1e-6
NA_HEADS = 8
NA_HEAD_DIM = 64
NA_WIN_H = 8
NA_WIN_W = 16
SSM_HEADS = 4
SSM_HEAD_DIM = 64
SSM_GROUPS = 2
SSM_STATE = 128
SSM_CONV = 5
ML_HEADS = 4
ML_HEAD_DIM = 64
ROPE_THETA = 10000.0
N_EXPERTS = 32
TOP_K = 4
SWIGLU_ALPHA = 1.702
SWIGLU_LIMIT = 7.0

NA_WIDTH = NA_HEADS * NA_HEAD_DIM
SSM_WIDTH = SSM_HEADS * SSM_HEAD_DIM
SSM_BC = SSM_GROUPS * SSM_STATE
SSM_CONV_CH = SSM_WIDTH + 2 * SSM_BC
ML_WIDTH = ML_HEADS * ML_HEAD_DIM
IN_SPLITS = (NA_WIDTH, NA_WIDTH, NA_WIDTH, SSM_CONV_CH, SSM_WIDTH, 2 * SSM_HEADS,
             ML_WIDTH, ML_WIDTH, ML_WIDTH, ML_WIDTH, 2 * ML_HEADS, 2 * ML_HEADS)
IN_OFFSETS = tuple(int(o) for o in np.cumsum((0,) + IN_SPLITS))

LANE = 128
CHUNK = 128
HALO = 16
TOK_TILE = 256
NA_ROWS = 2
NA_KROWS = NA_ROWS + NA_WIN_H
BIG_W = 3 * NA_WIDTH + SSM_CONV_CH + SSM_WIDTH + 4 * ML_WIDTH
SMALL_W = 2 * LANE
PLAIN_COLS = 768
MOE_BLOCK = 1024
MOE_GROUP = 2
MOE_CAP = 160
SCAN_BATCH = 4
NEG = -1e30
VMEM_LIMIT = 48 * 1024 * 1024
MOE_VMEM_LIMIT = 56 * 1024 * 1024

COL_Q, COL_K, COL_V = 0, NA_WIDTH, 2 * NA_WIDTH
COL_XBC = 3 * NA_WIDTH
COL_Z = COL_XBC + SSM_CONV_CH
COL_MQ = COL_Z + SSM_WIDTH
COL_MK = COL_MQ + ML_WIDTH
COL_MV = COL_MK + ML_WIDTH
COL_MO = COL_MV + ML_WIDTH


def _cparams(sem):
    return pltpu.CompilerParams(dimension_semantics=sem, vmem_limit_bytes=VMEM_LIMIT)


def _sigmoid(x):
    return 1.0 / (1.0 + jnp.exp(-x))


def _softplus(x):
    return jnp.maximum(x, 0.0) + jnp.log1p(jnp.exp(-jnp.abs(x)))


def _dot(a, b):
    return jnp.dot(a, b, preferred_element_type=F32)


def _dot_nt(a, b):
    return lax.dot_general(a, b, (((1,), (1,)), ((), ())), preferred_element_type=F32)


def _dot_tn(a, b):
    return lax.dot_general(a, b, (((0,), (0,)), ((), ())), preferred_element_type=F32)


def _split3(x):
    x1 = x.astype(BF16)
    r1 = x - x1.astype(F32)
    x2 = r1.astype(BF16)
    x3 = (r1 - x2.astype(F32)).astype(BF16)
    return x1, x2, x3


def _split2(x):
    x1 = x.astype(BF16)
    return x1, (x - x1.astype(F32)).astype(BF16)


def _segment_mean_sq(x, bd):
    hi, lo = _split2(x * x)
    return (_dot(hi, bd) + _dot(lo, bd)) * (1.0 / NA_HEAD_DIM)


def _ada_kernel(c_ref, w_ref, b_ref, o_ref):
    c = c_ref[...]
    s = c * _sigmoid(c)
    o_ref[...] = jnp.dot(s, w_ref[...], preferred_element_type=F32,
                         precision=lax.Precision.HIGHEST) + b_ref[...]


def _ada(cond, w, b):
    n, d = cond.shape
    wcols = w.shape[1]
    bn = wcols // 4
    return pl.pallas_call(
        _ada_kernel,
        out_shape=jax.ShapeDtypeStruct((n, wcols), F32),
        grid=(4,),
        in_specs=[pl.BlockSpec((n, d), lambda j: (0, 0)),
                  pl.BlockSpec((d, bn), lambda j: (0, j)),
                  pl.BlockSpec((1, bn), lambda j: (0, j))],
        out_specs=pl.BlockSpec((n, bn), lambda j: (0, j)),
        compiler_params=_cparams(("arbitrary",)),
        name="ada",
    )(cond, w, b.reshape(1, wcols))


def _inproj_kernel(h_ref, sh_ref, sc_ref, g_ref, w_ref, qkg_ref, bd_ref, cos_ref, sin_ref,
                   big_ref, small_ref, kt_ref):
    x = h_ref[...]
    ms = jnp.mean(x * x, axis=-1, keepdims=True)
    u = x * lax.rsqrt(ms + EPS) * g_ref[...]
    u = u * (1.0 + sc_ref[...]) + sh_ref[...]
    ub = u.astype(BF16)
    bd = bd_ref[...]

    for c0 in range(0, 2 * NA_WIDTH, NA_WIDTH):
        wide = _dot(ub, w_ref[:, c0:c0 + NA_WIDTH])
        for c1 in range(0, NA_WIDTH, 2 * LANE):
            acc = wide[:, c1:c1 + 2 * LANE]
            msq = _segment_mean_sq(acc, bd)
            y = acc * lax.rsqrt(msq + EPS) * qkg_ref[:, c0 + c1:c0 + c1 + 2 * LANE]
            big_ref[:, c0 + c1:c0 + c1 + 2 * LANE] = y.astype(BF16)
    for c0 in range(COL_V, COL_MQ, PLAIN_COLS):
        big_ref[:, c0:c0 + PLAIN_COLS] = _dot(ub, w_ref[:, c0:c0 + PLAIN_COLS]).astype(BF16)
    lane = lax.broadcasted_iota(jnp.int32, (x.shape[0], LANE), 1)
    first_half = (lane % ML_HEAD_DIM) < (ML_HEAD_DIM // 2)
    for base, scale in ((COL_MQ, ML_HEAD_DIM ** -0.5), (COL_MK, 1.0)):
        for j in range(ML_WIDTH // LANE):
            c0 = base + j * LANE
            acc = _dot(ub, w_ref[:, c0:c0 + LANE])
            swapped = jnp.where(first_half,
                                pltpu.roll(acc, LANE - ML_HEAD_DIM // 2, 1),
                                pltpu.roll(acc, ML_HEAD_DIM // 2, 1))
            y = acc * cos_ref[:, j * LANE:(j + 1) * LANE] + swapped * sin_ref[:, j * LANE:(j + 1) * LANE]
            big_ref[:, c0:c0 + LANE] = (y * scale).astype(BF16)
            if base == COL_MK:
                kt_ref[j * LANE:(j + 1) * LANE, :] = y.T.astype(BF16)
    big_ref[:, COL_MV:BIG_W] = _dot(ub, w_ref[:, COL_MV:BIG_W]).astype(BF16)
    small_ref[...] = _dot(ub, w_ref[:, BIG_W:BIG_W + SMALL_W])


def _inproj(h, sh, sc, g, w, qkg, bd, cosf, sinf):
    bsz, s_len, d = h.shape
    tm = TOK_TILE
    mod_spec = pl.BlockSpec((None, None, 1, d), lambda b, i: (b, jnp.minimum(i, 1), 0, 0))
    full = lambda shape: pl.BlockSpec(shape, lambda b, i: (0,) * len(shape))
    return pl.pallas_call(
        _inproj_kernel,
        out_shape=(jax.ShapeDtypeStruct((bsz, s_len, BIG_W), BF16),
                   jax.ShapeDtypeStruct((bsz, s_len, SMALL_W), F32),
                   jax.ShapeDtypeStruct((bsz, ML_WIDTH, s_len), BF16)),
        grid=(bsz, s_len // tm),
        in_specs=[pl.BlockSpec((None, tm, d), lambda b, i: (b, i, 0)),
                  mod_spec, mod_spec, full((1, d)), full(w.shape), full(qkg.shape), full(bd.shape),
                  pl.BlockSpec((tm, ML_WIDTH), lambda b, i: (i, 0)),
                  pl.BlockSpec((tm, ML_WIDTH), lambda b, i: (i, 0))],
        out_specs=(pl.BlockSpec((None, tm, BIG_W), lambda b, i: (b, i, 0)),
                   pl.BlockSpec((None, tm, SMALL_W), lambda b, i: (b, i, 0)),
                   pl.BlockSpec((None, ML_WIDTH, tm), lambda b, i: (b, 0, i))),
        compiler_params=_cparams(("arbitrary", "arbitrary")),
        name="inproj",
    )(h, sh, sc, g, w, qkg, bd, cosf, sinf)


def _na_kernel(q_ref, k0, k1, k2, k3, k4, v0, v1, v2, v3, v4, kc_ref, vc_ref, bias_ref, o_ref):
    nq = q_ref.shape[0]
    lane = lax.broadcasted_iota(jnp.int32, (nq, LANE), 1)
    low = lane < NA_HEAD_DIM
    zero = jnp.zeros((), BF16)
    outs = []
    for p in range(NA_WIDTH // LANE):
        cs = slice(p * LANE, (p + 1) * LANE)
        qp = q_ref[:, cs]
        q01 = jnp.concatenate([jnp.where(low, qp, zero), jnp.where(low, zero, qp)], axis=0)
        kw = jnp.concatenate([r[:, cs] for r in (k0, k1, k2, k3, k4)], axis=0)
        vw = jnp.concatenate([r[:, cs] for r in (v0, v1, v2, v3, v4)], axis=0)
        vc = vc_ref[:, cs]
        vw = jnp.concatenate([vw, jnp.ones(vw.shape, BF16)], axis=1)
        vc = jnp.concatenate([vc, jnp.ones(vc.shape, BF16)], axis=1)
        s_w = _dot_nt(q01, kw) + bias_ref[p]
        s_c = _dot_nt(q01, kc_ref[:, cs])
        m = jnp.maximum(jnp.max(s_w, axis=-1, keepdims=True), jnp.max(s_c, axis=-1, keepdims=True))
        p_w = jnp.exp(s_w - m).astype(BF16)
        p_c = jnp.exp(s_c - m).astype(BF16)
        o01 = _dot(p_w, vw) + _dot(p_c, vc)
        o01 = o01[:, :LANE] / o01[:, LANE:]
        outs.append(jnp.where(low, o01[:nq], o01[nq:]).astype(o_ref.dtype))
    o_ref[...] = jnp.concatenate(outs, axis=1)


def _attention(big, bias, n_ctx, rows, ctx_out):
    bsz, s_len, _ = big.shape
    nq = NA_ROWS * GRID_W
    ncb = n_ctx // nq
    nrp = rows // NA_ROWS
    first = 0 if ctx_out else ncb
    n_win = NA_KROWS // NA_ROWS

    def lat(i):
        return jnp.maximum(i + first - ncb, 0)

    def kv_spec(j, col):
        return pl.BlockSpec(
            (None, nq, NA_WIDTH),
            lambda b, i: (b, ncb + jnp.clip(lat(i) - 2, 0, nrp - n_win) + j, col))

    def bias_type(b, i):
        step = i + first
        r = lat(i)
        t = jnp.where(r < 2, r, jnp.where(r >= nrp - 2, r - (nrp - 2) + 3, 2))
        return (jnp.where(step < ncb, 5, t), 0, 0, 0)

    return pl.pallas_call(
        _na_kernel,
        out_shape=jax.ShapeDtypeStruct((bsz, s_len, NA_WIDTH), BF16),
        grid=(bsz, ncb + nrp - first),
        in_specs=[pl.BlockSpec((None, nq, NA_WIDTH), lambda b, i: (b, i + first, 0))]
                 + [kv_spec(j, 1) for j in range(n_win)] + [kv_spec(j, 2) for j in range(n_win)]
                 + [pl.BlockSpec((None, n_ctx, NA_WIDTH), lambda b, i: (b, 0, 1)),
                    pl.BlockSpec((None, n_ctx, NA_WIDTH), lambda b, i: (b, 0, 2)),
                    pl.BlockSpec((None, NA_WIDTH // LANE, 2 * nq, NA_KROWS * GRID_W), bias_type)],
        out_specs=pl.BlockSpec((None, nq, NA_WIDTH), lambda b, i: (b, i + first, 0)),
        compiler_params=_cparams(("arbitrary", "arbitrary")),
        name="attention",
    )(big, *([big] * (2 * n_win + 2)), bias)


def _attention_bias(rpb, rows):
    nrp = rows // NA_ROWS
    n_win = NA_KROWS // NA_ROWS
    reps = np.array([0, 1, 2, nrp - 2, nrp - 1])
    qr = (NA_ROWS * reps)[:, None] + np.arange(NA_ROWS)[None, :]
    ws = NA_ROWS * np.clip(reps - 2, 0, nrp - n_win)
    kr = ws[:, None] + np.arange(NA_KROWS)[None, :]
    r0 = np.clip(qr - NA_WIN_H // 2, 0, rows - NA_WIN_H)
    ok_r = (kr[:, None, :] >= r0[:, :, None]) & (kr[:, None, :] < r0[:, :, None] + NA_WIN_H)
    dr = np.clip(kr[:, None, :] - qr[:, :, None] + (NA_WIN_H - 1), 0, 2 * NA_WIN_H - 2)
    qc = np.arange(GRID_W)
    c0 = np.clip(qc - NA_WIN_W // 2, 0, GRID_W - NA_WIN_W)
    kc = np.arange(GRID_W)
    ok_c = (kc[None, :] >= c0[:, None]) & (kc[None, :] < c0[:, None] + NA_WIN_W)
    dc = np.clip(kc[None, :] - qc[:, None] + (NA_WIN_W - 1), 0, 2 * NA_WIN_W - 2)
    sel_r = jnp.asarray(dr[..., None] == np.arange(2 * NA_WIN_H - 1), F32)
    sel_c = jnp.asarray(dc[..., None] == np.arange(2 * NA_WIN_W - 1), F32)
    hp = lax.Precision.HIGHEST
    g = jnp.einsum('hrc,taer->thaec', rpb.astype(F32), sel_r, precision=hp)
    g = jnp.einsum('thaec,qkc->thaqek', g, sel_c, precision=hp)
    ok = ok_r[:, None, :, None, :, None] & ok_c[None, None, None, :, None, :]
    g = jnp.where(ok, g, NEG)
    g = g.reshape(5, NA_HEADS * NA_ROWS * GRID_W, NA_KROWS * GRID_W)
    g = g.reshape(5, NA_WIDTH // LANE, 2 * NA_ROWS * GRID_W, NA_KROWS * GRID_W)
    masked = jnp.full((1,) + g.shape[1:], NEG, F32)
    return jnp.concatenate([g, masked], axis=0)


def _scan_chunk(d, t, ncc, nc):
    bwd = jnp.where(t < ncc, ncc - 1 - t, ncc + (nc - 1 - t))
    return jnp.where(d == 0, t, bwd)


def _precedes(d):
    ii = lax.broadcasted_iota(jnp.int32, (CHUNK, CHUNK), 0)
    jj = lax.broadcasted_iota(jnp.int32, (CHUNK, CHUNK), 1)
    return (jj - ii) * (1 - 2 * d) <= 0


def _cumsum_scan_order(prec, x):
    tri = jnp.where(prec, 1.0, 0.0).astype(BF16)
    x1, x2, x3 = _split3(x)
    return _dot(tri, x1) + _dot(tri, x2) + _dot(tri, x3)


def _ssd_kernel(x_ref, xp_ref, xn_ref, sm_ref, cw_ref, cb_ref, par_ref, y_ref, st_ref, cbuf_ref,
                *, ncc, nc):
    d = pl.program_id(1)
    t = pl.program_id(2)
    chunk = _scan_chunk(d, t, ncc, nc)

    @pl.when(t == 0)
    def _():
        st_ref[...] = jnp.zeros_like(st_ref)

    prev_ok = jnp.where(jnp.logical_and(chunk != 0, chunk != ncc), 1.0, 0.0)
    next_ok = jnp.where(jnp.logical_and(chunk != ncc - 1, chunk != nc - 1), 1.0, 0.0)
    prec = _precedes(d)
    rep = SSM_HEADS // SSM_GROUPS
    nb = x_ref.shape[0]
    for bb in range(nb):
        cbuf_ref[bb, 0:HALO, :] = xp_ref[bb].astype(F32) * prev_ok
        cbuf_ref[bb, HALO:HALO + CHUNK, :] = x_ref[bb].astype(F32)
        cbuf_ref[bb, HALO + CHUNK:, :] = xn_ref[bb].astype(F32) * next_ok
    taps = [[cbuf_ref[bb, HALO - SSM_CONV // 2 + k:HALO - SSM_CONV // 2 + k + CHUNK, :]
             for k in range(SSM_CONV)] for bb in range(nb)]
    gates_in = [sm_ref[bb] for bb in range(nb)]
    states = [[st_ref[bb, h] for h in range(SSM_HEADS)] for bb in range(nb)]
    stores = []
    for bb in range(nb):
        acc = jnp.broadcast_to(cb_ref[...], (CHUNK, SSM_CONV_CH))
        for k in range(SSM_CONV):
            acc = acc + taps[bb][k] * cw_ref[k:k + 1, :]
        xc = acc * _sigmoid(acc)
        xs = xc[:, :SSM_WIDTH]
        bm = xc[:, SSM_WIDTH:SSM_WIDTH + SSM_BC]
        cm = xc[:, SSM_WIDTH + SSM_BC:]

        dt = _softplus(gates_in[bb] + par_ref[0:1, :])
        la = dt * (-jnp.exp(par_ref[1:2, :]))
        acum = _cumsum_scan_order(prec, la)
        acum_t = acum.T
        total = jnp.sum(la, axis=0, keepdims=True)

        for g in range(SSM_GROUPS):
            cg = cm[:, g * SSM_STATE:(g + 1) * SSM_STATE].astype(BF16)
            bg = bm[:, g * SSM_STATE:(g + 1) * SSM_STATE].astype(BF16)
            cb = _dot_nt(cg, bg)
            for hh in range(rep):
                h = g * rep + hh
                a_col = acum[:, h:h + 1]
                a_row = acum_t[h:h + 1, :]
                tot = total[:, h:h + 1]
                x_h = xs[:, h * SSM_HEAD_DIM:(h + 1) * SSM_HEAD_DIM]
                xdt = x_h * dt[:, h:h + 1]
                lmat = jnp.exp(jnp.where(prec, a_col - a_row, -jnp.inf))
                h_in = states[bb][h]
                y = _dot((cb * lmat).astype(BF16), xdt.astype(BF16))
                y = y + _dot(cg, h_in.astype(BF16)) * jnp.exp(a_col)
                y = y + x_h * par_ref[2:3, h:h + 1]
                xw = (xdt * jnp.exp(tot - a_col)).astype(BF16)
                stores.append((bb, h, y, jnp.exp(tot) * h_in + _dot_tn(bg, xw)))

    for bb, h, y, h_new in stores:
        y_ref[bb, :, h * SSM_HEAD_DIM:(h + 1) * SSM_HEAD_DIM] = y
        st_ref[bb, h] = h_new


def _scan_batch(bsz):
    return math.gcd(bsz, SCAN_BATCH)


def _ssd(big, small, conv_w, conv_b, par, n_ctx):
    bsz, s_len, _ = big.shape
    ncc, nc = n_ctx // CHUNK, s_len // CHUNK
    per = CHUNK // HALO
    nb = _scan_batch(bsz)
    ch = functools.partial(_scan_chunk, ncc=ncc, nc=nc)
    cblk = COL_XBC // SSM_CONV_CH
    return pl.pallas_call(
        functools.partial(_ssd_kernel, ncc=ncc, nc=nc),
        out_shape=jax.ShapeDtypeStruct((2, bsz, s_len, SSM_WIDTH), F32),
        grid=(bsz // nb, 2, nc),
        in_specs=[
            pl.BlockSpec((nb, CHUNK, SSM_CONV_CH), lambda b, d, t: (b, ch(d, t), cblk)),
            pl.BlockSpec((nb, HALO, SSM_CONV_CH),
                         lambda b, d, t: (b, jnp.maximum(ch(d, t) * per - 1, 0), cblk)),
            pl.BlockSpec((nb, HALO, SSM_CONV_CH),
                         lambda b, d, t: (b, jnp.minimum((ch(d, t) + 1) * per, nc * per - 1), cblk)),
            pl.BlockSpec((nb, CHUNK, LANE), lambda b, d, t: (b, ch(d, t), d)),
            pl.BlockSpec(conv_w.shape, lambda b, d, t: (0, 0)),
            pl.BlockSpec(conv_b.shape, lambda b, d, t: (0, 0)),
            pl.BlockSpec((None, 8, LANE), lambda b, d, t: (d, 0, 0)),
        ],
        out_specs=pl.BlockSpec((None, nb, CHUNK, SSM_WIDTH), lambda b, d, t: (d, b, ch(d, t), 0)),
        scratch_shapes=[pltpu.VMEM((nb, SSM_HEADS, SSM_STATE, SSM_HEAD_DIM), F32),
                        pltpu.VMEM((nb, CHUNK + 2 * HALO, SSM_CONV_CH), F32)],
        compiler_params=_cparams(("arbitrary", "arbitrary", "arbitrary")),
        name="ssd",
    )(big, big, big, small, conv_w, conv_b, par)


def _mlstm_kernel(q_ref, k_ref, v_ref, kt_ref, sm_ref, par_ref, y_ref, st_ref, m_ref, *, ncc, nc):
    d = pl.program_id(1)
    t = pl.program_id(2)

    @pl.when(t == 0)
    def _():
        st_ref[...] = jnp.zeros_like(st_ref)
        m_ref[...] = jnp.zeros_like(m_ref)

    prec = _precedes(d)
    nb = q_ref.shape[0]
    dh = ML_HEAD_DIM
    loaded = [(sm_ref[bb], q_ref[bb], k_ref[bb], v_ref[bb], kt_ref[bb],
               [(m_ref[bb, h][0:1, :], st_ref[bb, h]) for h in range(ML_HEADS)])
              for bb in range(nb)]
    sel_row = lax.broadcasted_iota(jnp.int32, (LANE, LANE), 0)
    low = lax.broadcasted_iota(jnp.int32, (CHUNK, LANE), 1) < dh
    ones_tile = jnp.ones((CHUNK, LANE), BF16)
    twice = lambda x: jnp.concatenate([x, x], axis=1)
    stores = []
    for bb in range(nb):
        sm, q_all, k_all, v_all, kt_all, states = loaded[bb]
        pre = sm + par_ref[0:1, :]
        lf = -_softplus(-pre)
        bcum = _cumsum_scan_order(prec, lf)
        bcum_t = bcum.T
        li_t = pre.T
        gsum = jnp.broadcast_to(jnp.sum(lf, axis=0, keepdims=True), (8, LANE))
        bcum_p, gsum_p = _split3(bcum), _split3(gsum)

        def lane_rep(pieces, c):
            sel = jnp.where(sel_row == c, 1.0, 0.0).astype(BF16)
            return _dot(pieces[0], sel) + _dot(pieces[1], sel) + _dot(pieces[2], sel)

        tiles = []
        for h in range(ML_HEADS):
            hs = slice(h * dh, (h + 1) * dh)
            ps = slice((h // 2) * LANE, (h // 2 + 1) * LANE)
            ci, cf = ML_HEADS + h, 2 * ML_HEADS + h
            b_rep = lane_rep(bcum_p, cf)
            gt = lane_rep(gsum_p, cf)[0:1, :]
            b_row = bcum_t[cf:cf + 1, :]
            li_row = li_t[ci:ci + 1, :]
            qh = q_all[:, hs]
            kh = k_all[:, hs]
            v_aug = jnp.concatenate([v_all[:, ps], ones_tile], axis=1)
            m_in, st_in = states[h]

            log_d = jnp.where(prec, b_rep - b_row + li_row, -jnp.inf)
            log_inter = b_rep + m_in
            m_i = jnp.maximum(jnp.max(log_d, axis=-1, keepdims=True), log_inter)
            s = _dot_nt(qh, kh) * jnp.exp(log_d - m_i)
            inter = jnp.exp(log_inter - m_i)
            numden = _dot(s.astype(BF16), v_aug) + _dot(qh, st_in.astype(BF16)) * twice(inter)
            tiles.append(numden[:, :LANE] / jnp.maximum(jnp.abs(numden[:, LANE:]), jnp.exp(-m_i)))

            w_row = gt - b_row + li_row
            m_loc = jnp.max(w_row, axis=-1, keepdims=True)
            ket = (kt_all[hs, :].astype(F32) * jnp.exp(w_row - m_loc)).astype(BF16)
            m_new = jnp.maximum(gt + m_in, m_loc)
            a_old = jnp.exp(gt + m_in - m_new)
            a_new = jnp.exp(m_loc - m_new)
            st_new = twice(a_old) * st_in + twice(a_new) * _dot(ket, v_aug)
            stores.append((bb, h, st_new, m_new))
        for p in range(ML_HEADS // 2):
            y_ref[bb, :, p * LANE:(p + 1) * LANE] = jnp.where(low, tiles[2 * p], tiles[2 * p + 1])

    for bb, h, st_new, m_new in stores:
        st_ref[bb, h] = st_new
        m_ref[bb, h] = jnp.broadcast_to(m_new, m_ref.shape[2:])


def _mlstm(big, kt, small, par, n_ctx):
    bsz, s_len, _ = big.shape
    ncc, nc = n_ctx // CHUNK, s_len // CHUNK
    nb = _scan_batch(bsz)
    ch = functools.partial(_scan_chunk, ncc=ncc, nc=nc)

    def col_spec(col):
        return pl.BlockSpec((nb, CHUNK, ML_WIDTH), lambda b, d, t: (b, ch(d, t), col // ML_WIDTH))

    return pl.pallas_call(
        functools.partial(_mlstm_kernel, ncc=ncc, nc=nc),
        out_shape=jax.ShapeDtypeStruct((2, bsz, s_len, ML_WIDTH), F32),
        grid=(bsz // nb, 2, nc),
        in_specs=[col_spec(COL_MQ), col_spec(COL_MK), col_spec(COL_MV),
                  pl.BlockSpec((nb, ML_WIDTH, CHUNK), lambda b, d, t: (b, 0, ch(d, t))),
                  pl.BlockSpec((nb, CHUNK, LANE), lambda b, d, t: (b, ch(d, t), d)),
                  pl.BlockSpec((None, 8, LANE), lambda b, d, t: (d, 0, 0))],
        out_specs=pl.BlockSpec((None, nb, CHUNK, ML_WIDTH), lambda b, d, t: (d, b, ch(d, t), 0)),
        scratch_shapes=[pltpu.VMEM((nb, ML_HEADS, ML_HEAD_DIM, 2 * LANE), F32),
                        pltpu.VMEM((nb, ML_HEADS, 8, LANE), F32)],
        compiler_params=_cparams(("arbitrary", "arbitrary", "arbitrary")),
        name="mlstm",
    )(big, big, big, kt, small, par)


def _merge_kernel(ya_ref, yb_ref, z_ref, yc_ref, o_ref, h_ref, g1_ref, sh_ref, sc_ref, ng_ref,
                  sg_ref, mg_ref, bd_ref, wo_ref, rwh_ref, rwl_ref, rb_ref,
                  hout_ref, v_ref, gates_ref):
    z = z_ref[...].astype(F32)
    yb = (yb_ref[0] + yb_ref[1]) * (z * _sigmoid(z))
    yb = yb * lax.rsqrt(jnp.mean(yb * yb, axis=-1, keepdims=True) + EPS) * sg_ref[...]
    yc = _sigmoid(o_ref[...].astype(F32)) * (yc_ref[0] + yc_ref[1])
    yc = yc * lax.rsqrt(_segment_mean_sq(yc, bd_ref[...]) + EPS) * mg_ref[...]
    y = _dot(jnp.concatenate([ya_ref[...], yb.astype(BF16), yc.astype(BF16)], axis=1), wo_ref[...])
    hn = h_ref[...] + g1_ref[...] * y
    hout_ref[...] = hn
    v = hn * lax.rsqrt(jnp.mean(hn * hn, axis=-1, keepdims=True) + EPS) * ng_ref[...]
    v = v * (1.0 + sc_ref[...]) + sh_ref[...]
    v_hi, v_lo = _split2(v)
    v_ref[...] = v_hi
    rwh = rwh_ref[...]
    logits = (_dot_nt(rwh, v_hi) + _dot_nt(rwh, v_lo) + _dot_nt(rwl_ref[...], v_hi))[:N_EXPERTS] + rb_ref[...]

    row = lax.broadcasted_iota(jnp.int32, logits.shape, 0).astype(F32)
    work = logits
    top = None
    gates = jnp.zeros_like(logits)
    den = jnp.zeros((1, logits.shape[1]), F32)
    for _ in range(TOP_K):
        mx = jnp.max(work, axis=0, keepdims=True)
        idx = jnp.min(jnp.where(work == mx, row, float(N_EXPERTS)), axis=0, keepdims=True)
        hit = row == idx
        top = mx if top is None else top
        e = jnp.exp(mx - top)
        gates = gates + jnp.where(hit, e, 0.0)
        den = den + e
        work = jnp.where(hit, -jnp.inf, work)
    gates_ref[...] = jnp.concatenate(
        [gates / den, jnp.zeros((LANE - N_EXPERTS, logits.shape[1]), F32)], axis=0)


def _merge(ya, yb, yc, big, h, g1, sh2, sc2, ng, sg, mg, bd, wo, rwh, rwl, rb, n_ctx, ctx_out):
    bsz, s_len, d = h.shape
    tm = TOK_TILE
    first = 0 if ctx_out else n_ctx // tm
    n_out = s_len - first * tm
    mod_spec = pl.BlockSpec((None, None, 1, d), lambda b, i: (b, jnp.minimum(i + first, 1), 0, 0))
    full = lambda shape: pl.BlockSpec(shape, lambda b, i: (0,) * len(shape))
    pair = lambda w: pl.BlockSpec((2, None, tm, w), lambda b, i: (0, b, i + first, 0))
    nl = (s_len - n_ctx) // tm

    def flat_row(b, i):
        if ctx_out:
            return jnp.where(i == 0, bsz * nl + b, (i - 1) * bsz + b)
        return i * bsz + b

    flat_tok = lambda w: pl.BlockSpec((tm, w), lambda b, i: (flat_row(b, i), 0))
    return pl.pallas_call(
        _merge_kernel,
        out_shape=(jax.ShapeDtypeStruct((bsz, n_out, d), F32),
                   jax.ShapeDtypeStruct((bsz * n_out, d), BF16),
                   jax.ShapeDtypeStruct((LANE, bsz * n_out), F32)),
        grid=(bsz, n_out // tm),
        in_specs=[pl.BlockSpec((None, tm, NA_WIDTH), lambda b, i: (b, i + first, 0)),
                  pair(SSM_WIDTH),
                  pl.BlockSpec((None, tm, SSM_WIDTH), lambda b, i: (b, i + first, COL_Z // SSM_WIDTH)),
                  pair(ML_WIDTH),
                  pl.BlockSpec((None, tm, ML_WIDTH), lambda b, i: (b, i + first, COL_MO // ML_WIDTH)),
                  pl.BlockSpec((None, tm, d), lambda b, i: (b, i + first, 0)),
                  mod_spec, mod_spec, mod_spec,
                  full((1, d)), full((1, SSM_WIDTH)), full((1, ML_WIDTH)), full(bd.shape),
                  full(wo.shape), full(rwh.shape), full(rwl.shape), full(rb.shape)],
        out_specs=(pl.BlockSpec((None, tm, d), lambda b, i: (b, i, 0)), flat_tok(d),
                   pl.BlockSpec((LANE, tm), lambda b, i: (0, flat_row(b, i)))),
        compiler_params=_cparams(("arbitrary", "arbitrary")),
        name="merge",
    )(ya, yb, big, yc, big, h, g1, sh2, sc2, ng, sg, mg, bd, wo, rwh, rwl, rb)


def _moe_kernel(x_ref, gates_ref, w1_ref, b1_ref, w2_ref, b2_ref, f_ref, rank_ref, *, nblk):
    e = pl.program_id(1)
    tb = x_ref.shape[0] // nblk
    d_ff = w2_ref.shape[0]

    @pl.when(e == 0)
    def _():
        f_ref[...] = jnp.zeros_like(f_ref)
        ii = lax.broadcasted_iota(jnp.int32, (tb, tb), 0)
        jj = lax.broadcasted_iota(jnp.int32, (tb, tb), 1)
        tri = jnp.where(ii <= jj, 1.0, 0.0).astype(BF16)
        for k in range(nblk):
            picked = gates_ref[:, k * tb:(k + 1) * tb] > 0.0
            incl = _dot(jnp.where(picked, 1.0, 0.0).astype(BF16), tri)
            rank_ref[k] = jnp.where(picked, incl - 1.0, -1.0)

    def compact(k, s):
        slot = rank_ref[k, pl.ds(e, 1), :]
        gate = gates_ref[pl.ds(e, 1), k * tb:(k + 1) * tb]
        row = (lax.broadcasted_iota(jnp.int32, (MOE_CAP, tb), 0) + s * MOE_CAP).astype(F32)
        hit = slot == row
        onehot = jnp.where(hit, 1.0, 0.0).astype(BF16)
        xe = _dot(onehot, x_ref[k * tb:(k + 1) * tb, :]).astype(BF16)
        w = jnp.sum(jnp.where(hit, gate, 0.0), axis=-1, keepdims=True)
        return onehot, xe, w

    def expert(xe):
        hid = _dot(xe, w1_ref[...]) + b1_ref[...]
        x_glu = jnp.minimum(hid[:, :d_ff], SWIGLU_LIMIT)
        x_lin = jnp.clip(hid[:, d_ff:], -SWIGLU_LIMIT, SWIGLU_LIMIT)
        act = x_glu * _sigmoid(SWIGLU_ALPHA * x_glu) * (x_lin + 1.0)
        return _dot(act.astype(BF16), w2_ref[...]) + b2_ref[...]

    def scatter(k, onehot, yw):
        f_ref[k * tb:(k + 1) * tb, :] += _dot_tn(onehot, yw.astype(BF16))

    first = [compact(k, 0) for k in range(nblk)]
    y = expert(jnp.concatenate([xe for _, xe, _ in first], axis=0))
    for k, (onehot, _, w) in enumerate(first):
        scatter(k, onehot, y[k * MOE_CAP:(k + 1) * MOE_CAP] * w)

    for k in range(nblk):
        count = jnp.max(rank_ref[k, pl.ds(e, 1), :]).astype(jnp.int32) + 1

        def overflow(s, carry, k=k):
            onehot, xe, w = compact(k, s)
            scatter(k, onehot, expert(xe) * w)
            return carry

        lax.fori_loop(1, (count + MOE_CAP - 1) // MOE_CAP, overflow, 0)


def _moe(x, gates, w1, b1, w2, b2, layer, first_tok, n_tok):
    d = x.shape[1]
    nblk = MOE_GROUP if n_tok % (MOE_GROUP * MOE_BLOCK) == 0 else 1
    tb = math.gcd(n_tok, MOE_BLOCK)
    rows = nblk * tb
    d_ff = w2.shape[1]
    off = first_tok // rows
    assert first_tok % rows == 0 and n_tok % rows == 0
    expert = lambda i, e: (layer * N_EXPERTS + e, 0, 0)
    return pl.pallas_call(
        functools.partial(_moe_kernel, nblk=nblk),
        out_shape=jax.ShapeDtypeStruct((n_tok, d), F32),
        grid=(n_tok // rows, N_EXPERTS),
        in_specs=[pl.BlockSpec((rows, d), lambda i, e: (i + off, 0)),
                  pl.BlockSpec((LANE, rows), lambda i, e: (0, i + off)),
                  pl.BlockSpec((None, d, 2 * d_ff), expert),
                  pl.BlockSpec((None, 1, 2 * d_ff), expert),
                  pl.BlockSpec((None, d_ff, d), expert),
                  pl.BlockSpec((None, 1, d), expert)],
        out_specs=pl.BlockSpec((rows, d), lambda i, e: (i, 0)),
        scratch_shapes=[pltpu.VMEM((nblk, LANE, tb), F32)],
        compiler_params=pltpu.CompilerParams(dimension_semantics=("arbitrary", "arbitrary"),
                                             vmem_limit_bytes=MOE_VMEM_LIMIT),
        name="moe",
    )(x, gates, w1, b1, w2, b2)


def _resid_kernel(h_ref, g_ref, fl_ref, *rest, first):
    if first == 0:
        fc_ref, o_ref = rest
        f = jnp.where(pl.program_id(1) == 0, fc_ref[...], fl_ref[...])
    else:
        (o_ref,) = rest
        f = fl_ref[...]
    o_ref[...] = h_ref[...] + g_ref[...] * f


def _resid(h, g2, f_lat, f_ctx):
    bsz, n_tok, d = h.shape
    tm = TOK_TILE
    first = 0 if f_ctx is not None else 1
    in_specs = [pl.BlockSpec((None, tm, d), lambda b, i: (b, i, 0)),
                pl.BlockSpec((None, None, 1, d), lambda b, i: (b, jnp.minimum(i + first, 1), 0, 0)),
                pl.BlockSpec((tm, d), lambda b, i: (jnp.maximum(i + first - 1, 0) * bsz + b, 0))]
    args = [h, g2, f_lat]
    if f_ctx is not None:
        in_specs.append(pl.BlockSpec((tm, d), lambda b, i: (b, 0)))
        args.append(f_ctx)
    return pl.pallas_call(
        functools.partial(_resid_kernel, first=first),
        out_shape=jax.ShapeDtypeStruct((bsz, n_tok, d), F32),
        grid=(bsz, n_tok // tm),
        in_specs=in_specs,
        out_specs=pl.BlockSpec((None, tm, d), lambda b, i: (b, i, 0)),
        compiler_params=_cparams(("arbitrary", "arbitrary")),
        name="resid",
    )(*args)


def _in_weight_perm():
    o = IN_OFFSETS
    big = np.concatenate([np.arange(o[0], o[5]), np.arange(o[6], o[10])])
    small = np.full((2, LANE), -1, np.int64)
    for d in range(2):
        for h in range(SSM_HEADS):
            small[d, h] = o[5] + d * SSM_HEADS + h
        for h in range(ML_HEADS):
            small[d, ML_HEADS + h] = o[10] + d * ML_HEADS + h
            small[d, 2 * ML_HEADS + h] = o[11] + d * ML_HEADS + h
    return np.concatenate([big, small.reshape(-1)])


def _rope_tables(n_ctx, n_lat):
    pos = np.arange(n_lat)
    row = (pos // GRID_W).astype(np.float32)
    col = (pos % GRID_W).astype(np.float32)
    n_freq = ML_HEAD_DIM // 4
    inv_freq = jnp.asarray(ROPE_THETA, F32) ** (-jnp.arange(n_freq, dtype=F32) / n_freq)
    ang = jnp.concatenate([jnp.asarray(row)[:, None] * inv_freq, jnp.asarray(col)[:, None] * inv_freq], axis=-1)
    cos, sin = jnp.cos(ang), jnp.sin(ang)
    cosf = jnp.tile(jnp.concatenate([cos, cos], axis=-1), (1, ML_HEADS))
    sinf = jnp.tile(jnp.concatenate([-sin, sin], axis=-1), (1, ML_HEADS))
    cosf = jnp.concatenate([jnp.ones((n_ctx, ML_WIDTH), F32), cosf], axis=0)
    sinf = jnp.concatenate([jnp.zeros((n_ctx, ML_WIDTH), F32), sinf], axis=0)
    return cosf, sinf


def _lane_rows(rows):
    out = jnp.zeros((8, LANE), F32)
    for r, (off, vec) in enumerate(rows):
        out = out.at[r, off:off + vec.shape[0]].set(vec.astype(F32))
    return out


def kernel(x, c, ctx, c_ctx, ada_w, ada_b, norm_g, w_in, na_qk_g, na_rpb, ssm_conv_w, ssm_conv_b,
           ssm_dt_bias, ssm_a_log, ssm_d, ssm_norm_g, ml_gate_b, ml_norm_g, w_out,
           router_w, router_b, exp_w1, exp_b1, exp_w2, exp_b2):
    bsz, n_lat, d = x.shape
    n_ctx = ctx.shape[1]
    rows = n_lat // GRID_W
    depth = w_in.shape[0]
    assert bsz < 8 and n_ctx == TOK_TILE and n_lat % TOK_TILE == 0 and rows >= NA_KROWS

    h = jnp.concatenate([ctx, x], axis=1)
    cond = jnp.zeros((8, d), F32).at[:bsz].set(c).at[bsz].set(c_ctx)
    cosf, sinf = _rope_tables(n_ctx, n_lat)
    perm = _in_weight_perm()
    seg = np.arange(2 * LANE) // NA_HEAD_DIM
    bd = jnp.asarray(seg[:, None] == seg[None, :], BF16)

    n_exp = exp_w1.shape[0] * exp_w1.shape[1]
    w1_all = exp_w1.astype(BF16).reshape((n_exp,) + exp_w1.shape[2:])
    w2_all = exp_w2.astype(BF16).reshape((n_exp,) + exp_w2.shape[2:])
    b1_all = exp_b1.reshape(n_exp, 1, -1)
    b2_all = exp_b2.reshape(n_exp, 1, -1)

    out = None
    for layer in range(depth):
        ctx_out = layer < depth - 1
        mod = _ada(cond, ada_w[layer], ada_b[layer])
        mods = []
        for m in jnp.split(mod, 6, axis=-1):
            lat = m[:bsz]
            cx = jnp.broadcast_to(m[bsz][None], (bsz, d))
            mods.append(jnp.stack([cx, lat], axis=1).reshape(bsz, 2, 1, d))
        sh1, sc1, g1, sh2, sc2, g2 = mods

        w = jnp.where(perm[None, :] >= 0, w_in[layer][:, np.maximum(perm, 0)], 0.0).astype(BF16)
        qkg = jnp.concatenate([jnp.tile(na_qk_g[layer, 0] * NA_HEAD_DIM ** -0.5, NA_HEADS),
                               jnp.tile(na_qk_g[layer, 1], NA_HEADS)]).reshape(1, 2 * NA_WIDTH)
        big, small, kt = _inproj(h, sh1, sc1, norm_g[layer, 0].reshape(1, d), w, qkg, bd, cosf, sinf)

        bias = _attention_bias(na_rpb[layer], rows)
        ya = _attention(big, bias, n_ctx, rows, ctx_out)

        conv_w = jnp.zeros((8, SSM_CONV_CH), F32).at[:SSM_CONV].set(ssm_conv_w[layer])
        ssm_par = jnp.stack([_lane_rows([(0, ssm_dt_bias[layer, dr]), (0, ssm_a_log[layer, dr]),
                                         (0, ssm_d[layer, dr])]) for dr in range(2)])
        yb = _ssd(big, small, conv_w, ssm_conv_b[layer].reshape(1, -1), ssm_par, n_ctx)

        gate_row = [jnp.concatenate([jnp.zeros((ML_HEADS,), F32), ml_gate_b[layer, dr, 0],
                                     ml_gate_b[layer, dr, 1]]) for dr in range(2)]
        ml_par = jnp.stack([_lane_rows([(0, gate_row[dr])]) for dr in range(2)])
        yc = _mlstm(big, kt, small, ml_par, n_ctx)

        rw = jnp.zeros((LANE, d), F32).at[:N_EXPERTS].set(router_w[layer].T)
        rwh = rw.astype(BF16)
        rwl = (rw - rwh.astype(F32)).astype(BF16)
        rb = jnp.broadcast_to(router_b[layer].astype(F32)[:, None], (N_EXPERTS, TOK_TILE))
        hn, v, gates = _merge(ya, yb, yc, big, h, g1, sh2, sc2, norm_g[layer, 1].reshape(1, d),
                              ssm_norm_g[layer].reshape(1, -1), ml_norm_g[layer].reshape(1, -1), bd,
                              w_out[layer].astype(BF16), rwh, rwl, rb, n_ctx, ctx_out)

        n_lat_tok = bsz * n_lat
        f_lat = _moe(v, gates, w1_all, b1_all, w2_all, b2_all, layer, 0, n_lat_tok)
        f_ctx = (_moe(v, gates, w1_all, b1_all, w2_all, b2_all, layer, n_lat_tok, bsz * n_ctx)
                 if ctx_out else None)
        h = _resid(hn, g2, f_lat, f_ctx)
        out = h
    return out
```

```python
import functools
import math

import numpy as np
import jax
import jax.numpy as jnp
from jax import lax
from jax.experimental import pallas as pl
from jax.experimental.pallas import tpu as pltpu

F32 = jnp.float32
BF16 = jnp.bfloat16

GRID_W = 64
EPS = 1e-6
NA_HEADS = 8
NA_HEAD_DIM = 64
NA_WIN_H = 8
NA_WIN_W = 16
SSM_HEADS = 4
SSM_HEAD_DIM = 64
SSM_GROUPS = 2
SSM_STATE = 128
SSM_CONV = 5
ML_HEADS = 4
ML_HEAD_DIM = 64
ROPE_THETA = 10000.0
N_EXPERTS = 32
TOP_K = 4
SWIGLU_ALPHA = 1.702
SWIGLU_LIMIT = 7.0

NA_WIDTH = NA_HEADS * NA_HEAD_DIM
SSM_WIDTH = SSM_HEADS * SSM_HEAD_DIM
SSM_BC = SSM_GROUPS * SSM_STATE
SSM_CONV_CH = SSM_WIDTH + 2 * SSM_BC
ML_WIDTH = ML_HEADS * ML_HEAD_DIM
IN_SPLITS = (NA_WIDTH, NA_WIDTH, NA_WIDTH, SSM_CONV_CH, SSM_WIDTH, 2 * SSM_HEADS,
             ML_WIDTH, ML_WIDTH, ML_WIDTH, ML_WIDTH, 2 * ML_HEADS, 2 * ML_HEADS)
IN_OFFSETS = tuple(int(o) for o in np.cumsum((0,) + IN_SPLITS))

LANE = 128
CHUNK = 128
HALO = 16
TOK_TILE = 256
NA_ROWS = 2
NA_KROWS = NA_ROWS + NA_WIN_H
BIG_W = 3 * NA_WIDTH + SSM_CONV_CH + SSM_WIDTH + 4 * ML_WIDTH
SMALL_W = 2 * LANE
PLAIN_COLS = 768
MOE_BLOCK = 1024
MOE_GROUP = 2
MOE_CAP = 160
SCAN_BATCH = 4
NEG = -1e30
VMEM_LIMIT = 48 * 1024 * 1024
MOE_VMEM_LIMIT = 56 * 1024 * 1024

COL_Q, COL_K, COL_V = 0, NA_WIDTH, 2 * NA_WIDTH
COL_XBC = 3 * NA_WIDTH
COL_Z = COL_XBC + SSM_CONV_CH
COL_MQ = COL_Z + SSM_WIDTH
COL_MK = COL_MQ + ML_WIDTH
COL_MV = COL_MK + ML_WIDTH
COL_MO = COL_MV + ML_WIDTH


def _cparams(sem):
    return pltpu.CompilerParams(dimension_semantics=sem, vmem_limit_bytes=VMEM_LIMIT)


def _sigmoid(x):
    return 1.0 / (1.0 + jnp.exp(-x))


def _softplus(x):
    return jnp.maximum(x, 0.0) + jnp.log1p(jnp.exp(-jnp.abs(x)))


def _dot(a, b):
    return jnp.dot(a, b, preferred_element_type=F32)


def _dot_nt(a, b):
    return lax.dot_general(a, b, (((1,), (1,)), ((), ())), preferred_element_type=F32)


def _dot_tn(a, b):
    return lax.dot_general(a, b, (((0,), (0,)), ((), ())), preferred_element_type=F32)


def _split3(x):
    x1 = x.astype(BF16)
    r1 = x - x1.astype(F32)
    x2 = r1.astype(BF16)
    x3 = (r1 - x2.astype(F32)).astype(BF16)
    return x1, x2, x3


def _split2(x):
    x1 = x.astype(BF16)
    return x1, (x - x1.astype(F32)).astype(BF16)


def _segment_mean_sq(x, bd):
    hi, lo = _split2(x * x)
    return (_dot(hi, bd) + _dot(lo, bd)) * (1.0 / NA_HEAD_DIM)


def _ada_kernel(c_ref, w_ref, b_ref, o_ref):
    c = c_ref[...]
    s = c * _sigmoid(c)
    o_ref[...] = jnp.dot(s, w_ref[...], preferred_element_type=F32,
                         precision=lax.Precision.HIGHEST) + b_ref[...]


def _ada(cond, w, b):
    n, d = cond.shape
    wcols = w.shape[1]
    bn = wcols // 4
    return pl.pallas_call(
        _ada_kernel,
        out_shape=jax.ShapeDtypeStruct((n, wcols), F32),
        grid=(4,),
        in_specs=[pl.BlockSpec((n, d), lambda j: (0, 0)),
                  pl.BlockSpec((d, bn), lambda j: (0, j)),
                  pl.BlockSpec((1, bn), lambda j: (0, j))],
        out_specs=pl.BlockSpec((n, bn), lambda j: (0, j)),
        compiler_params=_cparams(("arbitrary",)),
        name="ada",
    )(cond, w, b.reshape(1, wcols))


def _inproj_kernel(h_ref, sh_ref, sc_ref, g_ref, w_ref, qkg_ref, bd_ref, cos_ref, sin_ref,
                   big_ref, small_ref, kt_ref):
    x = h_ref[...]
    ms = jnp.mean(x * x, axis=-1, keepdims=True)
    u = x * lax.rsqrt(ms + EPS) * g_ref[...]
    u = u * (1.0 + sc_ref[...]) + sh_ref[...]
    ub = u.astype(BF16)
    bd = bd_ref[...]

    for c0 in range(0, 2 * NA_WIDTH, NA_WIDTH):
        wide = _dot(ub, w_ref[:, c0:c0 + NA_WIDTH])
        for c1 in range(0, NA_WIDTH, 2 * LANE):
            acc = wide[:, c1:c1 + 2 * LANE]
            msq = _segment_mean_sq(acc, bd)
            y = acc * lax.rsqrt(msq + EPS) * qkg_ref[:, c0 + c1:c0 + c1 + 2 * LANE]
            big_ref[:, c0 + c1:c0 + c1 + 2 * LANE] = y.astype(BF16)
    for c0 in range(COL_V, COL_MQ, PLAIN_COLS):
        big_ref[:, c0:c0 + PLAIN_COLS] = _dot(ub, w_ref[:, c0:c0 + PLAIN_COLS]).astype(BF16)
    lane = lax.broadcasted_iota(jnp.int32, (x.shape[0], LANE), 1)
    first_half = (lane % ML_HEAD_DIM) < (ML_HEAD_DIM // 2)
    for base, scale in ((COL_MQ, ML_HEAD_DIM ** -0.5), (COL_MK, 1.0)):
        for j in range(ML_WIDTH // LANE):
            c0 = base + j * LANE
            acc = _dot(ub, w_ref[:, c0:c0 + LANE])
            swapped = jnp.where(first_half,
                                pltpu.roll(acc, LANE - ML_HEAD_DIM // 2, 1),
                                pltpu.roll(acc, ML_HEAD_DIM // 2, 1))
            y = acc * cos_ref[:, j * LANE:(j + 1) * LANE] + swapped * sin_ref[:, j * LANE:(j + 1) * LANE]
            big_ref[:, c0:c0 + LANE] = (y * scale).astype(BF16)
            if base == COL_MK:
                kt_ref[j * LANE:(j + 1) * LANE, :] = y.T.astype(BF16)
    big_ref[:, COL_MV:BIG_W] = _dot(ub, w_ref[:, COL_MV:BIG_W]).astype(BF16)
    small_ref[...] = _dot(ub, w_ref[:, BIG_W:BIG_W + SMALL_W])


def _inproj(h, sh, sc, g, w, qkg, bd, cosf, sinf):
    bsz, s_len, d = h.shape
    tm = TOK_TILE
    mod_spec = pl.BlockSpec((None, None, 1, d), lambda b, i: (b, jnp.minimum(i, 1), 0, 0))
    full = lambda shape: pl.BlockSpec(shape, lambda b, i: (0,) * len(shape))
    return pl.pallas_call(
        _inproj_kernel,
        out_shape=(jax.ShapeDtypeStruct((bsz, s_len, BIG_W), BF16),
                   jax.ShapeDtypeStruct((bsz, s_len, SMALL_W), F32),
                   jax.ShapeDtypeStruct((bsz, ML_WIDTH, s_len), BF16)),
        grid=(bsz, s_len // tm),
        in_specs=[pl.BlockSpec((None, tm, d), lambda b, i: (b, i, 0)),
                  mod_spec, mod_spec, full((1, d)), full(w.shape), full(qkg.shape), full(bd.shape),
                  pl.BlockSpec((tm, ML_WIDTH), lambda b, i: (i, 0)),
                  pl.BlockSpec((tm, ML_WIDTH), lambda b, i: (i, 0))],
        out_specs=(pl.BlockSpec((None, tm, BIG_W), lambda b, i: (b, i, 0)),
                   pl.BlockSpec((None, tm, SMALL_W), lambda b, i: (b, i, 0)),
                   pl.BlockSpec((None, ML_WIDTH, tm), lambda b, i: (b, 0, i))),
        compiler_params=_cparams(("arbitrary", "arbitrary")),
        name="inproj",
    )(h, sh, sc, g, w, qkg, bd, cosf, sinf)


def _na_kernel(q_ref, k0, k1, k2, k3, k4, v0, v1, v2, v3, v4, kc_ref, vc_ref, bias_ref, o_ref):
    nq = q_ref.shape[0]
    lane = lax.broadcasted_iota(jnp.int32, (nq, LANE), 1)
    low = lane < NA_HEAD_DIM
    zero = jnp.zeros((), BF16)
    outs = []
    for p in range(NA_WIDTH // LANE):
        cs = slice(p * LANE, (p + 1) * LANE)
        qp = q_ref[:, cs]
        q01 = jnp.concatenate([jnp.where(low, qp, zero), jnp.where(low, zero, qp)], axis=0)
        k_all = jnp.concatenate([r[:, cs] for r in (k0, k1, k2, k3, k4, kc_ref)], axis=0)
        v_all = jnp.concatenate([r[:, cs] for r in (v0, v1, v2, v3, v4, vc_ref)], axis=0)
        v_all = jnp.concatenate([v_all, jnp.ones(v_all.shape, BF16)], axis=1)
        s = _dot_nt(q01, k_all)
        n_win = bias_ref.shape[-1]
        s_w = s[:, :n_win] + bias_ref[p]
        s_c = s[:, n_win:]
        m = jnp.maximum(jnp.max(s_w, axis=-1, keepdims=True), jnp.max(s_c, axis=-1, keepdims=True))
        probs = jnp.concatenate([jnp.exp(s_w - m), jnp.exp(s_c - m)], axis=1).astype(BF16)
        o01 = _dot(probs, v_all)
        o01 = o01[:, :LANE] / o01[:, LANE:]
        outs.append(jnp.where(low, o01[:nq], o01[nq:]).astype(o_ref.dtype))
    o_ref[...] = jnp.concatenate(outs, axis=1)


def _attention(big, bias, n_ctx, rows, ctx_out):
    bsz, s_len, _ = big.shape
    nq = NA_ROWS * GRID_W
    ncb = n_ctx // nq
    nrp = rows // NA_ROWS
    first = 0 if ctx_out else ncb
    n_win = NA_KROWS // NA_ROWS

    def lat(i):
        return jnp.maximum(i + first - ncb, 0)

    def kv_spec(j, col):
        return pl.BlockSpec(
            (None, nq, NA_WIDTH),
            lambda b, i: (b, ncb + jnp.clip(lat(i) - 2, 0, nrp - n_win) + j, col))

    def bias_type(b, i):
        step = i + first
        r = lat(i)
        t = jnp.where(r < 2, r, jnp.where(r >= nrp - 2, r - (nrp - 2) + 3, 2))
        return (jnp.where(step < ncb, 5, t), 0, 0, 0)

    return pl.pallas_call(
        _na_kernel,
        out_shape=jax.ShapeDtypeStruct((bsz, s_len, NA_WIDTH), BF16),
        grid=(bsz, ncb + nrp - first),
        in_specs=[pl.BlockSpec((None, nq, NA_WIDTH), lambda b, i: (b, i + first, 0))]
                 + [kv_spec(j, 1) for j in range(n_win)] + [kv_spec(j, 2) for j in range(n_win)]
                 + [pl.BlockSpec((None, n_ctx, NA_WIDTH), lambda b, i: (b, 0, 1)),
                    pl.BlockSpec((None, n_ctx, NA_WIDTH), lambda b, i: (b, 0, 2)),
                    pl.BlockSpec((None, NA_WIDTH // LANE, 2 * nq, NA_KROWS * GRID_W), bias_type)],
        out_specs=pl.BlockSpec((None, nq, NA_WIDTH), lambda b, i: (b, i + first, 0)),
        compiler_params=_cparams(("arbitrary", "arbitrary")),
        name="attention",
    )(big, *([big] * (2 * n_win + 2)), bias)


def _attention_bias(rpb, rows):
    nrp = rows // NA_ROWS
    n_win = NA_KROWS // NA_ROWS
    reps = np.array([0, 1, 2, nrp - 2, nrp - 1])
    qr = (NA_ROWS * reps)[:, None] + np.arange(NA_ROWS)[None, :]
    ws = NA_ROWS * np.clip(reps - 2, 0, nrp - n_win)
    kr = ws[:, None] + np.arange(NA_KROWS)[None, :]
    r0 = np.clip(qr - NA_WIN_H // 2, 0, rows - NA_WIN_H)
    ok_r = (kr[:, None, :] >= r0[:, :, None]) & (kr[:, None, :] < r0[:, :, None] + NA_WIN_H)
    dr = np.clip(kr[:, None, :] - qr[:, :, None] + (NA_WIN_H - 1), 0, 2 * NA_WIN_H - 2)
    qc = np.arange(GRID_W)
    c0 = np.clip(qc - NA_WIN_W // 2, 0, GRID_W - NA_WIN_W)
    kc = np.arange(GRID_W)
    ok_c = (kc[None, :] >= c0[:, None]) & (kc[None, :] < c0[:, None] + NA_WIN_W)
    dc = np.clip(kc[None, :] - qc[:, None] + (NA_WIN_W - 1), 0, 2 * NA_WIN_W - 2)
    sel_r = jnp.asarray(dr[..., None] == np.arange(2 * NA_WIN_H - 1), F32)
    sel_c = jnp.asarray(dc[..., None] == np.arange(2 * NA_WIN_W - 1), F32)
    hp = lax.Precision.HIGHEST
    g = jnp.einsum('hrc,taer->thaec', rpb.astype(F32), sel_r, precision=hp)
    g = jnp.einsum('thaec,qkc->thaqek', g, sel_c, precision=hp)
    ok = ok_r[:, None, :, None, :, None] & ok_c[None, None, None, :, None, :]
    g = jnp.where(ok, g, NEG)
    g = g.reshape(5, NA_HEADS * NA_ROWS * GRID_W, NA_KROWS * GRID_W)
    g = g.reshape(5, NA_WIDTH // LANE, 2 * NA_ROWS * GRID_W, NA_KROWS * GRID_W)
    masked = jnp.full((1,) + g.shape[1:], NEG, F32)
    return jnp.concatenate([g, masked], axis=0)


def _scan_chunk(d, t, ncc, nc):
    bwd = jnp.where(t < ncc, ncc - 1 - t, ncc + (nc - 1 - t))
    return jnp.where(d == 0, t, bwd)


def _precedes(d):
    ii = lax.broadcasted_iota(jnp.int32, (CHUNK, CHUNK), 0)
    jj = lax.broadcasted_iota(jnp.int32, (CHUNK, CHUNK), 1)
    return (jj - ii) * (1 - 2 * d) <= 0


def _cumsum_scan_order(prec, x):
    tri = jnp.where(prec, 1.0, 0.0).astype(BF16)
    w = x.shape[1]
    parts = _dot(tri, jnp.concatenate(_split3(x), axis=1))
    return parts[:, :w] + parts[:, w:2 * w] + parts[:, 2 * w:]


def _ssd_kernel(x_ref, xp_ref, xn_ref, sm_ref, cw_ref, cb_ref, par_ref, y_ref, st_ref, cbuf_ref,
                *, ncc, nc):
    d = pl.program_id(1)
    t = pl.program_id(2)
    chunk = _scan_chunk(d, t, ncc, nc)

    @pl.when(t == 0)
    def _():
        st_ref[...] = jnp.zeros_like(st_ref)

    prev_ok = jnp.where(jnp.logical_and(chunk != 0, chunk != ncc), 1.0, 0.0)
    next_ok = jnp.where(jnp.logical_and(chunk != ncc - 1, chunk != nc - 1), 1.0, 0.0)
    prec = _precedes(d)
    rep = SSM_HEADS // SSM_GROUPS
    nb = x_ref.shape[0]
    for bb in range(nb):
        cbuf_ref[bb, 0:HALO, :] = xp_ref[bb].astype(F32) * prev_ok
        cbuf_ref[bb, HALO:HALO + CHUNK, :] = x_ref[bb].astype(F32)
        cbuf_ref[bb, HALO + CHUNK:, :] = xn_ref[bb].astype(F32) * next_ok
    taps = [[cbuf_ref[bb, HALO - SSM_CONV // 2 + k:HALO - SSM_CONV // 2 + k + CHUNK, :]
             for k in range(SSM_CONV)] for bb in range(nb)]
    gates_in = [sm_ref[bb] for bb in range(nb)]
    states = [[st_ref[bb, h] for h in range(SSM_HEADS)] for bb in range(nb)]
    stores = []
    for bb in range(nb):
        acc = jnp.broadcast_to(cb_ref[...], (CHUNK, SSM_CONV_CH))
        for k in range(SSM_CONV):
            acc = acc + taps[bb][k] * cw_ref[k:k + 1, :]
        xc = acc * _sigmoid(acc)
        xs = xc[:, :SSM_WIDTH]
        bm = xc[:, SSM_WIDTH:SSM_WIDTH + SSM_BC]
        cm = xc[:, SSM_WIDTH + SSM_BC:]

        dt = _softplus(gates_in[bb] + par_ref[0:1, :])
        la = dt * (-jnp.exp(par_ref[1:2, :]))
        acum = _cumsum_scan_order(prec, la)
        acum_t = acum.T
        total = jnp.sum(la, axis=0, keepdims=True)

        for g in range(SSM_GROUPS):
            cg = cm[:, g * SSM_STATE:(g + 1) * SSM_STATE].astype(BF16)
            bg = bm[:, g * SSM_STATE:(g + 1) * SSM_STATE].astype(BF16)
            cb = _dot_nt(cg, bg)
            for hh in range(rep):
                h = g * rep + hh
                a_col = acum[:, h:h + 1]
                a_row = acum_t[h:h + 1, :]
                tot = total[:, h:h + 1]
                x_h = xs[:, h * SSM_HEAD_DIM:(h + 1) * SSM_HEAD_DIM]
                xdt = x_h * dt[:, h:h + 1]
                lmat = jnp.exp(jnp.where(prec, a_col - a_row, -jnp.inf))
                h_in = states[bb][h]
                y = _dot((cb * lmat).astype(BF16), xdt.astype(BF16))
                y = y + _dot(cg, h_in.astype(BF16)) * jnp.exp(a_col)
                y = y + x_h * par_ref[2:3, h:h + 1]
                xw = (xdt * jnp.exp(tot - a_col)).astype(BF16)
                stores.append((bb, h, y, jnp.exp(tot) * h_in + _dot_tn(bg, xw)))

    for bb, h, y, h_new in stores:
        y_ref[bb, :, h * SSM_HEAD_DIM:(h + 1) * SSM_HEAD_DIM] = y
        st_ref[bb, h] = h_new


def _scan_batch(bsz):
    return math.gcd(bsz, SCAN_BATCH)


def _ssd(big, small, conv_w, conv_b, par, n_ctx):
    bsz, s_len, _ = big.shape
    ncc, nc = n_ctx // CHUNK, s_len // CHUNK
    per = CHUNK // HALO
    nb = _scan_batch(bsz)
    ch = functools.partial(_scan_chunk, ncc=ncc, nc=nc)
    cblk = COL_XBC // SSM_CONV_CH
    return pl.pallas_call(
        functools.partial(_ssd_kernel, ncc=ncc, nc=nc),
        out_shape=jax.ShapeDtypeStruct((2, bsz, s_len, SSM_WIDTH), F32),
        grid=(bsz // nb, 2, nc),
        in_specs=[
            pl.BlockSpec((nb, CHUNK, SSM_CONV_CH), lambda b, d, t: (b, ch(d, t), cblk)),
            pl.BlockSpec((nb, HALO, SSM_CONV_CH),
                         lambda b, d, t: (b, jnp.maximum(ch(d, t) * per - 1, 0), cblk)),
            pl.BlockSpec((nb, HALO, SSM_CONV_CH),
                         lambda b, d, t: (b, jnp.minimum((ch(d, t) + 1) * per, nc * per - 1), cblk)),
            pl.BlockSpec((nb, CHUNK, LANE), lambda b, d, t: (b, ch(d, t), d)),
            pl.BlockSpec(conv_w.shape, lambda b, d, t: (0, 0)),
            pl.BlockSpec(conv_b.shape, lambda b, d, t: (0, 0)),
            pl.BlockSpec((None, 8, LANE), lambda b, d, t: (d, 0, 0)),
        ],
        out_specs=pl.BlockSpec((None, nb, CHUNK, SSM_WIDTH), lambda b, d, t: (d, b, ch(d, t), 0)),
        scratch_shapes=[pltpu.VMEM((nb, SSM_HEADS, SSM_STATE, SSM_HEAD_DIM), F32),
                        pltpu.VMEM((nb, CHUNK + 2 * HALO, SSM_CONV_CH), F32)],
        compiler_params=_cparams(("arbitrary", "arbitrary", "arbitrary")),
        name="ssd",
    )(big, big, big, small, conv_w, conv_b, par)


def _mlstm_kernel(q_ref, k_ref, v_ref, kt_ref, sm_ref, par_ref, y_ref, st_ref, m_ref, *, ncc, nc):
    d = pl.program_id(1)
    t = pl.program_id(2)

    @pl.when(t == 0)
    def _():
        st_ref[...] = jnp.zeros_like(st_ref)
        m_ref[...] = jnp.zeros_like(m_ref)

    prec = _precedes(d)
    nb = q_ref.shape[0]
    dh = ML_HEAD_DIM
    loaded = [(sm_ref[bb], q_ref[bb], k_ref[bb], v_ref[bb], kt_ref[bb],
               [(m_ref[bb, h][0:1, :], st_ref[bb, h]) for h in range(ML_HEADS)])
              for bb in range(nb)]
    sel_k = lax.broadcasted_iota(jnp.int32, (3 * LANE, ML_HEADS * LANE), 0) % LANE
    sel_h = lax.broadcasted_iota(jnp.int32, (3 * LANE, ML_HEADS * LANE), 1) // LANE
    sel_all = jnp.where(sel_k == 2 * ML_HEADS + sel_h, 1.0, 0.0).astype(BF16)
    low = lax.broadcasted_iota(jnp.int32, (CHUNK, LANE), 1) < dh
    ones_tile = jnp.ones((CHUNK, LANE), BF16)
    twice = lambda x: jnp.concatenate([x, x], axis=1)
    stores = []
    for bb in range(nb):
        sm, q_all, k_all, v_all, kt_all, states = loaded[bb]
        pre = sm + par_ref[0:1, :]
        lf = -_softplus(-pre)
        bcum = _cumsum_scan_order(prec, lf)
        bcum_t = bcum.T
        li_t = pre.T
        gsum = jnp.broadcast_to(jnp.sum(lf, axis=0, keepdims=True), (HALO, LANE))
        both = jnp.concatenate([bcum, gsum], axis=0)
        rep = (_dot(jnp.concatenate(_split3(both), axis=1), sel_all))
        tiles = []
        for h in range(ML_HEADS):
            hs = slice(h * dh, (h + 1) * dh)
            ps = slice((h // 2) * LANE, (h // 2 + 1) * LANE)
            ci, cf = ML_HEADS + h, 2 * ML_HEADS + h
            b_rep = rep[:CHUNK, h * LANE:(h + 1) * LANE]
            gt = rep[CHUNK:CHUNK + 1, h * LANE:(h + 1) * LANE]
            b_row = bcum_t[cf:cf + 1, :]
            li_row = li_t[ci:ci + 1, :]
            qh = q_all[:, hs]
            kh = k_all[:, hs]
            v_aug = jnp.concatenate([v_all[:, ps], ones_tile], axis=1)
            m_in, st_in = states[h]

            log_d = jnp.where(prec, b_rep - b_row + li_row, -jnp.inf)
            log_inter = b_rep + m_in
            m_i = jnp.maximum(jnp.max(log_d, axis=-1, keepdims=True), log_inter)
            s = _dot_nt(qh, kh) * jnp.exp(log_d - m_i)
            inter = jnp.exp(log_inter - m_i)
            numden = _dot(s.astype(BF16), v_aug) + _dot(qh, st_in.astype(BF16)) * twice(inter)
            tiles.append(numden[:, :LANE] / jnp.maximum(jnp.abs(numden[:, LANE:]), jnp.exp(-m_i)))

            w_row = gt - b_row + li_row
            m_loc = jnp.max(w_row, axis=-1, keepdims=True)
            ket = (kt_all[hs, :].astype(F32) * jnp.exp(w_row - m_loc)).astype(BF16)
            m_new = jnp.maximum(gt + m_in, m_loc)
            a_old = jnp.exp(gt + m_in - m_new)
            a_new = jnp.exp(m_loc - m_new)
            st_new = twice(a_old) * st_in + twice(a_new) * _dot(ket, v_aug)
            stores.append((bb, h, st_new, m_new))
        for p in range(ML_HEADS // 2):
            y_ref[bb, :, p * LANE:(p + 1) * LANE] = jnp.where(low, tiles[2 * p], tiles[2 * p + 1])

    for bb, h, st_new, m_new in stores:
        st_ref[bb, h] = st_new
        m_ref[bb, h] = jnp.broadcast_to(m_new, m_ref.shape[2:])


def _mlstm(big, kt, small, par, n_ctx):
    bsz, s_len, _ = big.shape
    ncc, nc = n_ctx // CHUNK, s_len // CHUNK
    nb = _scan_batch(bsz)
    ch = functools.partial(_scan_chunk, ncc=ncc, nc=nc)

    def col_spec(col):
        return pl.BlockSpec((nb, CHUNK, ML_WIDTH), lambda b, d, t: (b, ch(d, t), col // ML_WIDTH))

    return pl.pallas_call(
        functools.partial(_mlstm_kernel, ncc=ncc, nc=nc),
        out_shape=jax.ShapeDtypeStruct((2, bsz, s_len, ML_WIDTH), F32),
        grid=(bsz // nb, 2, nc),
        in_specs=[col_spec(COL_MQ), col_spec(COL_MK), col_spec(COL_MV),
                  pl.BlockSpec((nb, ML_WIDTH, CHUNK), lambda b, d, t: (b, 0, ch(d, t))),
                  pl.BlockSpec((nb, CHUNK, LANE), lambda b, d, t: (b, ch(d, t), d)),
                  pl.BlockSpec((None, 8, LANE), lambda b, d, t: (d, 0, 0))],
        out_specs=pl.BlockSpec((None, nb, CHUNK, ML_WIDTH), lambda b, d, t: (d, b, ch(d, t), 0)),
        scratch_shapes=[pltpu.VMEM((nb, ML_HEADS, ML_HEAD_DIM, 2 * LANE), F32),
                        pltpu.VMEM((nb, ML_HEADS, 8, LANE), F32)],
        compiler_params=_cparams(("arbitrary", "arbitrary", "arbitrary")),
        name="mlstm",
    )(big, big, big, kt, small, par)


def _merge_kernel(ya_ref, yb_ref, z_ref, yc_ref, o_ref, h_ref, g1_ref, sh_ref, sc_ref, ng_ref,
                  sg_ref, mg_ref, bd_ref, wo_ref, rwh_ref, rwl_ref, rb_ref,
                  hout_ref, v_ref, gates_ref):
    z = z_ref[...].astype(F32)
    yb = (yb_ref[0] + yb_ref[1]) * (z * _sigmoid(z))
    yb = yb * lax.rsqrt(jnp.mean(yb * yb, axis=-1, keepdims=True) + EPS) * sg_ref[...]
    yc = _sigmoid(o_ref[...].astype(F32)) * (yc_ref[0] + yc_ref[1])
    yc = yc * lax.rsqrt(_segment_mean_sq(yc, bd_ref[...]) + EPS) * mg_ref[...]
    y = _dot(jnp.concatenate([ya_ref[...], yb.astype(BF16), yc.astype(BF16)], axis=1), wo_ref[...])
    hn = h_ref[...] + g1_ref[...] * y
    hout_ref[...] = hn
    v = hn * lax.rsqrt(jnp.mean(hn * hn, axis=-1, keepdims=True) + EPS) * ng_ref[...]
    v = v * (1.0 + sc_ref[...]) + sh_ref[...]
    v_hi, v_lo = _split2(v)
    v_ref[...] = v_hi
    rwh = rwh_ref[...]
    logits = (_dot_nt(rwh, v_hi) + _dot_nt(rwh, v_lo) + _dot_nt(rwl_ref[...], v_hi))[:N_EXPERTS] + rb_ref[...]

    row = lax.broadcasted_iota(jnp.int32, logits.shape, 0).astype(F32)
    work = logits
    top = None
    gates = jnp.zeros_like(logits)
    den = jnp.zeros((1, logits.shape[1]), F32)
    for _ in range(TOP_K):
        mx = jnp.max(work, axis=0, keepdims=True)
        idx = jnp.min(jnp.where(work == mx, row, float(N_EXPERTS)), axis=0, keepdims=True)
        hit = row == idx
        top = mx if top is None else top
        e = jnp.exp(mx - top)
        gates = gates + jnp.where(hit, e, 0.0)
        den = den + e
        work = jnp.where(hit, -jnp.inf, work)
    gates_ref[...] = jnp.concatenate(
        [gates / den, jnp.zeros((LANE - N_EXPERTS, logits.shape[1]), F32)], axis=0)


def _merge(ya, yb, yc, big, h, g1, sh2, sc2, ng, sg, mg, bd, wo, rwh, rwl, rb, n_ctx, ctx_out):
    bsz, s_len, d = h.shape
    tm = TOK_TILE
    first = 0 if ctx_out else n_ctx // tm
    n_out = s_len - first * tm
    mod_spec = pl.BlockSpec((None, None, 1, d), lambda b, i: (b, jnp.minimum(i + first, 1), 0, 0))
    full = lambda shape: pl.BlockSpec(shape, lambda b, i: (0,) * len(shape))
    pair = lambda w: pl.BlockSpec((2, None, tm, w), lambda b, i: (0, b, i + first, 0))
    nl = (s_len - n_ctx) // tm

    def flat_row(b, i):
        if ctx_out:
            return jnp.where(i == 0, bsz * nl + b, (i - 1) * bsz + b)
        return i * bsz + b

    flat_tok = lambda w: pl.BlockSpec((tm, w), lambda b, i: (flat_row(b, i), 0))
    return pl.pallas_call(
        _merge_kernel,
        out_shape=(jax.ShapeDtypeStruct((bsz, n_out, d), F32),
                   jax.ShapeDtypeStruct((bsz * n_out, d), BF16),
                   jax.ShapeDtypeStruct((LANE, bsz * n_out), F32)),
        grid=(bsz, n_out // tm),
        in_specs=[pl.BlockSpec((None, tm, NA_WIDTH), lambda b, i: (b, i + first, 0)),
                  pair(SSM_WIDTH),
                  pl.BlockSpec((None, tm, SSM_WIDTH), lambda b, i: (b, i + first, COL_Z // SSM_WIDTH)),
                  pair(ML_WIDTH),
                  pl.BlockSpec((None, tm, ML_WIDTH), lambda b, i: (b, i + first, COL_MO // ML_WIDTH)),
                  pl.BlockSpec((None, tm, d), lambda b, i: (b, i + first, 0)),
                  mod_spec, mod_spec, mod_spec,
                  full((1, d)), full((1, SSM_WIDTH)), full((1, ML_WIDTH)), full(bd.shape),
                  full(wo.shape), full(rwh.shape), full(rwl.shape), full(rb.shape)],
        out_specs=(pl.BlockSpec((None, tm, d), lambda b, i: (b, i, 0)), flat_tok(d),
                   pl.BlockSpec((LANE, tm), lambda b, i: (0, flat_row(b, i)))),
        compiler_params=_cparams(("arbitrary", "arbitrary")),
        name="merge",
    )(ya, yb, big, yc, big, h, g1, sh2, sc2, ng, sg, mg, bd, wo, rwh, rwl, rb)


def _moe_kernel(x_ref, gates_ref, w1_ref, b1_ref, w2_ref, b2_ref, f_ref, rank_ref, *, nblk):
    e = pl.program_id(1)
    tb = x_ref.shape[0] // nblk
    d_ff = w2_ref.shape[0]

    @pl.when(e == 0)
    def _():
        f_ref[...] = jnp.zeros_like(f_ref)
        ii = lax.broadcasted_iota(jnp.int32, (tb, tb), 0)
        jj = lax.broadcasted_iota(jnp.int32, (tb, tb), 1)
        tri = jnp.where(ii <= jj, 1.0, 0.0).astype(BF16)
        for k in range(nblk):
            picked = gates_ref[:, k * tb:(k + 1) * tb] > 0.0
            incl = _dot(jnp.where(picked, 1.0, 0.0).astype(BF16), tri)
            rank_ref[k] = jnp.where(picked, incl - 1.0, -1.0)

    def compact(k, s):
        slot = rank_ref[k, pl.ds(e, 1), :]
        gate = gates_ref[pl.ds(e, 1), k * tb:(k + 1) * tb]
        row = (lax.broadcasted_iota(jnp.int32, (MOE_CAP, tb), 0) + s * MOE_CAP).astype(F32)
        hit = slot == row
        onehot = jnp.where(hit, 1.0, 0.0).astype(BF16)
        xe = _dot(onehot, x_ref[k * tb:(k + 1) * tb, :]).astype(BF16)
        w = jnp.sum(jnp.where(hit, gate, 0.0), axis=-1, keepdims=True)
        return onehot, xe, w

    def expert(xe):
        hid = _dot(xe, w1_ref[...]) + b1_ref[...]
        x_glu = jnp.minimum(hid[:, :d_ff], SWIGLU_LIMIT)
        x_lin = jnp.clip(hid[:, d_ff:], -SWIGLU_LIMIT, SWIGLU_LIMIT)
        act = x_glu * _sigmoid(SWIGLU_ALPHA * x_glu) * (x_lin + 1.0)
        return _dot(act.astype(BF16), w2_ref[...]) + b2_ref[...]

    def scatter(k, onehot, yw):
        f_ref[k * tb:(k + 1) * tb, :] += _dot_tn(onehot, yw.astype(BF16))

    first = [compact(k, 0) for k in range(nblk)]
    y = expert(jnp.concatenate([xe for _, xe, _ in first], axis=0))
    for k, (onehot, _, w) in enumerate(first):
        scatter(k, onehot, y[k * MOE_CAP:(k + 1) * MOE_CAP] * w)

    for k in range(nblk):
        count = jnp.max(rank_ref[k, pl.ds(e, 1), :]).astype(jnp.int32) + 1

        def overflow(s, carry, k=k):
            onehot, xe, w = compact(k, s)
            scatter(k, onehot, expert(xe) * w)
            return carry

        lax.fori_loop(1, (count + MOE_CAP - 1) // MOE_CAP, overflow, 0)


def _moe(x, gates, w1, b1, w2, b2, layer, first_tok, n_tok):
    d = x.shape[1]
    nblk = MOE_GROUP if n_tok % (MOE_GROUP * MOE_BLOCK) == 0 else 1
    tb = math.gcd(n_tok, MOE_BLOCK)
    rows = nblk * tb
    d_ff = w2.shape[1]
    off = first_tok // rows
    assert first_tok % rows == 0 and n_tok % rows == 0
    expert = lambda i, e: (layer * N_EXPERTS + e, 0, 0)
    return pl.pallas_call(
        functools.partial(_moe_kernel, nblk=nblk),
        out_shape=jax.ShapeDtypeStruct((n_tok, d), F32),
        grid=(n_tok // rows, N_EXPERTS),
        in_specs=[pl.BlockSpec((rows, d), lambda i, e: (i + off, 0)),
                  pl.BlockSpec((LANE, rows), lambda i, e: (0, i + off)),
                  pl.BlockSpec((None, d, 2 * d_ff), expert),
                  pl.BlockSpec((None, 1, 2 * d_ff), expert),
                  pl.BlockSpec((None, d_ff, d), expert),
                  pl.BlockSpec((None, 1, d), expert)],
        out_specs=pl.BlockSpec((rows, d), lambda i, e: (i, 0)),
        scratch_shapes=[pltpu.VMEM((nblk, LANE, tb), F32)],
        compiler_params=pltpu.CompilerParams(dimension_semantics=("arbitrary", "arbitrary"),
                                             vmem_limit_bytes=MOE_VMEM_LIMIT),
        name="moe",
    )(x, gates, w1, b1, w2, b2)


def _resid_kernel(h_ref, g_ref, fl_ref, *rest, first):
    if first == 0:
        fc_ref, o_ref = rest
        f = jnp.where(pl.program_id(1) == 0, fc_ref[...], fl_ref[...])
    else:
        (o_ref,) = rest
        f = fl_ref[...]
    o_ref[...] = h_ref[...] + g_ref[...] * f


def _resid(h, g2, f_lat, f_ctx):
    bsz, n_tok, d = h.shape
    tm = TOK_TILE
    first = 0 if f_ctx is not None else 1
    in_specs = [pl.BlockSpec((None, tm, d), lambda b, i: (b, i, 0)),
                pl.BlockSpec((None, None, 1, d), lambda b, i: (b, jnp.minimum(i + first, 1), 0, 0)),
                pl.BlockSpec((tm, d), lambda b, i: (jnp.maximum(i + first - 1, 0) * bsz + b, 0))]
    args = [h, g2, f_lat]
    if f_ctx is not None:
        in_specs.append(pl.BlockSpec((tm, d), lambda b, i: (b, 0)))
        args.append(f_ctx)
    return pl.pallas_call(
        functools.partial(_resid_kernel, first=first),
        out_shape=jax.ShapeDtypeStruct((bsz, n_tok, d), F32),
        grid=(bsz, n_tok // tm),
        in_specs=in_specs,
        out_specs=pl.BlockSpec((None, tm, d), lambda b, i: (b, i, 0)),
        compiler_params=_cparams(("arbitrary", "arbitrary")),
        name="resid",
    )(*args)


def _in_weight_perm():
    o = IN_OFFSETS
    big = np.concatenate([np.arange(o[0], o[5]), np.arange(o[6], o[10])])
    small = np.full((2, LANE), -1, np.int64)
    for d in range(2):
        for h in range(SSM_HEADS):
            small[d, h] = o[5] + d * SSM_HEADS + h
        for h in range(ML_HEADS):
            small[d, ML_HEADS + h] = o[10] + d * ML_HEADS + h
            small[d, 2 * ML_HEADS + h] = o[11] + d * ML_HEADS + h
    return np.concatenate([big, small.reshape(-1)])


def _rope_tables(n_ctx, n_lat):
    pos = np.arange(n_lat)
    row = (pos // GRID_W).astype(np.float32)
    col = (pos % GRID_W).astype(np.float32)
    n_freq = ML_HEAD_DIM // 4
    inv_freq = jnp.asarray(ROPE_THETA, F32) ** (-jnp.arange(n_freq, dtype=F32) / n_freq)
    ang = jnp.concatenate([jnp.asarray(row)[:, None] * inv_freq, jnp.asarray(col)[:, None] * inv_freq], axis=-1)
    cos, sin = jnp.cos(ang), jnp.sin(ang)
    cosf = jnp.tile(jnp.concatenate([cos, cos], axis=-1), (1, ML_HEADS))
    sinf = jnp.tile(jnp.concatenate([-sin, sin], axis=-1), (1, ML_HEADS))
    cosf = jnp.concatenate([jnp.ones((n_ctx, ML_WIDTH), F32), cosf], axis=0)
    sinf = jnp.concatenate([jnp.zeros((n_ctx, ML_WIDTH), F32), sinf], axis=0)
    return cosf, sinf


def _lane_rows(rows):
    out = jnp.zeros((8, LANE), F32)
    for r, (off, vec) in enumerate(rows):
        out = out.at[r, off:off + vec.shape[0]].set(vec.astype(F32))
    return out


def kernel(x, c, ctx, c_ctx, ada_w, ada_b, norm_g, w_in, na_qk_g, na_rpb, ssm_conv_w, ssm_conv_b,
           ssm_dt_bias, ssm_a_log, ssm_d, ssm_norm_g, ml_gate_b, ml_norm_g, w_out,
           router_w, router_b, exp_w1, exp_b1, exp_w2, exp_b2):
    bsz, n_lat, d = x.shape
    n_ctx = ctx.shape[1]
    rows = n_lat // GRID_W
    depth = w_in.shape[0]
    assert bsz < 8 and n_ctx == TOK_TILE and n_lat % TOK_TILE == 0 and rows >= NA_KROWS

    h = jnp.concatenate([ctx, x], axis=1)
    cond = jnp.zeros((8, d), F32).at[:bsz].set(c).at[bsz].set(c_ctx)
    cosf, sinf = _rope_tables(n_ctx, n_lat)
    perm = _in_weight_perm()
    seg = np.arange(2 * LANE) // NA_HEAD_DIM
    bd = jnp.asarray(seg[:, None] == seg[None, :], BF16)

    n_exp = exp_w1.shape[0] * exp_w1.shape[1]
    w1_all = exp_w1.astype(BF16).reshape((n_exp,) + exp_w1.shape[2:])
    w2_all = exp_w2.astype(BF16).reshape((n_exp,) + exp_w2.shape[2:])
    b1_all = exp_b1.reshape(n_exp, 1, -1)
    b2_all = exp_b2.reshape(n_exp, 1, -1)

    out = None
    for layer in range(depth):
        ctx_out = layer < depth - 1
        mod = _ada(cond, ada_w[layer], ada_b[layer])
        mods = []
        for m in jnp.split(mod, 6, axis=-1):
            lat = m[:bsz]
            cx = jnp.broadcast_to(m[bsz][None], (bsz, d))
            mods.append(jnp.stack([cx, lat], axis=1).reshape(bsz, 2, 1, d))
        sh1, sc1, g1, sh2, sc2, g2 = mods

        w = jnp.where(perm[None, :] >= 0, w_in[layer][:, np.maximum(perm, 0)], 0.0).astype(BF16)
        qkg = jnp.concatenate([jnp.tile(na_qk_g[layer, 0] * NA_HEAD_DIM ** -0.5, NA_HEADS),
                               jnp.tile(na_qk_g[layer, 1], NA_HEADS)]).reshape(1, 2 * NA_WIDTH)
        big, small, kt = _inproj(h, sh1, sc1, norm_g[layer, 0].reshape(1, d), w, qkg, bd, cosf, sinf)

        bias = _attention_bias(na_rpb[layer], rows)
        ya = _attention(big, bias, n_ctx, rows, ctx_out)

        conv_w = jnp.zeros((8, SSM_CONV_CH), F32).at[:SSM_CONV].set(ssm_conv_w[layer])
        ssm_par = jnp.stack([_lane_rows([(0, ssm_dt_bias[layer, dr]), (0, ssm_a_log[layer, dr]),
                                         (0, ssm_d[layer, dr])]) for dr in range(2)])
        yb = _ssd(big, small, conv_w, ssm_conv_b[layer].reshape(1, -1), ssm_par, n_ctx)

        gate_row = [jnp.concatenate([jnp.zeros((ML_HEADS,), F32), ml_gate_b[layer, dr, 0],
                                     ml_gate_b[layer, dr, 1]]) for dr in range(2)]
        ml_par = jnp.stack([_lane_rows([(0, gate_row[dr])]) for dr in range(2)])
        yc = _mlstm(big, kt, small, ml_par, n_ctx)

        rw = jnp.zeros((LANE, d), F32).at[:N_EXPERTS].set(router_w[layer].T)
        rwh = rw.astype(BF16)
        rwl = (rw - rwh.astype(F32)).astype(BF16)
        rb = jnp.broadcast_to(router_b[layer].astype(F32)[:, None], (N_EXPERTS, TOK_TILE))
        hn, v, gates = _merge(ya, yb, yc, big, h, g1, sh2, sc2, norm_g[layer, 1].reshape(1, d),
                              ssm_norm_g[layer].reshape(1, -1), ml_norm_g[layer].reshape(1, -1), bd,
                              w_out[layer].astype(BF16), rwh, rwl, rb, n_ctx, ctx_out)

        n_lat_tok = bsz * n_lat
        f_lat = _moe(v, gates, w1_all, b1_all, w2_all, b2_all, layer, 0, n_lat_tok)
        f_ctx = (_moe(v, gates, w1_all, b1_all, w2_all, b2_all, layer, n_lat_tok, bsz * n_ctx)
                 if ctx_out else None)
        h = _resid(hn, g2, f_lat, f_ctx)
        out = h
    return out
```

```python
import functools
import math

import numpy as np
import jax
import jax.numpy as jnp
from jax import lax
from jax.experimental import pallas as pl
from jax.experimental.pallas import tpu as pltpu

F32 = jnp.float32
BF16 = jnp.bfloat16

GRID_W = 64
EPS = 1e-6
NA_HEADS = 8
NA_HEAD_DIM = 64
NA_WIN_H = 8
NA_WIN_W = 16
SSM_HEADS = 4
SSM_HEAD_DIM = 64
SSM_GROUPS = 2
SSM_STATE = 128
SSM_CONV = 5
ML_HEADS = 4
ML_HEAD_DIM = 64
ROPE_THETA = 10000.0
N_EXPERTS = 32
TOP_K = 4
SWIGLU_ALPHA = 1.702
SWIGLU_LIMIT = 7.0

NA_WIDTH = NA_HEADS * NA_HEAD_DIM
SSM_WIDTH = SSM_HEADS * SSM_HEAD_DIM
SSM_BC = SSM_GROUPS * SSM_STATE
SSM_CONV_CH = SSM_WIDTH + 2 * SSM_BC
ML_WIDTH = ML_HEADS * ML_HEAD_DIM
IN_SPLITS = (NA_WIDTH, NA_WIDTH, NA_WIDTH, SSM_CONV_CH, SSM_WIDTH, 2 * SSM_HEADS,
             ML_WIDTH, ML_WIDTH, ML_WIDTH, ML_WIDTH, 2 * ML_HEADS, 2 * ML_HEADS)
IN_OFFSETS = tuple(int(o) for o in np.cumsum((0,) + IN_SPLITS))

LANE = 128
CHUNK = 128
HALO = 16
TOK_TILE = 256
NA_ROWS = 2
NA_KROWS = NA_ROWS + NA_WIN_H
BIG_W = 3 * NA_WIDTH + SSM_CONV_CH + SSM_WIDTH + 4 * ML_WIDTH
SMALL_W = 2 * LANE
PLAIN_COLS = 768
MOE_BLOCK = 1024
MOE_GROUP = 2
MOE_CAP = 160
SCAN_BATCH = 4
NEG = -1e30
VMEM_LIMIT = 48 * 1024 * 1024
MOE_VMEM_LIMIT = 56 * 1024 * 1024

COL_Q, COL_K, COL_V = 0, NA_WIDTH, 2 * NA_WIDTH
COL_XBC = 3 * NA_WIDTH
COL_Z = COL_XBC + SSM_CONV_CH
COL_MQ = COL_Z + SSM_WIDTH
COL_MK = COL_MQ + ML_WIDTH
COL_MV = COL_MK + ML_WIDTH
COL_MO = COL_MV + ML_WIDTH


def _cparams(sem):
    return pltpu.CompilerParams(dimension_semantics=sem, vmem_limit_bytes=VMEM_LIMIT)


def _sigmoid(x):
    return 1.0 / (1.0 + jnp.exp(-x))


def _softplus(x):
    return jnp.maximum(x, 0.0) + jnp.log1p(jnp.exp(-jnp.abs(x)))


def _dot(a, b):
    return jnp.dot(a, b, preferred_element_type=F32)


def _dot_nt(a, b):
    return lax.dot_general(a, b, (((1,), (1,)), ((), ())), preferred_element_type=F32)


def _dot_tn(a, b):
    return lax.dot_general(a, b, (((0,), (0,)), ((), ())), preferred_element_type=F32)


def _split3(x):
    x1 = x.astype(BF16)
    r1 = x - x1.astype(F32)
    x2 = r1.astype(BF16)
    x3 = (r1 - x2.astype(F32)).astype(BF16)
    return x1, x2, x3


def _split2(x):
    x1 = x.astype(BF16)
    return x1, (x - x1.astype(F32)).astype(BF16)


def _segment_mean_sq(x, bd):
    hi, lo = _split2(x * x)
    return (_dot(hi, bd) + _dot(lo, bd)) * (1.0 / NA_HEAD_DIM)


def _ada_kernel(c_ref, w_ref, b_ref, o_ref):
    c = c_ref[...]
    s = c * _sigmoid(c)
    o_ref[...] = jnp.dot(s, w_ref[...], preferred_element_type=F32,
                         precision=lax.Precision.HIGHEST) + b_ref[...]


def _ada(cond, w, b):
    n, d = cond.shape
    wcols = w.shape[1]
    bn = wcols // 4
    return pl.pallas_call(
        _ada_kernel,
        out_shape=jax.ShapeDtypeStruct((n, wcols), F32),
        grid=(4,),
        in_specs=[pl.BlockSpec((n, d), lambda j: (0, 0)),
                  pl.BlockSpec((d, bn), lambda j: (0, j)),
                  pl.BlockSpec((1, bn), lambda j: (0, j))],
        out_specs=pl.BlockSpec((n, bn), lambda j: (0, j)),
        compiler_params=_cparams(("arbitrary",)),
        name="ada",
    )(cond, w, b.reshape(1, wcols))


def _inproj_kernel(h_ref, sh_ref, sc_ref, g_ref, w_ref, qkg_ref, bd_ref, cos_ref, sin_ref,
                   big_ref, small_ref, kt_ref):
    x = h_ref[...]
    ms = jnp.mean(x * x, axis=-1, keepdims=True)
    u = x * lax.rsqrt(ms + EPS) * g_ref[...]
    u = u * (1.0 + sc_ref[...]) + sh_ref[...]
    ub = u.astype(BF16)
    bd = bd_ref[...]

    for c0 in range(0, 2 * NA_WIDTH, NA_WIDTH):
        wide = _dot(ub, w_ref[:, c0:c0 + NA_WIDTH])
        for c1 in range(0, NA_WIDTH, 2 * LANE):
            acc = wide[:, c1:c1 + 2 * LANE]
            msq = _segment_mean_sq(acc, bd)
            y = acc * lax.rsqrt(msq + EPS) * qkg_ref[:, c0 + c1:c0 + c1 + 2 * LANE]
            big_ref[:, c0 + c1:c0 + c1 + 2 * LANE] = y.astype(BF16)
    for c0 in range(COL_V, COL_MQ, PLAIN_COLS):
        big_ref[:, c0:c0 + PLAIN_COLS] = _dot(ub, w_ref[:, c0:c0 + PLAIN_COLS]).astype(BF16)
    lane = lax.broadcasted_iota(jnp.int32, (x.shape[0], LANE), 1)
    first_half = (lane % ML_HEAD_DIM) < (ML_HEAD_DIM // 2)
    for base, scale in ((COL_MQ, ML_HEAD_DIM ** -0.5), (COL_MK, 1.0)):
        for j in range(ML_WIDTH // LANE):
            c0 = base + j * LANE
            acc = _dot(ub, w_ref[:, c0:c0 + LANE])
            swapped = jnp.where(first_half,
                                pltpu.roll(acc, LANE - ML_HEAD_DIM // 2, 1),
                                pltpu.roll(acc, ML_HEAD_DIM // 2, 1))
            y = acc * cos_ref[:, j * LANE:(j + 1) * LANE] + swapped * sin_ref[:, j * LANE:(j + 1) * LANE]
            big_ref[:, c0:c0 + LANE] = (y * scale).astype(BF16)
            if base == COL_MK:
                kt_ref[j * LANE:(j + 1) * LANE, :] = y.T.astype(BF16)
    big_ref[:, COL_MV:BIG_W] = _dot(ub, w_ref[:, COL_MV:BIG_W]).astype(BF16)
    small_ref[...] = _dot(ub, w_ref[:, BIG_W:BIG_W + SMALL_W])


def _inproj(h, sh, sc, g, w, qkg, bd, cosf, sinf):
    bsz, s_len, d = h.shape
    tm = TOK_TILE
    mod_spec = pl.BlockSpec((None, None, 1, d), lambda b, i: (b, jnp.minimum(i, 1), 0, 0))
    full = lambda shape: pl.BlockSpec(shape, lambda b, i: (0,) * len(shape))
    return pl.pallas_call(
        _inproj_kernel,
        out_shape=(jax.ShapeDtypeStruct((bsz, s_len, BIG_W), BF16),
                   jax.ShapeDtypeStruct((bsz, s_len, SMALL_W), F32),
                   jax.ShapeDtypeStruct((bsz, ML_WIDTH, s_len), BF16)),
        grid=(bsz, s_len // tm),
        in_specs=[pl.BlockSpec((None, tm, d), lambda b, i: (b, i, 0)),
                  mod_spec, mod_spec, full((1, d)), full(w.shape), full(qkg.shape), full(bd.shape),
                  pl.BlockSpec((tm, ML_WIDTH), lambda b, i: (i, 0)),
                  pl.BlockSpec((tm, ML_WIDTH), lambda b, i: (i, 0))],
        out_specs=(pl.BlockSpec((None, tm, BIG_W), lambda b, i: (b, i, 0)),
                   pl.BlockSpec((None, tm, SMALL_W), lambda b, i: (b, i, 0)),
                   pl.BlockSpec((None, ML_WIDTH, tm), lambda b, i: (b, 0, i))),
        compiler_params=_cparams(("arbitrary", "arbitrary")),
        name="inproj",
    )(h, sh, sc, g, w, qkg, bd, cosf, sinf)


def _na_kernel(q_ref, k0, k1, k2, k3, k4, v0, v1, v2, v3, v4, kc_ref, vc_ref, bias_ref, o_ref):
    nq = q_ref.shape[0]
    lane = lax.broadcasted_iota(jnp.int32, (nq, LANE), 1)
    low = lane < NA_HEAD_DIM
    zero = jnp.zeros((), BF16)
    outs = []
    for p in range(NA_WIDTH // LANE):
        cs = slice(p * LANE, (p + 1) * LANE)
        qp = q_ref[:, cs]
        q01 = jnp.concatenate([jnp.where(low, qp, zero), jnp.where(low, zero, qp)], axis=0)
        k_all = jnp.concatenate([r[:, cs] for r in (k0, k1, k2, k3, k4, kc_ref)], axis=0)
        v_all = jnp.concatenate([r[:, cs] for r in (v0, v1, v2, v3, v4, vc_ref)], axis=0)
        v_all = jnp.concatenate([v_all, jnp.ones(v_all.shape, BF16)], axis=1)
        s = _dot_nt(q01, k_all)
        n_win = bias_ref.shape[-1]
        s_w = s[:, :n_win] + bias_ref[p]
        s_c = s[:, n_win:]
        m = jnp.maximum(jnp.max(s_w, axis=-1, keepdims=True), jnp.max(s_c, axis=-1, keepdims=True))
        probs = jnp.concatenate([jnp.exp(s_w - m), jnp.exp(s_c - m)], axis=1).astype(BF16)
        o01 = _dot(probs, v_all)
        o01 = o01[:, :LANE] / o01[:, LANE:]
        outs.append(jnp.where(low, o01[:nq], o01[nq:]).astype(o_ref.dtype))
    o_ref[...] = jnp.concatenate(outs, axis=1)


def _attention(big, bias, n_ctx, rows, ctx_out):
    bsz, s_len, _ = big.shape
    nq = NA_ROWS * GRID_W
    ncb = n_ctx // nq
    nrp = rows // NA_ROWS
    first = 0 if ctx_out else ncb
    n_win = NA_KROWS // NA_ROWS

    def lat(i):
        return jnp.maximum(i + first - ncb, 0)

    def kv_spec(j, col):
        return pl.BlockSpec(
            (None, nq, NA_WIDTH),
            lambda b, i: (b, ncb + jnp.clip(lat(i) - 2, 0, nrp - n_win) + j, col))

    def bias_type(b, i):
        step = i + first
        r = lat(i)
        t = jnp.where(r < 2, r, jnp.where(r >= nrp - 2, r - (nrp - 2) + 3, 2))
        return (jnp.where(step < ncb, 5, t), 0, 0, 0)

    return pl.pallas_call(
        _na_kernel,
        out_shape=jax.ShapeDtypeStruct((bsz, s_len - first * nq, NA_WIDTH), BF16),
        grid=(bsz, ncb + nrp - first),
        in_specs=[pl.BlockSpec((None, nq, NA_WIDTH), lambda b, i: (b, i + first, 0))]
                 + [kv_spec(j, 1) for j in range(n_win)] + [kv_spec(j, 2) for j in range(n_win)]
                 + [pl.BlockSpec((None, n_ctx, NA_WIDTH), lambda b, i: (b, 0, 1)),
                    pl.BlockSpec((None, n_ctx, NA_WIDTH), lambda b, i: (b, 0, 2)),
                    pl.BlockSpec((None, NA_WIDTH // LANE, 2 * nq, NA_KROWS * GRID_W), bias_type)],
        out_specs=pl.BlockSpec((None, nq, NA_WIDTH), lambda b, i: (b, i, 0)),
        compiler_params=_cparams(("arbitrary", "arbitrary")),
        name="attention",
    )(big, *([big] * (2 * n_win + 2)), bias)


def _attention_bias(rpb, rows):
    nrp = rows // NA_ROWS
    n_win = NA_KROWS // NA_ROWS
    reps = np.array([0, 1, 2, nrp - 2, nrp - 1])
    qr = (NA_ROWS * reps)[:, None] + np.arange(NA_ROWS)[None, :]
    ws = NA_ROWS * np.clip(reps - 2, 0, nrp - n_win)
    kr = ws[:, None] + np.arange(NA_KROWS)[None, :]
    r0 = np.clip(qr - NA_WIN_H // 2, 0, rows - NA_WIN_H)
    ok_r = (kr[:, None, :] >= r0[:, :, None]) & (kr[:, None, :] < r0[:, :, None] + NA_WIN_H)
    dr = np.clip(kr[:, None, :] - qr[:, :, None] + (NA_WIN_H - 1), 0, 2 * NA_WIN_H - 2)
    qc = np.arange(GRID_W)
    c0 = np.clip(qc - NA_WIN_W // 2, 0, GRID_W - NA_WIN_W)
    kc = np.arange(GRID_W)
    ok_c = (kc[None, :] >= c0[:, None]) & (kc[None, :] < c0[:, None] + NA_WIN_W)
    dc = np.clip(kc[None, :] - qc[:, None] + (NA_WIN_W - 1), 0, 2 * NA_WIN_W - 2)
    sel_r = jnp.asarray(dr[..., None] == np.arange(2 * NA_WIN_H - 1), F32)
    sel_c = jnp.asarray(dc[..., None] == np.arange(2 * NA_WIN_W - 1), F32)
    hp = lax.Precision.HIGHEST
    g = jnp.einsum('hrc,taer->thaec', rpb.astype(F32), sel_r, precision=hp)
    g = jnp.einsum('thaec,qkc->thaqek', g, sel_c, precision=hp)
    ok = ok_r[:, None, :, None, :, None] & ok_c[None, None, None, :, None, :]
    g = jnp.where(ok, g, NEG)
    g = g.reshape(5, NA_HEADS * NA_ROWS * GRID_W, NA_KROWS * GRID_W)
    g = g.reshape(5, NA_WIDTH // LANE, 2 * NA_ROWS * GRID_W, NA_KROWS * GRID_W)
    masked = jnp.full((1,) + g.shape[1:], NEG, F32)
    return jnp.concatenate([g, masked], axis=0)


def _scan_chunk(d, t, ncc, nc):
    bwd = jnp.where(t < ncc, ncc - 1 - t, ncc + (nc - 1 - t))
    return jnp.where(d == 0, t, bwd)


def _precedes(d):
    ii = lax.broadcasted_iota(jnp.int32, (CHUNK, CHUNK), 0)
    jj = lax.broadcasted_iota(jnp.int32, (CHUNK, CHUNK), 1)
    return (jj - ii) * (1 - 2 * d) <= 0


def _cumsum_scan_order(prec, x):
    tri = jnp.where(prec, 1.0, 0.0).astype(BF16)
    w = x.shape[1]
    parts = _dot(tri, jnp.concatenate(_split3(x), axis=1))
    return parts[:, :w] + parts[:, w:2 * w] + parts[:, 2 * w:]


def _ssd_kernel(x_ref, xp_ref, xn_ref, sm_ref, cw_ref, cb_ref, par_ref, y_ref, st_ref, cbuf_ref,
                *, ncc, nc):
    d = pl.program_id(1)
    t = pl.program_id(2)
    chunk = _scan_chunk(d, t, ncc, nc)

    @pl.when(t == 0)
    def _():
        st_ref[...] = jnp.zeros_like(st_ref)

    prev_ok = jnp.where(jnp.logical_and(chunk != 0, chunk != ncc), 1.0, 0.0)
    next_ok = jnp.where(jnp.logical_and(chunk != ncc - 1, chunk != nc - 1), 1.0, 0.0)
    prec = _precedes(d)
    rep = SSM_HEADS // SSM_GROUPS
    nb = x_ref.shape[0]
    for bb in range(nb):
        cbuf_ref[bb, 0:HALO, :] = xp_ref[bb].astype(F32) * prev_ok
        cbuf_ref[bb, HALO:HALO + CHUNK, :] = x_ref[bb].astype(F32)
        cbuf_ref[bb, HALO + CHUNK:, :] = xn_ref[bb].astype(F32) * next_ok
    taps = [[cbuf_ref[bb, HALO - SSM_CONV // 2 + k:HALO - SSM_CONV // 2 + k + CHUNK, :]
             for k in range(SSM_CONV)] for bb in range(nb)]
    gates_in = [sm_ref[bb] for bb in range(nb)]
    states = [[st_ref[bb, h] for h in range(SSM_HEADS)] for bb in range(nb)]
    stores = []
    for bb in range(nb):
        acc = jnp.broadcast_to(cb_ref[...], (CHUNK, SSM_CONV_CH))
        for k in range(SSM_CONV):
            acc = acc + taps[bb][k] * cw_ref[k:k + 1, :]
        xc = acc * _sigmoid(acc)
        xs = xc[:, :SSM_WIDTH]
        bm = xc[:, SSM_WIDTH:SSM_WIDTH + SSM_BC]
        cm = xc[:, SSM_WIDTH + SSM_BC:]

        dt = _softplus(gates_in[bb] + par_ref[0:1, :])
        la = dt * (-jnp.exp(par_ref[1:2, :]))
        acum = _cumsum_scan_order(prec, la)
        acum_t = acum.T
        total = jnp.sum(la, axis=0, keepdims=True)

        for g in range(SSM_GROUPS):
            cg = cm[:, g * SSM_STATE:(g + 1) * SSM_STATE].astype(BF16)
            bg = bm[:, g * SSM_STATE:(g + 1) * SSM_STATE].astype(BF16)
            cb = _dot_nt(cg, bg)
            for hh in range(rep):
                h = g * rep + hh
                a_col = acum[:, h:h + 1]
                a_row = acum_t[h:h + 1, :]
                tot = total[:, h:h + 1]
                x_h = xs[:, h * SSM_HEAD_DIM:(h + 1) * SSM_HEAD_DIM]
                xdt = x_h * dt[:, h:h + 1]
                lmat = jnp.exp(jnp.where(prec, a_col - a_row, -jnp.inf))
                h_in = states[bb][h]
                y = _dot((cb * lmat).astype(BF16), xdt.astype(BF16))
                y = y + _dot(cg, h_in.astype(BF16)) * jnp.exp(a_col)
                y = y + x_h * par_ref[2:3, h:h + 1]
                xw = (xdt * jnp.exp(tot - a_col)).astype(BF16)
                stores.append((bb, h, y, jnp.exp(tot) * h_in + _dot_tn(bg, xw)))

    for bb, h, y, h_new in stores:
        y_ref[bb, :, h * SSM_HEAD_DIM:(h + 1) * SSM_HEAD_DIM] = y
        st_ref[bb, h] = h_new


def _scan_batch(bsz):
    return math.gcd(bsz, SCAN_BATCH)


def _ssd(big, small, conv_w, conv_b, par, n_ctx):
    bsz, s_len, _ = big.shape
    ncc, nc = n_ctx // CHUNK, s_len // CHUNK
    per = CHUNK // HALO
    nb = _scan_batch(bsz)
    ch = functools.partial(_scan_chunk, ncc=ncc, nc=nc)
    cblk = COL_XBC // SSM_CONV_CH
    return pl.pallas_call(
        functools.partial(_ssd_kernel, ncc=ncc, nc=nc),
        out_shape=jax.ShapeDtypeStruct((2, bsz, s_len, SSM_WIDTH), F32),
        grid=(bsz // nb, 2, nc),
        in_specs=[
            pl.BlockSpec((nb, CHUNK, SSM_CONV_CH), lambda b, d, t: (b, ch(d, t), cblk)),
            pl.BlockSpec((nb, HALO, SSM_CONV_CH),
                         lambda b, d, t: (b, jnp.maximum(ch(d, t) * per - 1, 0), cblk)),
            pl.BlockSpec((nb, HALO, SSM_CONV_CH),
                         lambda b, d, t: (b, jnp.minimum((ch(d, t) + 1) * per, nc * per - 1), cblk)),
            pl.BlockSpec((nb, CHUNK, LANE), lambda b, d, t: (b, ch(d, t), d)),
            pl.BlockSpec(conv_w.shape, lambda b, d, t: (0, 0)),
            pl.BlockSpec(conv_b.shape, lambda b, d, t: (0, 0)),
            pl.BlockSpec((None, 8, LANE), lambda b, d, t: (d, 0, 0)),
        ],
        out_specs=pl.BlockSpec((None, nb, CHUNK, SSM_WIDTH), lambda b, d, t: (d, b, ch(d, t), 0)),
        scratch_shapes=[pltpu.VMEM((nb, SSM_HEADS, SSM_STATE, SSM_HEAD_DIM), F32),
                        pltpu.VMEM((nb, CHUNK + 2 * HALO, SSM_CONV_CH), F32)],
        compiler_params=_cparams(("arbitrary", "arbitrary", "arbitrary")),
        name="ssd",
    )(big, big, big, small, conv_w, conv_b, par)


def _mlstm_kernel(q_ref, k_ref, v_ref, kt_ref, sm_ref, par_ref, y_ref, st_ref, m_ref, *, ncc, nc):
    d = pl.program_id(1)
    t = pl.program_id(2)

    @pl.when(t == 0)
    def _():
        st_ref[...] = jnp.zeros_like(st_ref)
        m_ref[...] = jnp.zeros_like(m_ref)

    prec = _precedes(d)
    nb = q_ref.shape[0]
    dh = ML_HEAD_DIM
    loaded = [(sm_ref[bb], q_ref[bb], k_ref[bb], v_ref[bb], kt_ref[bb],
               [(m_ref[bb, h][0:1, :], st_ref[bb, h]) for h in range(ML_HEADS)])
              for bb in range(nb)]
    sel_k = lax.broadcasted_iota(jnp.int32, (3 * LANE, ML_HEADS * LANE), 0) % LANE
    sel_h = lax.broadcasted_iota(jnp.int32, (3 * LANE, ML_HEADS * LANE), 1) // LANE
    sel_all = jnp.where(sel_k == 2 * ML_HEADS + sel_h, 1.0, 0.0).astype(BF16)
    low = lax.broadcasted_iota(jnp.int32, (CHUNK, LANE), 1) < dh
    ones_tile = jnp.ones((CHUNK, LANE), BF16)
    twice = lambda x: jnp.concatenate([x, x], axis=1)
    stores = []
    for bb in range(nb):
        sm, q_all, k_all, v_all, kt_all, states = loaded[bb]
        pre = sm + par_ref[0:1, :]
        lf = -_softplus(-pre)
        bcum = _cumsum_scan_order(prec, lf)
        bcum_t = bcum.T
        li_t = pre.T
        gsum = jnp.broadcast_to(jnp.sum(lf, axis=0, keepdims=True), (HALO, LANE))
        both = jnp.concatenate([bcum, gsum], axis=0)
        rep = (_dot(jnp.concatenate(_split3(both), axis=1), sel_all))
        tiles = []
        for h in range(ML_HEADS):
            hs = slice(h * dh, (h + 1) * dh)
            ps = slice((h // 2) * LANE, (h // 2 + 1) * LANE)
            ci, cf = ML_HEADS + h, 2 * ML_HEADS + h
            b_rep = rep[:CHUNK, h * LANE:(h + 1) * LANE]
            gt = rep[CHUNK:CHUNK + 1, h * LANE:(h + 1) * LANE]
            b_row = bcum_t[cf:cf + 1, :]
            li_row = li_t[ci:ci + 1, :]
            qh = q_all[:, hs]
            kh = k_all[:, hs]
            v_aug = jnp.concatenate([v_all[:, ps], ones_tile], axis=1)
            m_in, st_in = states[h]

            log_d = jnp.where(prec, b_rep - b_row + li_row, -jnp.inf)
            log_inter = b_rep + m_in
            m_i = jnp.maximum(jnp.max(log_d, axis=-1, keepdims=True), log_inter)
            s = _dot_nt(qh, kh) * jnp.exp(log_d - m_i)
            inter = jnp.exp(log_inter - m_i)
            numden = _dot(s.astype(BF16), v_aug) + _dot(qh, st_in.astype(BF16)) * twice(inter)
            tiles.append(numden[:, :LANE] / jnp.maximum(jnp.abs(numden[:, LANE:]), jnp.exp(-m_i)))

            w_row = gt - b_row + li_row
            m_loc = jnp.max(w_row, axis=-1, keepdims=True)
            ket = (kt_all[hs, :].astype(F32) * jnp.exp(w_row - m_loc)).astype(BF16)
            m_new = jnp.maximum(gt + m_in, m_loc)
            a_old = jnp.exp(gt + m_in - m_new)
            a_new = jnp.exp(m_loc - m_new)
            st_new = twice(a_old) * st_in + twice(a_new) * _dot(ket, v_aug)
            stores.append((bb, h, st_new, m_new))
        for p in range(ML_HEADS // 2):
            y_ref[bb, :, p * LANE:(p + 1) * LANE] = jnp.where(low, tiles[2 * p], tiles[2 * p + 1])

    for bb, h, st_new, m_new in stores:
        st_ref[bb, h] = st_new
        m_ref[bb, h] = jnp.broadcast_to(m_new, m_ref.shape[2:])


def _mlstm(big, kt, small, par, n_ctx):
    bsz, s_len, _ = big.shape
    ncc, nc = n_ctx // CHUNK, s_len // CHUNK
    nb = _scan_batch(bsz)
    ch = functools.partial(_scan_chunk, ncc=ncc, nc=nc)

    def col_spec(col):
        return pl.BlockSpec((nb, CHUNK, ML_WIDTH), lambda b, d, t: (b, ch(d, t), col // ML_WIDTH))

    return pl.pallas_call(
        functools.partial(_mlstm_kernel, ncc=ncc, nc=nc),
        out_shape=jax.ShapeDtypeStruct((2, bsz, s_len, ML_WIDTH), F32),
        grid=(bsz // nb, 2, nc),
        in_specs=[col_spec(COL_MQ), col_spec(COL_MK), col_spec(COL_MV),
                  pl.BlockSpec((nb, ML_WIDTH, CHUNK), lambda b, d, t: (b, 0, ch(d, t))),
                  pl.BlockSpec((nb, CHUNK, LANE), lambda b, d, t: (b, ch(d, t), d)),
                  pl.BlockSpec((None, 8, LANE), lambda b, d, t: (d, 0, 0))],
        out_specs=pl.BlockSpec((None, nb, CHUNK, ML_WIDTH), lambda b, d, t: (d, b, ch(d, t), 0)),
        scratch_shapes=[pltpu.VMEM((nb, ML_HEADS, ML_HEAD_DIM, 2 * LANE), F32),
                        pltpu.VMEM((nb, ML_HEADS, 8, LANE), F32)],
        compiler_params=_cparams(("arbitrary", "arbitrary", "arbitrary")),
        name="mlstm",
    )(big, big, big, kt, small, par)


def _merge_kernel(ya_ref, yb_ref, z_ref, yc_ref, o_ref, h_ref, g1_ref, sh_ref, sc_ref, ng_ref,
                  sg_ref, mg_ref, bd_ref, wo_ref, rwh_ref, rwl_ref, rb_ref,
                  hout_ref, v_ref, gates_ref):
    z = z_ref[...].astype(F32)
    yb = (yb_ref[0] + yb_ref[1]) * (z * _sigmoid(z))
    yb = yb * lax.rsqrt(jnp.mean(yb * yb, axis=-1, keepdims=True) + EPS) * sg_ref[...]
    yc = _sigmoid(o_ref[...].astype(F32)) * (yc_ref[0] + yc_ref[1])
    yc = yc * lax.rsqrt(_segment_mean_sq(yc, bd_ref[...]) + EPS) * mg_ref[...]
    y = _dot(jnp.concatenate([ya_ref[...], yb.astype(BF16), yc.astype(BF16)], axis=1), wo_ref[...])
    hn = h_ref[...] + g1_ref[...] * y
    hout_ref[...] = hn
    v = hn * lax.rsqrt(jnp.mean(hn * hn, axis=-1, keepdims=True) + EPS) * ng_ref[...]
    v = v * (1.0 + sc_ref[...]) + sh_ref[...]
    v_hi, v_lo = _split2(v)
    v_ref[...] = v_hi
    rwh = rwh_ref[...]
    logits = (_dot_nt(rwh, v_hi) + _dot_nt(rwh, v_lo) + _dot_nt(rwl_ref[...], v_hi))[:N_EXPERTS] + rb_ref[...]

    row = lax.broadcasted_iota(jnp.int32, logits.shape, 0).astype(F32)
    work = logits
    top = None
    gates = jnp.zeros_like(logits)
    den = jnp.zeros((1, logits.shape[1]), F32)
    for _ in range(TOP_K):
        mx = jnp.max(work, axis=0, keepdims=True)
        idx = jnp.min(jnp.where(work == mx, row, float(N_EXPERTS)), axis=0, keepdims=True)
        hit = row == idx
        top = mx if top is None else top
        e = jnp.exp(mx - top)
        gates = gates + jnp.where(hit, e, 0.0)
        den = den + e
        work = jnp.where(hit, -jnp.inf, work)
    gates_ref[...] = jnp.concatenate(
        [gates / den, jnp.zeros((LANE - N_EXPERTS, logits.shape[1]), F32)], axis=0)


def _merge(ya, yb, yc, big, h, g1, sh2, sc2, ng, sg, mg, bd, wo, rwh, rwl, rb, n_ctx, ctx_out):
    bsz, s_len, d = h.shape
    tm = TOK_TILE
    first = 0 if ctx_out else n_ctx // tm
    n_out = s_len - first * tm
    mod_spec = pl.BlockSpec((None, None, 1, d), lambda b, i: (b, jnp.minimum(i + first, 1), 0, 0))
    full = lambda shape: pl.BlockSpec(shape, lambda b, i: (0,) * len(shape))
    pair = lambda w: pl.BlockSpec((2, None, tm, w), lambda b, i: (0, b, i + first, 0))
    nl = (s_len - n_ctx) // tm

    def flat_row(b, i):
        if ctx_out:
            return jnp.where(i == 0, bsz * nl + b, (i - 1) * bsz + b)
        return i * bsz + b

    flat_tok = lambda w: pl.BlockSpec((tm, w), lambda b, i: (flat_row(b, i), 0))
    return pl.pallas_call(
        _merge_kernel,
        out_shape=(jax.ShapeDtypeStruct((bsz, n_out, d), F32),
                   jax.ShapeDtypeStruct((bsz * n_out, d), BF16),
                   jax.ShapeDtypeStruct((LANE, bsz * n_out), F32)),
        grid=(bsz, n_out // tm),
        in_specs=[pl.BlockSpec((None, tm, NA_WIDTH), lambda b, i: (b, i, 0)),
                  pair(SSM_WIDTH),
                  pl.BlockSpec((None, tm, SSM_WIDTH), lambda b, i: (b, i + first, COL_Z // SSM_WIDTH)),
                  pair(ML_WIDTH),
                  pl.BlockSpec((None, tm, ML_WIDTH), lambda b, i: (b, i + first, COL_MO // ML_WIDTH)),
                  pl.BlockSpec((None, tm, d), lambda b, i: (b, i + first, 0)),
                  mod_spec, mod_spec, mod_spec,
                  full((1, d)), full((1, SSM_WIDTH)), full((1, ML_WIDTH)), full(bd.shape),
                  full(wo.shape), full(rwh.shape), full(rwl.shape), full(rb.shape)],
        out_specs=(pl.BlockSpec((None, tm, d), lambda b, i: (b, i, 0)), flat_tok(d),
                   pl.BlockSpec((LANE, tm), lambda b, i: (0, flat_row(b, i)))),
        compiler_params=_cparams(("arbitrary", "arbitrary")),
        name="merge",
    )(ya, yb, big, yc, big, h, g1, sh2, sc2, ng, sg, mg, bd, wo, rwh, rwl, rb)


def _moe_kernel(x_ref, gates_ref, w1_ref, b1_ref, w2_ref, b2_ref, f_ref, rank_ref, *, nblk):
    e = pl.program_id(1)
    tb = x_ref.shape[0] // nblk
    d_ff = w2_ref.shape[0]

    @pl.when(e == 0)
    def _():
        f_ref[...] = jnp.zeros_like(f_ref)
        ii = lax.broadcasted_iota(jnp.int32, (tb, tb), 0)
        jj = lax.broadcasted_iota(jnp.int32, (tb, tb), 1)
        tri = jnp.where(ii <= jj, 1.0, 0.0).astype(BF16)
        for k in range(nblk):
            picked = gates_ref[:, k * tb:(k + 1) * tb] > 0.0
            incl = _dot(jnp.where(picked, 1.0, 0.0).astype(BF16), tri)
            rank_ref[k] = jnp.where(picked, incl - 1.0, -1.0)

    def compact(k, s):
        slot = rank_ref[k, pl.ds(e, 1), :]
        gate = gates_ref[pl.ds(e, 1), k * tb:(k + 1) * tb]
        row = (lax.broadcasted_iota(jnp.int32, (MOE_CAP, tb), 0) + s * MOE_CAP).astype(F32)
        hit = slot == row
        onehot = jnp.where(hit, 1.0, 0.0).astype(BF16)
        xe = _dot(onehot, x_ref[k * tb:(k + 1) * tb, :]).astype(BF16)
        w = jnp.sum(jnp.where(hit, gate, 0.0), axis=-1, keepdims=True)
        return onehot, xe, w

    def expert(xe):
        hid = _dot(xe, w1_ref[...]) + b1_ref[...]
        x_glu = jnp.minimum(hid[:, :d_ff], SWIGLU_LIMIT)
        x_lin = jnp.clip(hid[:, d_ff:], -SWIGLU_LIMIT, SWIGLU_LIMIT)
        act = x_glu * _sigmoid(SWIGLU_ALPHA * x_glu) * (x_lin + 1.0)
        return _dot(act.astype(BF16), w2_ref[...]) + b2_ref[...]

    def scatter(k, onehot, yw):
        f_ref[k * tb:(k + 1) * tb, :] += _dot_tn(onehot, yw.astype(BF16))

    first = [compact(k, 0) for k in range(nblk)]
    y = expert(jnp.concatenate([xe for _, xe, _ in first], axis=0))
    for k, (onehot, _, w) in enumerate(first):
        scatter(k, onehot, y[k * MOE_CAP:(k + 1) * MOE_CAP] * w)

    for k in range(nblk):
        count = jnp.max(rank_ref[k, pl.ds(e, 1), :]).astype(jnp.int32) + 1

        def overflow(s, carry, k=k):
            onehot, xe, w = compact(k, s)
            scatter(k, onehot, expert(xe) * w)
            return carry

        lax.fori_loop(1, (count + MOE_CAP - 1) // MOE_CAP, overflow, 0)


def _moe(x, gates, w1, b1, w2, b2, layer, first_tok, n_tok):
    d = x.shape[1]
    nblk = MOE_GROUP if n_tok % (MOE_GROUP * MOE_BLOCK) == 0 else 1
    tb = math.gcd(n_tok, MOE_BLOCK)
    rows = nblk * tb
    d_ff = w2.shape[1]
    off = first_tok // rows
    assert first_tok % rows == 0 and n_tok % rows == 0
    expert = lambda i, e: (layer * N_EXPERTS + e, 0, 0)
    return pl.pallas_call(
        functools.partial(_moe_kernel, nblk=nblk),
        out_shape=jax.ShapeDtypeStruct((n_tok, d), F32),
        grid=(n_tok // rows, N_EXPERTS),
        in_specs=[pl.BlockSpec((rows, d), lambda i, e: (i + off, 0)),
                  pl.BlockSpec((LANE, rows), lambda i, e: (0, i + off)),
                  pl.BlockSpec((None, d, 2 * d_ff), expert),
                  pl.BlockSpec((None, 1, 2 * d_ff), expert),
                  pl.BlockSpec((None, d_ff, d), expert),
                  pl.BlockSpec((None, 1, d), expert)],
        out_specs=pl.BlockSpec((rows, d), lambda i, e: (i, 0)),
        scratch_shapes=[pltpu.VMEM((nblk, LANE, tb), F32)],
        compiler_params=pltpu.CompilerParams(dimension_semantics=("arbitrary", "arbitrary"),
                                             vmem_limit_bytes=MOE_VMEM_LIMIT),
        name="moe",
    )(x, gates, w1, b1, w2, b2)


def _resid_kernel(h_ref, g_ref, fl_ref, *rest, first):
    if first == 0:
        fc_ref, o_ref = rest
        f = jnp.where(pl.program_id(1) == 0, fc_ref[...], fl_ref[...])
    else:
        (o_ref,) = rest
        f = fl_ref[...]
    o_ref[...] = h_ref[...] + g_ref[...] * f


def _resid(h, g2, f_lat, f_ctx):
    bsz, n_tok, d = h.shape
    tm = TOK_TILE
    first = 0 if f_ctx is not None else 1
    in_specs = [pl.BlockSpec((None, tm, d), lambda b, i: (b, i, 0)),
                pl.BlockSpec((None, None, 1, d), lambda b, i: (b, jnp.minimum(i + first, 1), 0, 0)),
                pl.BlockSpec((tm, d), lambda b, i: (jnp.maximum(i + first - 1, 0) * bsz + b, 0))]
    args = [h, g2, f_lat]
    if f_ctx is not None:
        in_specs.append(pl.BlockSpec((tm, d), lambda b, i: (b, 0)))
        args.append(f_ctx)
    return pl.pallas_call(
        functools.partial(_resid_kernel, first=first),
        out_shape=jax.ShapeDtypeStruct((bsz, n_tok, d), F32),
        grid=(bsz, n_tok // tm),
        in_specs=in_specs,
        out_specs=pl.BlockSpec((None, tm, d), lambda b, i: (b, i, 0)),
        compiler_params=_cparams(("arbitrary", "arbitrary")),
        name="resid",
    )(*args)


def _in_weight_perm():
    o = IN_OFFSETS
    big = np.concatenate([np.arange(o[0], o[5]), np.arange(o[6], o[10])])
    small = np.full((2, LANE), -1, np.int64)
    for d in range(2):
        for h in range(SSM_HEADS):
            small[d, h] = o[5] + d * SSM_HEADS + h
        for h in range(ML_HEADS):
            small[d, ML_HEADS + h] = o[10] + d * ML_HEADS + h
            small[d, 2 * ML_HEADS + h] = o[11] + d * ML_HEADS + h
    return np.concatenate([big, small.reshape(-1)])


def _rope_tables(n_ctx, n_lat):
    pos = np.arange(n_lat)
    row = (pos // GRID_W).astype(np.float32)
    col = (pos % GRID_W).astype(np.float32)
    n_freq = ML_HEAD_DIM // 4
    inv_freq = jnp.asarray(ROPE_THETA, F32) ** (-jnp.arange(n_freq, dtype=F32) / n_freq)
    ang = jnp.concatenate([jnp.asarray(row)[:, None] * inv_freq, jnp.asarray(col)[:, None] * inv_freq], axis=-1)
    cos, sin = jnp.cos(ang), jnp.sin(ang)
    cosf = jnp.tile(jnp.concatenate([cos, cos], axis=-1), (1, ML_HEADS))
    sinf = jnp.tile(jnp.concatenate([-sin, sin], axis=-1), (1, ML_HEADS))
    cosf = jnp.concatenate([jnp.ones((n_ctx, ML_WIDTH), F32), cosf], axis=0)
    sinf = jnp.concatenate([jnp.zeros((n_ctx, ML_WIDTH), F32), sinf], axis=0)
    return cosf, sinf


def _lane_rows(rows):
    out = jnp.zeros((8, LANE), F32)
    for r, (off, vec) in enumerate(rows):
        out = out.at[r, off:off + vec.shape[0]].set(vec.astype(F32))
    return out


def kernel(x, c, ctx, c_ctx, ada_w, ada_b, norm_g, w_in, na_qk_g, na_rpb, ssm_conv_w, ssm_conv_b,
           ssm_dt_bias, ssm_a_log, ssm_d, ssm_norm_g, ml_gate_b, ml_norm_g, w_out,
           router_w, router_b, exp_w1, exp_b1, exp_w2, exp_b2):
    bsz, n_lat, d = x.shape
    n_ctx = ctx.shape[1]
    rows = n_lat // GRID_W
    depth = w_in.shape[0]
    assert bsz < 8 and n_ctx == TOK_TILE and n_lat % TOK_TILE == 0 and rows >= NA_KROWS

    h = jnp.concatenate([ctx, x], axis=1)
    cond = jnp.zeros((8, d), F32).at[:bsz].set(c).at[bsz].set(c_ctx)
    cosf, sinf = _rope_tables(n_ctx, n_lat)
    perm = _in_weight_perm()
    seg = np.arange(2 * LANE) // NA_HEAD_DIM
    bd = jnp.asarray(seg[:, None] == seg[None, :], BF16)

    n_exp = exp_w1.shape[0] * exp_w1.shape[1]
    w1_all = exp_w1.astype(BF16).reshape((n_exp,) + exp_w1.shape[2:])
    w2_all = exp_w2.astype(BF16).reshape((n_exp,) + exp_w2.shape[2:])
    b1_all = exp_b1.reshape(n_exp, 1, -1)
    b2_all = exp_b2.reshape(n_exp, 1, -1)

    out = None
    for layer in range(depth):
        ctx_out = layer < depth - 1
        mod = _ada(cond, ada_w[layer], ada_b[layer])
        mods = []
        for m in jnp.split(mod, 6, axis=-1):
            lat = m[:bsz]
            cx = jnp.broadcast_to(m[bsz][None], (bsz, d))
            mods.append(jnp.stack([cx, lat], axis=1).reshape(bsz, 2, 1, d))
        sh1, sc1, g1, sh2, sc2, g2 = mods

        w = jnp.where(perm[None, :] >= 0, w_in[layer][:, np.maximum(perm, 0)], 0.0).astype(BF16)
        qkg = jnp.concatenate([jnp.tile(na_qk_g[layer, 0] * NA_HEAD_DIM ** -0.5, NA_HEADS),
                               jnp.tile(na_qk_g[layer, 1], NA_HEADS)]).reshape(1, 2 * NA_WIDTH)
        big, small, kt = _inproj(h, sh1, sc1, norm_g[layer, 0].reshape(1, d), w, qkg, bd, cosf, sinf)

        bias = _attention_bias(na_rpb[layer], rows)
        ya = _attention(big, bias, n_ctx, rows, ctx_out)

        conv_w = jnp.zeros((8, SSM_CONV_CH), F32).at[:SSM_CONV].set(ssm_conv_w[layer])
        ssm_par = jnp.stack([_lane_rows([(0, ssm_dt_bias[layer, dr]), (0, ssm_a_log[layer, dr]),
                                         (0, ssm_d[layer, dr])]) for dr in range(2)])
        yb = _ssd(big, small, conv_w, ssm_conv_b[layer].reshape(1, -1), ssm_par, n_ctx)

        gate_row = [jnp.concatenate([jnp.zeros((ML_HEADS,), F32), ml_gate_b[layer, dr, 0],
                                     ml_gate_b[layer, dr, 1]]) for dr in range(2)]
        ml_par = jnp.stack([_lane_rows([(0, gate_row[dr])]) for dr in range(2)])
        yc = _mlstm(big, kt, small, ml_par, n_ctx)

        rw = jnp.zeros((LANE, d), F32).at[:N_EXPERTS].set(router_w[layer].T)
        rwh = rw.astype(BF16)
        rwl = (rw - rwh.astype(F32)).astype(BF16)
        rb = jnp.broadcast_to(router_b[layer].astype(F32)[:, None], (N_EXPERTS, TOK_TILE))
        hn, v, gates = _merge(ya, yb, yc, big, h, g1, sh2, sc2, norm_g[layer, 1].reshape(1, d),
                              ssm_norm_g[layer].reshape(1, -1), ml_norm_g[layer].reshape(1, -1), bd,
                              w_out[layer].astype(BF16), rwh, rwl, rb, n_ctx, ctx_out)

        n_lat_tok = bsz * n_lat
        f_lat = _moe(v, gates, w1_all, b1_all, w2_all, b2_all, layer, 0, n_lat_tok)
        f_ctx = (_moe(v, gates, w1_all, b1_all, w2_all, b2_all, layer, n_lat_tok, bsz * n_ctx)
                 if ctx_out else None)
        h = _resid(hn, g2, f_lat, f_ctx)
        out = h
    return out
```

```python
import functools
import math

import numpy as np
import jax
import jax.numpy as jnp
from jax import lax
from jax.experimental import pallas as pl
from jax.experimental.pallas import tpu as pltpu

F32 = jnp.float32
BF16 = jnp.bfloat16

GRID_W = 64
EPS = 1e-6
NA_HEADS = 8
NA_HEAD_DIM = 64
NA_WIN_H = 8
NA_WIN_W = 16
SSM_HEADS = 4
SSM_HEAD_DIM = 64
SSM_GROUPS = 2
SSM_STATE = 128
SSM_CONV = 5
ML_HEADS = 4
ML_HEAD_DIM = 64
ROPE_THETA = 10000.0
N_EXPERTS = 32
TOP_K = 4
SWIGLU_ALPHA = 1.702
SWIGLU_LIMIT = 7.0

NA_WIDTH = NA_HEADS * NA_HEAD_DIM
SSM_WIDTH = SSM_HEADS * SSM_HEAD_DIM
SSM_BC = SSM_GROUPS * SSM_STATE
SSM_CONV_CH = SSM_WIDTH + 2 * SSM_BC
ML_WIDTH = ML_HEADS * ML_HEAD_DIM
IN_SPLITS = (NA_WIDTH, NA_WIDTH, NA_WIDTH, SSM_CONV_CH, SSM_WIDTH, 2 * SSM_HEADS,
             ML_WIDTH, ML_WIDTH, ML_WIDTH, ML_WIDTH, 2 * ML_HEADS, 2 * ML_HEADS)
IN_OFFSETS = tuple(int(o) for o in np.cumsum((0,) + IN_SPLITS))

LANE = 128
CHUNK = 128
HALO = 16
TOK_TILE = 256
NA_ROWS = 2
NA_KROWS = NA_ROWS + NA_WIN_H
BIG_W = 3 * NA_WIDTH + SSM_CONV_CH + SSM_WIDTH + 4 * ML_WIDTH
SMALL_W = 2 * LANE
PLAIN_COLS = 768
MOE_BLOCK = 1024
MOE_GROUP = 2
MOE_CAP = 192
SCAN_BATCH = 4
NEG = -1e30
VMEM_LIMIT = 48 * 1024 * 1024
MOE_VMEM_LIMIT = 56 * 1024 * 1024

COL_Q, COL_K, COL_V = 0, NA_WIDTH, 2 * NA_WIDTH
COL_XBC = 3 * NA_WIDTH
COL_Z = COL_XBC + SSM_CONV_CH
COL_MQ = COL_Z + SSM_WIDTH
COL_MK = COL_MQ + ML_WIDTH
COL_MV = COL_MK + ML_WIDTH
COL_MO = COL_MV + ML_WIDTH


def _cparams(sem):
    return pltpu.CompilerParams(dimension_semantics=sem, vmem_limit_bytes=VMEM_LIMIT)


def _sigmoid(x):
    return 1.0 / (1.0 + jnp.exp(-x))


def _softplus(x):
    return jnp.maximum(x, 0.0) + jnp.log1p(jnp.exp(-jnp.abs(x)))


def _dot(a, b):
    return jnp.dot(a, b, preferred_element_type=F32)


def _dot_nt(a, b):
    return lax.dot_general(a, b, (((1,), (1,)), ((), ())), preferred_element_type=F32)


def _dot_tn(a, b):
    return lax.dot_general(a, b, (((0,), (0,)), ((), ())), preferred_element_type=F32)


def _split3(x):
    x1 = x.astype(BF16)
    r1 = x - x1.astype(F32)
    x2 = r1.astype(BF16)
    x3 = (r1 - x2.astype(F32)).astype(BF16)
    return x1, x2, x3


def _split2(x):
    x1 = x.astype(BF16)
    return x1, (x - x1.astype(F32)).astype(BF16)


def _segment_mean_sq(x, bd):
    hi, lo = _split2(x * x)
    return (_dot(hi, bd) + _dot(lo, bd)) * (1.0 / NA_HEAD_DIM)


def _ada_kernel(c_ref, w_ref, b_ref, o_ref):
    c = c_ref[...]
    s = c * _sigmoid(c)
    o_ref[...] = jnp.dot(s, w_ref[...], preferred_element_type=F32,
                         precision=lax.Precision.HIGHEST) + b_ref[...]


def _ada(cond, w, b):
    n, d = cond.shape
    wcols = w.shape[1]
    bn = wcols // 4
    return pl.pallas_call(
        _ada_kernel,
        out_shape=jax.ShapeDtypeStruct((n, wcols), F32),
        grid=(4,),
        in_specs=[pl.BlockSpec((n, d), lambda j: (0, 0)),
                  pl.BlockSpec((d, bn), lambda j: (0, j)),
                  pl.BlockSpec((1, bn), lambda j: (0, j))],
        out_specs=pl.BlockSpec((n, bn), lambda j: (0, j)),
        compiler_params=_cparams(("arbitrary",)),
        name="ada",
    )(cond, w, b.reshape(1, wcols))


def _inproj_kernel(h_ref, sh_ref, sc_ref, g_ref, w_ref, qkg_ref, bd_ref, cos_ref, sin_ref,
                   big_ref, small_ref, kt_ref):
    x = h_ref[...]
    ms = jnp.mean(x * x, axis=-1, keepdims=True)
    u = x * lax.rsqrt(ms + EPS) * g_ref[...]
    u = u * (1.0 + sc_ref[...]) + sh_ref[...]
    ub = u.astype(BF16)
    bd = bd_ref[...]

    for c0 in range(0, 2 * NA_WIDTH, NA_WIDTH):
        wide = _dot(ub, w_ref[:, c0:c0 + NA_WIDTH])
        for c1 in range(0, NA_WIDTH, 2 * LANE):
            acc = wide[:, c1:c1 + 2 * LANE]
            msq = _segment_mean_sq(acc, bd)
            y = acc * lax.rsqrt(msq + EPS) * qkg_ref[:, c0 + c1:c0 + c1 + 2 * LANE]
            big_ref[:, c0 + c1:c0 + c1 + 2 * LANE] = y.astype(BF16)
    for c0 in range(COL_V, COL_MQ, PLAIN_COLS):
        big_ref[:, c0:c0 + PLAIN_COLS] = _dot(ub, w_ref[:, c0:c0 + PLAIN_COLS]).astype(BF16)
    lane = lax.broadcasted_iota(jnp.int32, (x.shape[0], LANE), 1)
    first_half = (lane % ML_HEAD_DIM) < (ML_HEAD_DIM // 2)
    for base, scale in ((COL_MQ, ML_HEAD_DIM ** -0.5), (COL_MK, 1.0)):
        for j in range(ML_WIDTH // LANE):
            c0 = base + j * LANE
            acc = _dot(ub, w_ref[:, c0:c0 + LANE])
            swapped = jnp.where(first_half,
                                pltpu.roll(acc, LANE - ML_HEAD_DIM // 2, 1),
                                pltpu.roll(acc, ML_HEAD_DIM // 2, 1))
            y = acc * cos_ref[:, j * LANE:(j + 1) * LANE] + swapped * sin_ref[:, j * LANE:(j + 1) * LANE]
            big_ref[:, c0:c0 + LANE] = (y * scale).astype(BF16)
            if base == COL_MK:
                kt_ref[j * LANE:(j + 1) * LANE, :] = y.T.astype(BF16)
    big_ref[:, COL_MV:BIG_W] = _dot(ub, w_ref[:, COL_MV:BIG_W]).astype(BF16)
    small_ref[...] = _dot(ub, w_ref[:, BIG_W:BIG_W + SMALL_W])


def _inproj(h, sh, sc, g, w, qkg, bd, cosf, sinf):
    bsz, s_len, d = h.shape
    tm = TOK_TILE
    mod_spec = pl.BlockSpec((None, None, 1, d), lambda b, i: (b, jnp.minimum(i, 1), 0, 0))
    full = lambda shape: pl.BlockSpec(shape, lambda b, i: (0,) * len(shape))
    return pl.pallas_call(
        _inproj_kernel,
        out_shape=(jax.ShapeDtypeStruct((bsz, s_len, BIG_W), BF16),
                   jax.ShapeDtypeStruct((bsz, s_len, SMALL_W), F32),
                   jax.ShapeDtypeStruct((bsz, ML_WIDTH, s_len), BF16)),
        grid=(bsz, s_len // tm),
        in_specs=[pl.BlockSpec((None, tm, d), lambda b, i: (b, i, 0)),
                  mod_spec, mod_spec, full((1, d)), full(w.shape), full(qkg.shape), full(bd.shape),
                  pl.BlockSpec((tm, ML_WIDTH), lambda b, i: (i, 0)),
                  pl.BlockSpec((tm, ML_WIDTH), lambda b, i: (i, 0))],
        out_specs=(pl.BlockSpec((None, tm, BIG_W), lambda b, i: (b, i, 0)),
                   pl.BlockSpec((None, tm, SMALL_W), lambda b, i: (b, i, 0)),
                   pl.BlockSpec((None, ML_WIDTH, tm), lambda b, i: (b, 0, i))),
        compiler_params=_cparams(("arbitrary", "arbitrary")),
        name="inproj",
    )(h, sh, sc, g, w, qkg, bd, cosf, sinf)


def _na_kernel(q_ref, k0, k1, k2, k3, k4, v0, v1, v2, v3, v4, kc_ref, vc_ref, bias_ref, o_ref):
    nq = q_ref.shape[0]
    lane = lax.broadcasted_iota(jnp.int32, (nq, LANE), 1)
    low = lane < NA_HEAD_DIM
    zero = jnp.zeros((), BF16)
    outs = []
    for p in range(NA_WIDTH // LANE):
        cs = slice(p * LANE, (p + 1) * LANE)
        qp = q_ref[:, cs]
        q01 = jnp.concatenate([jnp.where(low, qp, zero), jnp.where(low, zero, qp)], axis=0)
        k_all = jnp.concatenate([r[:, cs] for r in (k0, k1, k2, k3, k4, kc_ref)], axis=0)
        v_all = jnp.concatenate([r[:, cs] for r in (v0, v1, v2, v3, v4, vc_ref)], axis=0)
        v_all = jnp.concatenate([v_all, jnp.ones(v_all.shape, BF16)], axis=1)
        s = _dot_nt(q01, k_all)
        n_win = bias_ref.shape[-1]
        s_w = s[:, :n_win] + bias_ref[p]
        s_c = s[:, n_win:]
        m = jnp.maximum(jnp.max(s_w, axis=-1, keepdims=True), jnp.max(s_c, axis=-1, keepdims=True))
        probs = jnp.concatenate([jnp.exp(s_w - m), jnp.exp(s_c - m)], axis=1).astype(BF16)
        o01 = _dot(probs, v_all)
        o01 = o01[:, :LANE] / o01[:, LANE:]
        outs.append(jnp.where(low, o01[:nq], o01[nq:]).astype(o_ref.dtype))
    o_ref[...] = jnp.concatenate(outs, axis=1)


def _attention(big, bias, n_ctx, rows, ctx_out):
    bsz, s_len, _ = big.shape
    nq = NA_ROWS * GRID_W
    ncb = n_ctx // nq
    nrp = rows // NA_ROWS
    first = 0 if ctx_out else ncb
    n_win = NA_KROWS // NA_ROWS

    def lat(i):
        return jnp.maximum(i + first - ncb, 0)

    def kv_spec(j, col):
        return pl.BlockSpec(
            (None, nq, NA_WIDTH),
            lambda b, i: (b, ncb + jnp.clip(lat(i) - 2, 0, nrp - n_win) + j, col))

    def bias_type(b, i):
        step = i + first
        r = lat(i)
        t = jnp.where(r < 2, r, jnp.where(r >= nrp - 2, r - (nrp - 2) + 3, 2))
        return (jnp.where(step < ncb, 5, t), 0, 0, 0)

    return pl.pallas_call(
        _na_kernel,
        out_shape=jax.ShapeDtypeStruct((bsz, s_len - first * nq, NA_WIDTH), BF16),
        grid=(bsz, ncb + nrp - first),
        in_specs=[pl.BlockSpec((None, nq, NA_WIDTH), lambda b, i: (b, i + first, 0))]
                 + [kv_spec(j, 1) for j in range(n_win)] + [kv_spec(j, 2) for j in range(n_win)]
                 + [pl.BlockSpec((None, n_ctx, NA_WIDTH), lambda b, i: (b, 0, 1)),
                    pl.BlockSpec((None, n_ctx, NA_WIDTH), lambda b, i: (b, 0, 2)),
                    pl.BlockSpec((None, NA_WIDTH // LANE, 2 * nq, NA_KROWS * GRID_W), bias_type)],
        out_specs=pl.BlockSpec((None, nq, NA_WIDTH), lambda b, i: (b, i, 0)),
        compiler_params=_cparams(("arbitrary", "arbitrary")),
        name="attention",
    )(big, *([big] * (2 * n_win + 2)), bias)


def _attention_bias(rpb, rows):
    nrp = rows // NA_ROWS
    n_win = NA_KROWS // NA_ROWS
    reps = np.array([0, 1, 2, nrp - 2, nrp - 1])
    qr = (NA_ROWS * reps)[:, None] + np.arange(NA_ROWS)[None, :]
    ws = NA_ROWS * np.clip(reps - 2, 0, nrp - n_win)
    kr = ws[:, None] + np.arange(NA_KROWS)[None, :]
    r0 = np.clip(qr - NA_WIN_H // 2, 0, rows - NA_WIN_H)
    ok_r = (kr[:, None, :] >= r0[:, :, None]) & (kr[:, None, :] < r0[:, :, None] + NA_WIN_H)
    dr = np.clip(kr[:, None, :] - qr[:, :, None] + (NA_WIN_H - 1), 0, 2 * NA_WIN_H - 2)
    qc = np.arange(GRID_W)
    c0 = np.clip(qc - NA_WIN_W // 2, 0, GRID_W - NA_WIN_W)
    kc = np.arange(GRID_W)
    ok_c = (kc[None, :] >= c0[:, None]) & (kc[None, :] < c0[:, None] + NA_WIN_W)
    dc = np.clip(kc[None, :] - qc[:, None] + (NA_WIN_W - 1), 0, 2 * NA_WIN_W - 2)
    sel_r = jnp.asarray(dr[..., None] == np.arange(2 * NA_WIN_H - 1), F32)
    sel_c = jnp.asarray(dc[..., None] == np.arange(2 * NA_WIN_W - 1), F32)
    hp = lax.Precision.HIGHEST
    g = jnp.einsum('hrc,taer->thaec', rpb.astype(F32), sel_r, precision=hp)
    g = jnp.einsum('thaec,qkc->thaqek', g, sel_c, precision=hp)
    ok = ok_r[:, None, :, None, :, None] & ok_c[None, None, None, :, None, :]
    g = jnp.where(ok, g, NEG)
    g = g.reshape(5, NA_HEADS * NA_ROWS * GRID_W, NA_KROWS * GRID_W)
    g = g.reshape(5, NA_WIDTH // LANE, 2 * NA_ROWS * GRID_W, NA_KROWS * GRID_W)
    masked = jnp.full((1,) + g.shape[1:], NEG, F32)
    return jnp.concatenate([g, masked], axis=0)


def _scan_chunk(d, t, ncc, nc):
    bwd = jnp.where(t < ncc, ncc - 1 - t, ncc + (nc - 1 - t))
    return jnp.where(d == 0, t, bwd)


def _precedes(d):
    ii = lax.broadcasted_iota(jnp.int32, (CHUNK, CHUNK), 0)
    jj = lax.broadcasted_iota(jnp.int32, (CHUNK, CHUNK), 1)
    return (jj - ii) * (1 - 2 * d) <= 0


def _cumsum_scan_order(prec, x):
    tri = jnp.where(prec, 1.0, 0.0).astype(BF16)
    w = x.shape[1]
    parts = _dot(tri, jnp.concatenate(_split3(x), axis=1))
    return parts[:, :w] + parts[:, w:2 * w] + parts[:, 2 * w:]


def _ssd_kernel(x_ref, xp_ref, xn_ref, sm_ref, cw_ref, cb_ref, par_ref, y_ref, st_ref, cbuf_ref,
                *, ncc, nc):
    d = pl.program_id(1)
    t = pl.program_id(2)
    chunk = _scan_chunk(d, t, ncc, nc)

    @pl.when(t == 0)
    def _():
        st_ref[...] = jnp.zeros_like(st_ref)

    prev_ok = jnp.where(jnp.logical_and(chunk != 0, chunk != ncc), 1.0, 0.0)
    next_ok = jnp.where(jnp.logical_and(chunk != ncc - 1, chunk != nc - 1), 1.0, 0.0)
    prec = _precedes(d)
    rep = SSM_HEADS // SSM_GROUPS
    nb = x_ref.shape[0]
    for bb in range(nb):
        cbuf_ref[bb, 0:HALO, :] = xp_ref[bb].astype(F32) * prev_ok
        cbuf_ref[bb, HALO:HALO + CHUNK, :] = x_ref[bb].astype(F32)
        cbuf_ref[bb, HALO + CHUNK:, :] = xn_ref[bb].astype(F32) * next_ok
    taps = [[cbuf_ref[bb, HALO - SSM_CONV // 2 + k:HALO - SSM_CONV // 2 + k + CHUNK, :]
             for k in range(SSM_CONV)] for bb in range(nb)]
    gates_in = [sm_ref[bb] for bb in range(nb)]
    states = [[st_ref[bb, h] for h in range(SSM_HEADS)] for bb in range(nb)]
    stores = []
    for bb in range(nb):
        acc = jnp.broadcast_to(cb_ref[...], (CHUNK, SSM_CONV_CH))
        for k in range(SSM_CONV):
            acc = acc + taps[bb][k] * cw_ref[k:k + 1, :]
        xc = acc * _sigmoid(acc)
        xs = xc[:, :SSM_WIDTH]
        bm = xc[:, SSM_WIDTH:SSM_WIDTH + SSM_BC]
        cm = xc[:, SSM_WIDTH + SSM_BC:]

        dt = _softplus(gates_in[bb] + par_ref[0:1, :])
        la = dt * (-jnp.exp(par_ref[1:2, :]))
        acum = _cumsum_scan_order(prec, la)
        acum_t = acum.T
        total = jnp.sum(la, axis=0, keepdims=True)

        for g in range(SSM_GROUPS):
            cg = cm[:, g * SSM_STATE:(g + 1) * SSM_STATE].astype(BF16)
            bg = bm[:, g * SSM_STATE:(g + 1) * SSM_STATE].astype(BF16)
            cb = _dot_nt(cg, bg)
            for hh in range(rep):
                h = g * rep + hh
                a_col = acum[:, h:h + 1]
                a_row = acum_t[h:h + 1, :]
                tot = total[:, h:h + 1]
                x_h = xs[:, h * SSM_HEAD_DIM:(h + 1) * SSM_HEAD_DIM]
                xdt = x_h * dt[:, h:h + 1]
                lmat = jnp.exp(jnp.where(prec, a_col - a_row, -jnp.inf))
                h_in = states[bb][h]
                y = _dot((cb * lmat).astype(BF16), xdt.astype(BF16))
                y = y + _dot(cg, h_in.astype(BF16)) * jnp.exp(a_col)
                y = y + x_h * par_ref[2:3, h:h + 1]
                xw = (xdt * jnp.exp(tot - a_col)).astype(BF16)
                stores.append((bb, h, y, jnp.exp(tot) * h_in + _dot_tn(bg, xw)))

    for bb, h, y, h_new in stores:
        y_ref[bb, :, h * SSM_HEAD_DIM:(h + 1) * SSM_HEAD_DIM] = y
        st_ref[bb, h] = h_new


def _scan_batch(bsz):
    return math.gcd(bsz, SCAN_BATCH)


def _ssd(big, small, conv_w, conv_b, par, n_ctx):
    bsz, s_len, _ = big.shape
    ncc, nc = n_ctx // CHUNK, s_len // CHUNK
    per = CHUNK // HALO
    nb = _scan_batch(bsz)
    ch = functools.partial(_scan_chunk, ncc=ncc, nc=nc)
    cblk = COL_XBC // SSM_CONV_CH
    return pl.pallas_call(
        functools.partial(_ssd_kernel, ncc=ncc, nc=nc),
        out_shape=jax.ShapeDtypeStruct((2, bsz, s_len, SSM_WIDTH), F32),
        grid=(bsz // nb, 2, nc),
        in_specs=[
            pl.BlockSpec((nb, CHUNK, SSM_CONV_CH), lambda b, d, t: (b, ch(d, t), cblk)),
            pl.BlockSpec((nb, HALO, SSM_CONV_CH),
                         lambda b, d, t: (b, jnp.maximum(ch(d, t) * per - 1, 0), cblk)),
            pl.BlockSpec((nb, HALO, SSM_CONV_CH),
                         lambda b, d, t: (b, jnp.minimum((ch(d, t) + 1) * per, nc * per - 1), cblk)),
            pl.BlockSpec((nb, CHUNK, LANE), lambda b, d, t: (b, ch(d, t), d)),
            pl.BlockSpec(conv_w.shape, lambda b, d, t: (0, 0)),
            pl.BlockSpec(conv_b.shape, lambda b, d, t: (0, 0)),
            pl.BlockSpec((None, 8, LANE), lambda b, d, t: (d, 0, 0)),
        ],
        out_specs=pl.BlockSpec((None, nb, CHUNK, SSM_WIDTH), lambda b, d, t: (d, b, ch(d, t), 0)),
        scratch_shapes=[pltpu.VMEM((nb, SSM_HEADS, SSM_STATE, SSM_HEAD_DIM), F32),
                        pltpu.VMEM((nb, CHUNK + 2 * HALO, SSM_CONV_CH), F32)],
        compiler_params=_cparams(("arbitrary", "arbitrary", "arbitrary")),
        name="ssd",
    )(big, big, big, small, conv_w, conv_b, par)


def _mlstm_kernel(q_ref, k_ref, v_ref, kt_ref, sm_ref, par_ref, y_ref, st_ref, m_ref, *, ncc, nc):
    d = pl.program_id(1)
    t = pl.program_id(2)

    @pl.when(t == 0)
    def _():
        st_ref[...] = jnp.zeros_like(st_ref)
        m_ref[...] = jnp.zeros_like(m_ref)

    prec = _precedes(d)
    nb = q_ref.shape[0]
    dh = ML_HEAD_DIM
    loaded = [(sm_ref[bb], q_ref[bb], k_ref[bb], v_ref[bb], kt_ref[bb],
               [(m_ref[bb, h][0:1, :], st_ref[bb, h]) for h in range(ML_HEADS)])
              for bb in range(nb)]
    sel_k = lax.broadcasted_iota(jnp.int32, (3 * LANE, ML_HEADS * LANE), 0) % LANE
    sel_h = lax.broadcasted_iota(jnp.int32, (3 * LANE, ML_HEADS * LANE), 1) // LANE
    sel_all = jnp.where(sel_k == 2 * ML_HEADS + sel_h, 1.0, 0.0).astype(BF16)
    low = lax.broadcasted_iota(jnp.int32, (CHUNK, LANE), 1) < dh
    ones_tile = jnp.ones((CHUNK, LANE), BF16)
    twice = lambda x: jnp.concatenate([x, x], axis=1)
    stores = []
    for bb in range(nb):
        sm, q_all, k_all, v_all, kt_all, states = loaded[bb]
        pre = sm + par_ref[0:1, :]
        lf = -_softplus(-pre)
        bcum = _cumsum_scan_order(prec, lf)
        bcum_t = bcum.T
        li_t = pre.T
        gsum = jnp.broadcast_to(jnp.sum(lf, axis=0, keepdims=True), (HALO, LANE))
        both = jnp.concatenate([bcum, gsum], axis=0)
        rep = (_dot(jnp.concatenate(_split3(both), axis=1), sel_all))
        tiles = []
        for h in range(ML_HEADS):
            hs = slice(h * dh, (h + 1) * dh)
            ps = slice((h // 2) * LANE, (h // 2 + 1) * LANE)
            ci, cf = ML_HEADS + h, 2 * ML_HEADS + h
            b_rep = rep[:CHUNK, h * LANE:(h + 1) * LANE]
            gt = rep[CHUNK:CHUNK + 1, h * LANE:(h + 1) * LANE]
            b_row = bcum_t[cf:cf + 1, :]
            li_row = li_t[ci:ci + 1, :]
            qh = q_all[:, hs]
            kh = k_all[:, hs]
            v_aug = jnp.concatenate([v_all[:, ps], ones_tile], axis=1)
            m_in, st_in = states[h]

            log_d = jnp.where(prec, b_rep - b_row + li_row, -jnp.inf)
            log_inter = b_rep + m_in
            m_i = jnp.maximum(jnp.max(log_d, axis=-1, keepdims=True), log_inter)
            s = _dot_nt(qh, kh) * jnp.exp(log_d - m_i)
            inter = jnp.exp(log_inter - m_i)
            numden = _dot(s.astype(BF16), v_aug) + _dot(qh, st_in.astype(BF16)) * twice(inter)
            tiles.append(numden[:, :LANE] / jnp.maximum(jnp.abs(numden[:, LANE:]), jnp.exp(-m_i)))

            w_row = gt - b_row + li_row
            m_loc = jnp.max(w_row, axis=-1, keepdims=True)
            ket = (kt_all[hs, :].astype(F32) * jnp.exp(w_row - m_loc)).astype(BF16)
            m_new = jnp.maximum(gt + m_in, m_loc)
            a_old = jnp.exp(gt + m_in - m_new)
            a_new = jnp.exp(m_loc - m_new)
            st_new = twice(a_old) * st_in + twice(a_new) * _dot(ket, v_aug)
            stores.append((bb, h, st_new, m_new))
        for p in range(ML_HEADS // 2):
            y_ref[bb, :, p * LANE:(p + 1) * LANE] = jnp.where(low, tiles[2 * p], tiles[2 * p + 1])

    for bb, h, st_new, m_new in stores:
        st_ref[bb, h] = st_new
        m_ref[bb, h] = jnp.broadcast_to(m_new, m_ref.shape[2:])


def _mlstm(big, kt, small, par, n_ctx):
    bsz, s_len, _ = big.shape
    ncc, nc = n_ctx // CHUNK, s_len // CHUNK
    nb = _scan_batch(bsz)
    ch = functools.partial(_scan_chunk, ncc=ncc, nc=nc)

    def col_spec(col):
        return pl.BlockSpec((nb, CHUNK, ML_WIDTH), lambda b, d, t: (b, ch(d, t), col // ML_WIDTH))

    return pl.pallas_call(
        functools.partial(_mlstm_kernel, ncc=ncc, nc=nc),
        out_shape=jax.ShapeDtypeStruct((2, bsz, s_len, ML_WIDTH), F32),
        grid=(bsz // nb, 2, nc),
        in_specs=[col_spec(COL_MQ), col_spec(COL_MK), col_spec(COL_MV),
                  pl.BlockSpec((nb, ML_WIDTH, CHUNK), lambda b, d, t: (b, 0, ch(d, t))),
                  pl.BlockSpec((nb, CHUNK, LANE), lambda b, d, t: (b, ch(d, t), d)),
                  pl.BlockSpec((None, 8, LANE), lambda b, d, t: (d, 0, 0))],
        out_specs=pl.BlockSpec((None, nb, CHUNK, ML_WIDTH), lambda b, d, t: (d, b, ch(d, t), 0)),
        scratch_shapes=[pltpu.VMEM((nb, ML_HEADS, ML_HEAD_DIM, 2 * LANE), F32),
                        pltpu.VMEM((nb, ML_HEADS, 8, LANE), F32)],
        compiler_params=_cparams(("arbitrary", "arbitrary", "arbitrary")),
        name="mlstm",
    )(big, big, big, kt, small, par)


def _merge_kernel(ya_ref, yb_ref, z_ref, yc_ref, o_ref, h_ref, g1_ref, sh_ref, sc_ref, ng_ref,
                  sg_ref, mg_ref, bd_ref, wo_ref, rwh_ref, rwl_ref, rb_ref,
                  hout_ref, v_ref, gates_ref):
    z = z_ref[...].astype(F32)
    yb = (yb_ref[0] + yb_ref[1]) * (z * _sigmoid(z))
    yb = yb * lax.rsqrt(jnp.mean(yb * yb, axis=-1, keepdims=True) + EPS) * sg_ref[...]
    yc = _sigmoid(o_ref[...].astype(F32)) * (yc_ref[0] + yc_ref[1])
    yc = yc * lax.rsqrt(_segment_mean_sq(yc, bd_ref[...]) + EPS) * mg_ref[...]
    y = _dot(jnp.concatenate([ya_ref[...], yb.astype(BF16), yc.astype(BF16)], axis=1), wo_ref[...])
    hn = h_ref[...] + g1_ref[...] * y
    hout_ref[...] = hn
    v = hn * lax.rsqrt(jnp.mean(hn * hn, axis=-1, keepdims=True) + EPS) * ng_ref[...]
    v = v * (1.0 + sc_ref[...]) + sh_ref[...]
    v_hi, v_lo = _split2(v)
    v_ref[...] = v_hi
    rwh = rwh_ref[...]
    logits = (_dot_nt(rwh, v_hi) + _dot_nt(rwh, v_lo) + _dot_nt(rwl_ref[...], v_hi))[:N_EXPERTS] + rb_ref[...]

    row = lax.broadcasted_iota(jnp.int32, logits.shape, 0).astype(F32)
    work = logits
    top = None
    gates = jnp.zeros_like(logits)
    den = jnp.zeros((1, logits.shape[1]), F32)
    for _ in range(TOP_K):
        mx = jnp.max(work, axis=0, keepdims=True)
        idx = jnp.min(jnp.where(work == mx, row, float(N_EXPERTS)), axis=0, keepdims=True)
        hit = row == idx
        top = mx if top is None else top
        e = jnp.exp(mx - top)
        gates = gates + jnp.where(hit, e, 0.0)
        den = den + e
        work = jnp.where(hit, -jnp.inf, work)
    gates_ref[...] = jnp.concatenate(
        [gates / den, jnp.zeros((LANE - N_EXPERTS, logits.shape[1]), F32)], axis=0)


def _merge(ya, yb, yc, big, h, g1, sh2, sc2, ng, sg, mg, bd, wo, rwh, rwl, rb, n_ctx, ctx_out):
    bsz, s_len, d = h.shape
    tm = TOK_TILE
    first = 0 if ctx_out else n_ctx // tm
    n_out = s_len - first * tm
    mod_spec = pl.BlockSpec((None, None, 1, d), lambda b, i: (b, jnp.minimum(i + first, 1), 0, 0))
    full = lambda shape: pl.BlockSpec(shape, lambda b, i: (0,) * len(shape))
    pair = lambda w: pl.BlockSpec((2, None, tm, w), lambda b, i: (0, b, i + first, 0))
    nl = (s_len - n_ctx) // tm

    def flat_row(b, i):
        if ctx_out:
            return jnp.where(i == 0, bsz * nl + b, (i - 1) * bsz + b)
        return i * bsz + b

    flat_tok = lambda w: pl.BlockSpec((tm, w), lambda b, i: (flat_row(b, i), 0))
    return pl.pallas_call(
        _merge_kernel,
        out_shape=(jax.ShapeDtypeStruct((bsz, n_out, d), F32),
                   jax.ShapeDtypeStruct((bsz * n_out, d), BF16),
                   jax.ShapeDtypeStruct((LANE, bsz * n_out), F32)),
        grid=(bsz, n_out // tm),
        in_specs=[pl.BlockSpec((None, tm, NA_WIDTH), lambda b, i: (b, i, 0)),
                  pair(SSM_WIDTH),
                  pl.BlockSpec((None, tm, SSM_WIDTH), lambda b, i: (b, i + first, COL_Z // SSM_WIDTH)),
                  pair(ML_WIDTH),
                  pl.BlockSpec((None, tm, ML_WIDTH), lambda b, i: (b, i + first, COL_MO // ML_WIDTH)),
                  pl.BlockSpec((None, tm, d), lambda b, i: (b, i + first, 0)),
                  mod_spec, mod_spec, mod_spec,
                  full((1, d)), full((1, SSM_WIDTH)), full((1, ML_WIDTH)), full(bd.shape),
                  full(wo.shape), full(rwh.shape), full(rwl.shape), full(rb.shape)],
        out_specs=(pl.BlockSpec((None, tm, d), lambda b, i: (b, i, 0)), flat_tok(d),
                   pl.BlockSpec((LANE, tm), lambda b, i: (0, flat_row(b, i)))),
        compiler_params=_cparams(("arbitrary", "arbitrary")),
        name="merge",
    )(ya, yb, big, yc, big, h, g1, sh2, sc2, ng, sg, mg, bd, wo, rwh, rwl, rb)


def _moe_kernel(x_ref, gates_ref, w1_ref, b1_ref, w2_ref, b2_ref, f_ref, rank_ref, *, nblk):
    e = pl.program_id(1)
    tb = x_ref.shape[0] // nblk
    d_ff = w2_ref.shape[0]

    @pl.when(e == 0)
    def _():
        f_ref[...] = jnp.zeros_like(f_ref)
        ii = lax.broadcasted_iota(jnp.int32, (tb, tb), 0)
        jj = lax.broadcasted_iota(jnp.int32, (tb, tb), 1)
        tri = jnp.where(ii <= jj, 1.0, 0.0).astype(BF16)
        for k in range(nblk):
            picked = gates_ref[:, k * tb:(k + 1) * tb] > 0.0
            incl = _dot(jnp.where(picked, 1.0, 0.0).astype(BF16), tri)
            rank_ref[k] = jnp.where(picked, incl - 1.0, -1.0)

    def compact(k, s):
        slot = rank_ref[k, pl.ds(e, 1), :]
        gate = gates_ref[pl.ds(e, 1), k * tb:(k + 1) * tb]
        row = (lax.broadcasted_iota(jnp.int32, (MOE_CAP, tb), 0) + s * MOE_CAP).astype(F32)
        hit = slot == row
        onehot = jnp.where(hit, 1.0, 0.0).astype(BF16)
        xe = _dot(onehot, x_ref[k * tb:(k + 1) * tb, :]).astype(BF16)
        w = jnp.sum(jnp.where(hit, gate, 0.0), axis=-1, keepdims=True)
        return onehot, xe, w

    def expert(xe):
        hid = _dot(xe, w1_ref[...]) + b1_ref[...]
        x_glu = jnp.minimum(hid[:, :d_ff], SWIGLU_LIMIT)
        x_lin = jnp.clip(hid[:, d_ff:], -SWIGLU_LIMIT, SWIGLU_LIMIT)
        act = x_glu * _sigmoid(SWIGLU_ALPHA * x_glu) * (x_lin + 1.0)
        return _dot(act.astype(BF16), w2_ref[...]) + b2_ref[...]

    def scatter(k, onehot, yw):
        f_ref[k * tb:(k + 1) * tb, :] += _dot_tn(onehot, yw.astype(BF16))

    first = [compact(k, 0) for k in range(nblk)]
    y = expert(jnp.concatenate([xe for _, xe, _ in first], axis=0))
    for k, (onehot, _, w) in enumerate(first):
        scatter(k, onehot, y[k * MOE_CAP:(k + 1) * MOE_CAP] * w)

    for k in range(nblk):
        count = jnp.max(rank_ref[k, pl.ds(e, 1), :]).astype(jnp.int32) + 1

        def overflow(s, carry, k=k):
            onehot, xe, w = compact(k, s)
            scatter(k, onehot, expert(xe) * w)
            return carry

        lax.fori_loop(1, (count + MOE_CAP - 1) // MOE_CAP, overflow, 0)


def _moe(x, gates, w1, b1, w2, b2, layer, first_tok, n_tok):
    d = x.shape[1]
    nblk = MOE_GROUP if n_tok % (MOE_GROUP * MOE_BLOCK) == 0 else 1
    tb = math.gcd(n_tok, MOE_BLOCK)
    rows = nblk * tb
    d_ff = w2.shape[1]
    off = first_tok // rows
    assert first_tok % rows == 0 and n_tok % rows == 0
    expert = lambda i, e: (layer * N_EXPERTS + e, 0, 0)
    return pl.pallas_call(
        functools.partial(_moe_kernel, nblk=nblk),
        out_shape=jax.ShapeDtypeStruct((n_tok, d), F32),
        grid=(n_tok // rows, N_EXPERTS),
        in_specs=[pl.BlockSpec((rows, d), lambda i, e: (i + off, 0)),
                  pl.BlockSpec((LANE, rows), lambda i, e: (0, i + off)),
                  pl.BlockSpec((None, d, 2 * d_ff), expert),
                  pl.BlockSpec((None, 1, 2 * d_ff), expert),
                  pl.BlockSpec((None, d_ff, d), expert),
                  pl.BlockSpec((None, 1, d), expert)],
        out_specs=pl.BlockSpec((rows, d), lambda i, e: (i, 0)),
        scratch_shapes=[pltpu.VMEM((nblk, LANE, tb), F32)],
        compiler_params=pltpu.CompilerParams(dimension_semantics=("arbitrary", "arbitrary"),
                                             vmem_limit_bytes=MOE_VMEM_LIMIT),
        name="moe",
    )(x, gates, w1, b1, w2, b2)


def _resid_kernel(h_ref, g_ref, fl_ref, *rest, first):
    if first == 0:
        fc_ref, o_ref = rest
        f = jnp.where(pl.program_id(1) == 0, fc_ref[...], fl_ref[...])
    else:
        (o_ref,) = rest
        f = fl_ref[...]
    o_ref[...] = h_ref[...] + g_ref[...] * f


def _resid(h, g2, f_lat, f_ctx):
    bsz, n_tok, d = h.shape
    tm = TOK_TILE
    first = 0 if f_ctx is not None else 1
    in_specs = [pl.BlockSpec((None, tm, d), lambda b, i: (b, i, 0)),
                pl.BlockSpec((None, None, 1, d), lambda b, i: (b, jnp.minimum(i + first, 1), 0, 0)),
                pl.BlockSpec((tm, d), lambda b, i: (jnp.maximum(i + first - 1, 0) * bsz + b, 0))]
    args = [h, g2, f_lat]
    if f_ctx is not None:
        in_specs.append(pl.BlockSpec((tm, d), lambda b, i: (b, 0)))
        args.append(f_ctx)
    return pl.pallas_call(
        functools.partial(_resid_kernel, first=first),
        out_shape=jax.ShapeDtypeStruct((bsz, n_tok, d), F32),
        grid=(bsz, n_tok // tm),
        in_specs=in_specs,
        out_specs=pl.BlockSpec((None, tm, d), lambda b, i: (b, i, 0)),
        compiler_params=_cparams(("arbitrary", "arbitrary")),
        name="resid",
    )(*args)


def _in_weight_perm():
    o = IN_OFFSETS
    big = np.concatenate([np.arange(o[0], o[5]), np.arange(o[6], o[10])])
    small = np.full((2, LANE), -1, np.int64)
    for d in range(2):
        for h in range(SSM_HEADS):
            small[d, h] = o[5] + d * SSM_HEADS + h
        for h in range(ML_HEADS):
            small[d, ML_HEADS + h] = o[10] + d * ML_HEADS + h
            small[d, 2 * ML_HEADS + h] = o[11] + d * ML_HEADS + h
    return np.concatenate([big, small.reshape(-1)])


def _rope_tables(n_ctx, n_lat):
    pos = np.arange(n_lat)
    row = (pos // GRID_W).astype(np.float32)
    col = (pos % GRID_W).astype(np.float32)
    n_freq = ML_HEAD_DIM // 4
    inv_freq = jnp.asarray(ROPE_THETA, F32) ** (-jnp.arange(n_freq, dtype=F32) / n_freq)
    ang = jnp.concatenate([jnp.asarray(row)[:, None] * inv_freq, jnp.asarray(col)[:, None] * inv_freq], axis=-1)
    cos, sin = jnp.cos(ang), jnp.sin(ang)
    cosf = jnp.tile(jnp.concatenate([cos, cos], axis=-1), (1, ML_HEADS))
    sinf = jnp.tile(jnp.concatenate([-sin, sin], axis=-1), (1, ML_HEADS))
    cosf = jnp.concatenate([jnp.ones((n_ctx, ML_WIDTH), F32), cosf], axis=0)
    sinf = jnp.concatenate([jnp.zeros((n_ctx, ML_WIDTH), F32), sinf], axis=0)
    return cosf, sinf


def _lane_rows(rows):
    out = jnp.zeros((8, LANE), F32)
    for r, (off, vec) in enumerate(rows):
        out = out.at[r, off:off + vec.shape[0]].set(vec.astype(F32))
    return out


def kernel(x, c, ctx, c_ctx, ada_w, ada_b, norm_g, w_in, na_qk_g, na_rpb, ssm_conv_w, ssm_conv_b,
           ssm_dt_bias, ssm_a_log, ssm_d, ssm_norm_g, ml_gate_b, ml_norm_g, w_out,
           router_w, router_b, exp_w1, exp_b1, exp_w2, exp_b2):
    bsz, n_lat, d = x.shape
    n_ctx = ctx.shape[1]
    rows = n_lat // GRID_W
    depth = w_in.shape[0]
    assert bsz < 8 and n_ctx == TOK_TILE and n_lat % TOK_TILE == 0 and rows >= NA_KROWS

    h = jnp.concatenate([ctx, x], axis=1)
    cond = jnp.zeros((8, d), F32).at[:bsz].set(c).at[bsz].set(c_ctx)
    cosf, sinf = _rope_tables(n_ctx, n_lat)
    perm = _in_weight_perm()
    seg = np.arange(2 * LANE) // NA_HEAD_DIM
    bd = jnp.asarray(seg[:, None] == seg[None, :], BF16)

    n_exp = exp_w1.shape[0] * exp_w1.shape[1]
    w1_all = exp_w1.astype(BF16).reshape((n_exp,) + exp_w1.shape[2:])
    w2_all = exp_w2.astype(BF16).reshape((n_exp,) + exp_w2.shape[2:])
    b1_all = exp_b1.reshape(n_exp, 1, -1)
    b2_all = exp_b2.reshape(n_exp, 1, -1)

    out = None
    for layer in range(depth):
        ctx_out = layer < depth - 1
        mod = _ada(cond, ada_w[layer], ada_b[layer])
        mods = []
        for m in jnp.split(mod, 6, axis=-1):
            lat = m[:bsz]
            cx = jnp.broadcast_to(m[bsz][None], (bsz, d))
            mods.append(jnp.stack([cx, lat], axis=1).reshape(bsz, 2, 1, d))
        sh1, sc1, g1, sh2, sc2, g2 = mods

        w = jnp.where(perm[None, :] >= 0, w_in[layer][:, np.maximum(perm, 0)], 0.0).astype(BF16)
        qkg = jnp.concatenate([jnp.tile(na_qk_g[layer, 0] * NA_HEAD_DIM ** -0.5, NA_HEADS),
                               jnp.tile(na_qk_g[layer, 1], NA_HEADS)]).reshape(1, 2 * NA_WIDTH)
        big, small, kt = _inproj(h, sh1, sc1, norm_g[layer, 0].reshape(1, d), w, qkg, bd, cosf, sinf)

        bias = _attention_bias(na_rpb[layer], rows)
        ya = _attention(big, bias, n_ctx, rows, ctx_out)

        conv_w = jnp.zeros((8, SSM_CONV_CH), F32).at[:SSM_CONV].set(ssm_conv_w[layer])
        ssm_par = jnp.stack([_lane_rows([(0, ssm_dt_bias[layer, dr]), (0, ssm_a_log[layer, dr]),
                                         (0, ssm_d[layer, dr])]) for dr in range(2)])
        yb = _ssd(big, small, conv_w, ssm_conv_b[layer].reshape(1, -1), ssm_par, n_ctx)

        gate_row = [jnp.concatenate([jnp.zeros((ML_HEADS,), F32), ml_gate_b[layer, dr, 0],
                                     ml_gate_b[layer, dr, 1]]) for dr in range(2)]
        ml_par = jnp.stack([_lane_rows([(0, gate_row[dr])]) for dr in range(2)])
        yc = _mlstm(big, kt, small, ml_par, n_ctx)

        rw = jnp.zeros((LANE, d), F32).at[:N_EXPERTS].set(router_w[layer].T)
        rwh = rw.astype(BF16)
        rwl = (rw - rwh.astype(F32)).astype(BF16)
        rb = jnp.broadcast_to(router_b[layer].astype(F32)[:, None], (N_EXPERTS, TOK_TILE))
        hn, v, gates = _merge(ya, yb, yc, big, h, g1, sh2, sc2, norm_g[layer, 1].reshape(1, d),
                              ssm_norm_g[layer].reshape(1, -1), ml_norm_g[layer].reshape(1, -1), bd,
                              w_out[layer].astype(BF16), rwh, rwl, rb, n_ctx, ctx_out)

        n_lat_tok = bsz * n_lat
        f_lat = _moe(v, gates, w1_all, b1_all, w2_all, b2_all, layer, 0, n_lat_tok)
        f_ctx = (_moe(v, gates, w1_all, b1_all, w2_all, b2_all, layer, n_lat_tok, bsz * n_ctx)
                 if ctx_out else None)
        h = _resid(hn, g2, f_lat, f_ctx)
        out = h
    return out
```
